```python
import math
import jax, jax.numpy as jnp
from jax import lax
import numpy as np

D_MODEL = 1024
BATCH = 2
SEQ = 8192
DEPTH = 2
DEC_BATCH = 128
DEC_SEQ = 8
PAST_LEN = 8192
PAGE_SIZE = 128

N_META = 16
CONV_CH = 512
CONV_WIDTH = 31
RET_HEADS = 4
RET_DK = 64
RET_DV = 128
RET_CHUNK = 128
SWA_HEADS = 8
SWA_KV_HEADS = 2
SWA_GROUP = SWA_HEADS // SWA_KV_HEADS
SWA_HD = 64
WINDOW = 128
N_BUCKETS = 32
REL_MAX_DIST = 128
PEER_HEADS = 8
N_KEYS = 128
N_EXPERTS = N_KEYS * N_KEYS
PEER_TOPK = 16
PEER_DKEY = 128
PEER_HALF = PEER_DKEY // 2
PEER_BLOCK = 256

MIX_WIDTH = CONV_CH + RET_HEADS * RET_DV + SWA_HEADS * SWA_HD
SPLIT_SIZES = (2 * CONV_CH, RET_HEADS * RET_DK, RET_HEADS * RET_DK, RET_HEADS * RET_DV,
               RET_HEADS * RET_DV, SWA_HEADS * SWA_HD, SWA_KV_HEADS * SWA_HD, SWA_KV_HEADS * SWA_HD)
PROJ_COLS = 2 * CONV_CH + 2 * RET_HEADS * RET_DK + 2 * RET_HEADS * RET_DV + SWA_HEADS * SWA_HD + 2 * SWA_KV_HEADS * SWA_HD
EPS = 1e-6
NEG = -1e30

kernel_name = "hymba_conv_retention_swa_peer_step"

F32 = jnp.float32


def _rmsnorm(x, g):
    xf = x.astype(F32)
    y = xf * lax.rsqrt(jnp.mean(xf * xf, axis=-1, keepdims=True) + EPS)
    return (y * g.astype(F32)).astype(x.dtype)


def _layernorm(xf, g, b):
    mu = jnp.mean(xf, axis=-1, keepdims=True)
    var = jnp.mean(jnp.square(xf - mu), axis=-1, keepdims=True)
    return (xf - mu) * lax.rsqrt(var + EPS) * g.astype(F32) + b.astype(F32)


def _project(a, w_in):
    p = a @ w_in
    parts, o = [], 0
    for size in SPLIT_SIZES:
        parts.append(p[..., o:o + size])
        o += size
    return parts


def _t5_bucket(dist):
    max_exact = N_BUCKETS // 2
    df = jnp.maximum(dist, 1).astype(F32)
    large = max_exact + (jnp.log(df / max_exact) / math.log(REL_MAX_DIST / max_exact)
                         * (N_BUCKETS - max_exact)).astype(jnp.int32)
    large = jnp.minimum(large, N_BUCKETS - 1)
    return jnp.where(dist < max_exact, dist, large)


def _glu(c):
    return c[..., :CONV_CH] * jax.nn.sigmoid(c[..., CONV_CH:])


def _dwconv(x, w, left_pad):
    return lax.conv_general_dilated(x, w.astype(x.dtype)[:, None, :], (1,), [(left_pad, 0)],
                                    dimension_numbers=("NWC", "WIO", "NWC"),
                                    feature_group_count=CONV_CH)


def _conv_post(y, b, g, beta):
    yf = _layernorm(y.astype(F32) + b.astype(F32), g, beta)
    return jax.nn.silu(yf).astype(y.dtype)


def _ret_log_gamma():
    return jnp.log(1.0 - 2.0 ** (-5.0 - jnp.arange(RET_HEADS, dtype=F32)))


def _rotary(x, pos):
    half = x.shape[-1] // 2
    inv = 1.0 / (10000.0 ** (jnp.arange(half, dtype=F32) / half))
    ang = pos.astype(F32)[:, None] * inv[None]
    cos, sin = jnp.cos(ang)[:, None, :], jnp.sin(ang)[:, None, :]
    x1, x2 = x[..., :half], x[..., half:]
    return jnp.concatenate([x1 * cos - x2 * sin, x1 * sin + x2 * cos], axis=-1)


def _ret_pre(rq, rk, rv, pos):
    b, L = rq.shape[:2]
    q = _rotary(rq.reshape(b, L, RET_HEADS, RET_DK).astype(F32), pos)
    k = _rotary(rk.reshape(b, L, RET_HEADS, RET_DK).astype(F32), pos) * (RET_DK ** -0.5)
    v = rv.reshape(b, L, RET_HEADS, RET_DV).astype(F32)
    return q.transpose(0, 2, 1, 3), k.transpose(0, 2, 1, 3), v.transpose(0, 2, 1, 3)


def _ret_chunk(S, q, k, v, lg):
    C = q.shape[2]
    i = jnp.arange(C, dtype=F32)
    diff = i[:, None] - i[None, :]
    decay = jnp.where(diff >= 0, jnp.exp(jnp.maximum(diff, 0.0)[None] * lg[:, None, None]), 0.0)
    inner = jnp.einsum("bhid,bhjd->bhij", q, k) * decay
    o = (jnp.einsum("bhij,bhje->bhie", inner, v)
         + jnp.exp((i[None] + 1.0) * lg[:, None])[None, :, :, None] * jnp.einsum("bhid,bhde->bhie", q, S))
    k_dec = k * jnp.exp((C - 1.0 - i)[None] * lg[:, None])[None, :, :, None]
    S_new = jnp.exp(C * lg)[None, :, None, None] * S + jnp.einsum("bhjd,bhje->bhde", k_dec, v)
    return o, S_new


def _retention_prompt(q, k, v, lg):
    b = q.shape[0]
    S0 = jnp.zeros((b, RET_HEADS, RET_DK, RET_DV), F32)
    o_m, S = _ret_chunk(S0, q[:, :, :N_META], k[:, :, :N_META], v[:, :, :N_META], lg)

    def to_chunks(t):
        t = t[:, :, N_META:]
        n = t.shape[2] // RET_CHUNK
        return jnp.moveaxis(t.reshape(b, RET_HEADS, n, RET_CHUNK, t.shape[-1]), 2, 0)

    def step(S, xs):
        o, S = _ret_chunk(S, xs[0], xs[1], xs[2], lg)
        return S, o

    S, o_c = lax.scan(step, S, (to_chunks(q), to_chunks(k), to_chunks(v)))
    o_c = jnp.moveaxis(o_c, 0, 2).reshape(b, RET_HEADS, -1, RET_DV)
    return jnp.concatenate([o_m, o_c], axis=2), S


def _ret_post(o, rg, g, beta):
    b, _, L, _ = o.shape
    o = o.transpose(0, 2, 1, 3)
    mu = jnp.mean(o, axis=-1, keepdims=True)
    var = jnp.mean(jnp.square(o - mu), axis=-1, keepdims=True)
    y = ((o - mu) * lax.rsqrt(var + EPS)).reshape(b, L, RET_HEADS * RET_DV)
    y = y * g.astype(F32) + beta.astype(F32)
    return (y * jax.nn.silu(rg.astype(F32))).astype(rg.dtype)


def _swa_pre(sq, sk, sv, qn, kn):
    b, L = sq.shape[:2]
    q = _rmsnorm(sq.reshape(b, L, SWA_KV_HEADS, SWA_GROUP, SWA_HD), qn)
    k = _rmsnorm(sk.reshape(b, L, SWA_KV_HEADS, SWA_HD), kn)
    v = sv.reshape(b, L, SWA_KV_HEADS, SWA_HD)
    return q, k, v


def _sink_attention(q, k, v, qpos, kpos, is_meta, rel_bias, sinks):
    s = jnp.einsum("...qkgd,...skd->...kgqs", q, k).astype(F32) * (SWA_HD ** -0.5)
    dist = qpos[..., :, None] - kpos[..., None, :]
    valid = jnp.where(is_meta, kpos[..., None, :] <= qpos[..., :, None],
                      (dist >= 0) & (dist < WINDOW) & (kpos[..., None, :] >= N_META))
    bias = jnp.take(rel_bias.astype(F32), _t5_bucket(jnp.maximum(dist, 0)), axis=0)
    bias = jnp.moveaxis(bias, -1, -3)
    bias = bias.reshape(bias.shape[:-3] + (SWA_KV_HEADS, SWA_GROUP) + bias.shape[-2:])
    s = jnp.where(valid[..., None, None, :, :], s + bias, NEG)
    sink = jnp.broadcast_to(sinks.astype(F32).reshape(SWA_KV_HEADS, SWA_GROUP, 1, 1), s.shape[:-1] + (1,))
    p = jax.nn.softmax(jnp.concatenate([s, sink], axis=-1), axis=-1)[..., :-1]
    return jnp.einsum("...kgqs,...skd->...qkgd", p.astype(v.dtype), v)


def _swa_prompt(q, k, v, rel_bias, sinks):
    b, L = q.shape[:2]
    nb = -(-L // WINDOW)
    pad = nb * WINDOW - L

    def padt(t):
        return jnp.pad(t, [(0, 0), (0, pad)] + [(0, 0)] * (t.ndim - 2))

    qb = padt(q).reshape(b, nb, WINDOW, SWA_KV_HEADS, SWA_GROUP, SWA_HD)

    def band(t):
        tb = padt(t).reshape(b, nb, WINDOW, SWA_KV_HEADS, SWA_HD)
        prev = jnp.pad(tb, [(0, 0), (1, 0), (0, 0), (0, 0), (0, 0)])[:, :-1]
        meta = jnp.broadcast_to(t[:, None, :N_META], (b, nb, N_META, SWA_KV_HEADS, SWA_HD))
        return jnp.concatenate([meta, prev, tb], axis=2)

    blk = jnp.arange(nb)[:, None]
    qpos = blk * WINDOW + jnp.arange(WINDOW)[None]
    kpos = jnp.concatenate([jnp.broadcast_to(jnp.arange(N_META)[None], (nb, N_META)),
                            (blk - 1) * WINDOW + jnp.arange(2 * WINDOW)[None]], axis=1)
    is_meta = jnp.arange(N_META + 2 * WINDOW) < N_META
    o = _sink_attention(qb, band(k), band(v), qpos, kpos, is_meta, rel_bias, sinks)
    return o.reshape(b, nb * WINDOW, SWA_HEADS * SWA_HD)[:, :L]


def _swa_sample(q, k, v, meta_kv, win_kv, rel_bias, sinks):
    b, S = q.shape[:2]
    new_kv = jnp.stack([k, v], axis=2)
    kv = jnp.concatenate([meta_kv.astype(new_kv.dtype), win_kv.astype(new_kv.dtype), new_kv], axis=1)
    qpos = PAST_LEN + jnp.arange(S)
    kpos = jnp.concatenate([jnp.arange(N_META), PAST_LEN - WINDOW + jnp.arange(WINDOW), qpos])
    is_meta = jnp.arange(kpos.shape[0]) < N_META
    o = _sink_attention(q, kv[:, :, 0], kv[:, :, 1], qpos, kpos, is_meta, rel_bias, sinks)
    new_win = jnp.concatenate([win_kv.astype(new_kv.dtype), new_kv], axis=1)[:, -WINDOW:]
    return o.reshape(b, S, SWA_HEADS * SWA_HD), new_win


def _peer(x, wq, keys, u_tab, v_tab):
    shp = x.shape
    t = x.reshape(-1, D_MODEL)
    n = t.shape[0]
    nb = -(-n // PEER_BLOCK)
    t = jnp.pad(t, ((0, nb * PEER_BLOCK - n), (0, 0))).reshape(nb, PEER_BLOCK, D_MODEL)

    def blk(xb):
        q = (xb @ wq).reshape(PEER_BLOCK, PEER_HEADS, 2, PEER_HALF)
        s = jnp.einsum("thcd,cnd->thcn", q, keys).astype(F32)
        sv, si = lax.top_k(s, PEER_TOPK)
        cand = (sv[:, :, 0, :, None] + sv[:, :, 1, None, :]).reshape(PEER_BLOCK, PEER_HEADS, PEER_TOPK * PEER_TOPK)
        top, flat = lax.top_k(cand, PEER_TOPK)
        i1 = jnp.take_along_axis(si[:, :, 0], flat // PEER_TOPK, axis=-1)
        i2 = jnp.take_along_axis(si[:, :, 1], flat % PEER_TOPK, axis=-1)
        eidx = i1 * N_KEYS + i2
        g = jax.nn.softmax(top, axis=-1).astype(xb.dtype)
        u = jnp.take(u_tab, eidx, axis=0)
        act = jax.nn.gelu(jnp.einsum("td,thkd->thk", xb, u), approximate=False)
        vv = jnp.take(v_tab, eidx, axis=0)
        return jnp.einsum("thk,thkd->td", g * act, vv)

    out = lax.map(blk, t).reshape(-1, D_MODEL)[:n]
    return out.reshape(shp)


def _finish(h, conv_o, ret_o, swa_o, lw):
    h = h + jnp.concatenate([conv_o, ret_o, swa_o], axis=-1) @ lw["w_out"]
    return h + _peer(_rmsnorm(h, lw["norm_ffn"]), lw["peer_wq"], lw["peer_keys"], lw["peer_u"], lw["peer_v"])


def _layer_prompt(h, pos, lw, rel_bias):
    c_in, rq, rk, rv, rg, sq, sk, sv = _project(_rmsnorm(h, lw["norm_mix"]), lw["w_in"])
    glu = _glu(c_in)
    conv_o = _conv_post(_dwconv(glu, lw["conv_w"], CONV_WIDTH - 1), lw["conv_b"], lw["conv_ln_g"], lw["conv_ln_b"])
    conv_state = glu[:, -(CONV_WIDTH - 1):]
    q, k, v = _ret_pre(rq, rk, rv, pos)
    o, ret_state = _retention_prompt(q, k, v, _ret_log_gamma())
    ret_o = _ret_post(o, rg, lw["ret_gn_g"], lw["ret_gn_b"])
    q, k, v = _swa_pre(sq, sk, sv, lw["swa_q_norm"], lw["swa_k_norm"])
    swa_o = _swa_prompt(q, k, v, rel_bias, lw["swa_sinks"])
    kv = jnp.stack([k, v], axis=2)
    h = _finish(h, conv_o, ret_o, swa_o, lw)
    return h, kv[:, :N_META], kv[:, -WINDOW:], ret_state.astype(h.dtype), conv_state


def _layer_sample(h, pos, meta_kv, win_kv, ret_state, conv_state, lw, rel_bias):
    c_in, rq, rk, rv, rg, sq, sk, sv = _project(_rmsnorm(h, lw["norm_mix"]), lw["w_in"])
    glu = _glu(c_in)
    xin = jnp.concatenate([conv_state.astype(glu.dtype), glu], axis=1)
    conv_o = _conv_post(_dwconv(xin, lw["conv_w"], 0), lw["conv_b"], lw["conv_ln_g"], lw["conv_ln_b"])
    new_conv = xin[:, -(CONV_WIDTH - 1):]
    q, k, v = _ret_pre(rq, rk, rv, pos)
    o, new_ret = _ret_chunk(ret_state.astype(F32), q, k, v, _ret_log_gamma())
    ret_o = _ret_post(o, rg, lw["ret_gn_g"], lw["ret_gn_b"])
    q, k, v = _swa_pre(sq, sk, sv, lw["swa_q_norm"], lw["swa_k_norm"])
    swa_o, new_win = _swa_sample(q, k, v, meta_kv, win_kv, rel_bias, lw["swa_sinks"])
    h = _finish(h, conv_o, ret_o, swa_o, lw)
    return h, new_win, new_ret.astype(h.dtype), new_conv


def setup_inputs(seed: int = 0) -> dict:
    key = jax.random.key(seed)
    ks = jax.random.split(key, 26)

    def nrm(k, shape, scale):
        return jax.random.normal(k, shape, F32) * scale

    return {
        "x_prompt": nrm(ks[0], (BATCH, SEQ, D_MODEL), 1.0),
        "x_sample": nrm(ks[1], (DEC_BATCH, DEC_SEQ, D_MODEL), 1.0),
        "cache_meta_kv": nrm(ks[2], (DEPTH, DEC_BATCH, N_META, 2, SWA_KV_HEADS, SWA_HD), 1.0),
        "cache_swa_kv": nrm(ks[3], (DEPTH, DEC_BATCH, WINDOW, 2, SWA_KV_HEADS, SWA_HD), 1.0),
        "state_ret": nrm(ks[4], (DEPTH, DEC_BATCH, RET_HEADS, RET_DK, RET_DV), 0.1),
        "state_conv": nrm(ks[5], (DEPTH, DEC_BATCH, CONV_WIDTH - 1, CONV_CH), 0.5),
        "meta_tokens": nrm(ks[6], (N_META, D_MODEL), 1.0),
        "rel_bias": nrm(ks[7], (N_BUCKETS, SWA_HEADS), 0.1),
        "norm_mix": 1.0 + nrm(ks[8], (DEPTH, D_MODEL), 0.02),
        "w_in": nrm(ks[9], (DEPTH, D_MODEL, PROJ_COLS), D_MODEL ** -0.5),
        "conv_w": nrm(ks[10], (DEPTH, CONV_WIDTH, CONV_CH), CONV_WIDTH ** -0.5),
        "conv_b": nrm(ks[11], (DEPTH, CONV_CH), 0.02),
        "conv_ln_g": 1.0 + nrm(ks[12], (DEPTH, CONV_CH), 0.02),
        "conv_ln_b": nrm(ks[13], (DEPTH, CONV_CH), 0.02),
        "ret_gn_g": 1.0 + nrm(ks[14], (DEPTH, RET_HEADS * RET_DV), 0.02),
        "ret_gn_b": nrm(ks[15], (DEPTH, RET_HEADS * RET_DV), 0.02),
        "swa_q_norm": 1.0 + nrm(ks[16], (DEPTH, SWA_HD), 0.02),
        "swa_k_norm": 1.0 + nrm(ks[17], (DEPTH, SWA_HD), 0.02),
        "swa_sinks": nrm(ks[18], (DEPTH, SWA_HEADS), 0.5),
        "w_out": nrm(ks[19], (DEPTH, MIX_WIDTH, D_MODEL), MIX_WIDTH ** -0.5),
        "norm_ffn": 1.0 + nrm(ks[20], (DEPTH, D_MODEL), 0.02),
        "peer_wq": nrm(ks[21], (DEPTH, D_MODEL, PEER_HEADS * PEER_DKEY), D_MODEL ** -0.5),
        "peer_keys": nrm(ks[22], (DEPTH, 2, N_KEYS, PEER_HALF), PEER_HALF ** -0.5),
        "peer_u": nrm(ks[23], (DEPTH, N_EXPERTS, D_MODEL), D_MODEL ** -0.5),
        "peer_v": nrm(ks[24], (DEPTH, N_EXPERTS, D_MODEL), PEER_HEADS ** -0.5),
    }


def reference(x_prompt, x_sample, cache_meta_kv, cache_swa_kv, state_ret, state_conv,
              meta_tokens, rel_bias, norm_mix, w_in, conv_w, conv_b, conv_ln_g, conv_ln_b,
              ret_gn_g, ret_gn_b, swa_q_norm, swa_k_norm, swa_sinks, w_out, norm_ffn,
              peer_wq, peer_keys, peer_u, peer_v):
    b = x_prompt.shape[0]
    meta = jnp.broadcast_to(meta_tokens.astype(x_prompt.dtype)[None], (b, N_META, D_MODEL))
    hp = jnp.concatenate([meta, x_prompt], axis=1)
    hs = x_sample
    pos_p = jnp.arange(hp.shape[1])
    pos_s = PAST_LEN + jnp.arange(hs.shape[1])
    meta_p, win_p, ret_p, conv_p = [], [], [], []
    win_s, ret_s, conv_s = [], [], []
    for l in range(DEPTH):
        lw = {
            "norm_mix": norm_mix[l], "w_in": w_in[l], "conv_w": conv_w[l], "conv_b": conv_b[l],
            "conv_ln_g": conv_ln_g[l], "conv_ln_b": conv_ln_b[l], "ret_gn_g": ret_gn_g[l],
            "ret_gn_b": ret_gn_b[l], "swa_q_norm": swa_q_norm[l], "swa_k_norm": swa_k_norm[l],
            "swa_sinks": swa_sinks[l], "w_out": w_out[l], "norm_ffn": norm_ffn[l],
            "peer_wq": peer_wq[l], "peer_keys": peer_keys[l], "peer_u": peer_u[l], "peer_v": peer_v[l],
        }
        hp, mkv, wkv, rst, cst = _layer_prompt(hp, pos_p, lw, rel_bias)
        meta_p.append(mkv); win_p.append(wkv); ret_p.append(rst); conv_p.append(cst)
        hs, wkv_s, rst_s, cst_s = _layer_sample(hs, pos_s, cache_meta_kv[l], cache_swa_kv[l],
                                                state_ret[l], state_conv[l], lw, rel_bias)
        win_s.append(wkv_s); ret_s.append(rst_s); conv_s.append(cst_s)
    y_prompt = hp[:, N_META:]
    return (y_prompt, hs, jnp.stack(meta_p), jnp.stack(win_p), jnp.stack(ret_p), jnp.stack(conv_p),
            jnp.stack(win_s), jnp.stack(ret_s), jnp.stack(conv_s))
```

```python
import functools
import math

import numpy as np
import jax
import jax.numpy as jnp
from jax import lax
from jax.experimental import pallas as pl
from jax.experimental.pallas import tpu as pltpu

F32 = jnp.float32
BF16 = jnp.bfloat16

D_MODEL = 1024
BATCH = 2
SEQ = 8192
DEPTH = 2
DEC_BATCH = 128
DEC_SEQ = 8
PAST_LEN = 8192
N_META = 16
CONV_CH = 512
CONV_WIDTH = 31
RET_HEADS = 4
RET_DK = 64
RET_DV = 128
SWA_HEADS = 8
SWA_KV_HEADS = 2
SWA_GROUP = SWA_HEADS // SWA_KV_HEADS
SWA_HD = 64
WINDOW = 128
N_BUCKETS = 32
REL_MAX_DIST = 128
PEER_HEADS = 8
N_KEYS = 128
N_EXPERTS = N_KEYS * N_KEYS
PEER_TOPK = 16
PEER_HALF = 64
EPS = 1e-6
NEG = -1e30

PROJ_COLS = 3328
C_CONV, C_RQ, C_RK, C_RV, C_RG, C_SQ, C_SK, C_SV = 0, 1024, 1280, 1536, 2048, 2560, 3072, 3200

L_REAL = N_META + SEQ
BLK = 128
N_BLK = 65
LP = N_BLK * BLK
NP_ROWS = BATCH * LP
NS_ROWS = DEC_BATCH * DEC_SEQ
N_ROWS = NP_ROWS + NS_ROWS
LAST_REAL = L_REAL - (N_BLK - 1) * BLK

TM = 384
TT = 768
EC = 1024
PEER_LANES = 256
CONV_T = 640
CONV_RB = 64
VMEM_LIMIT = 56 * 1024 * 1024


def _cparams(sem):
    return pltpu.CompilerParams(dimension_semantics=sem, vmem_limit_bytes=VMEM_LIMIT)


def _norm_proj_kernel(x_ref, g_ref, w_ref, o_ref):
    x = x_ref[...]
    ms = jnp.mean(x * x, axis=-1, keepdims=True)
    xn = x * lax.rsqrt(ms + EPS) * g_ref[...]
    o_ref[...] = jnp.dot(xn.astype(BF16), w_ref[...], preferred_element_type=F32)


def _norm_proj(h, g, w_bf):
    return pl.pallas_call(
        _norm_proj_kernel,
        out_shape=jax.ShapeDtypeStruct((N_ROWS, PROJ_COLS), F32),
        grid=(N_ROWS // TM,),
        in_specs=[pl.BlockSpec((TM, D_MODEL), lambda i: (i, 0)),
                  pl.BlockSpec((1, D_MODEL), lambda i: (0, 0)),
                  pl.BlockSpec((D_MODEL, PROJ_COLS), lambda i: (0, 0))],
        out_specs=pl.BlockSpec((TM, PROJ_COLS), lambda i: (i, 0)),
        compiler_params=_cparams(("parallel",)),
        name="norm_proj",
    )(h, g.reshape(1, D_MODEL), w_bf)


def _conv_kernel(c_ref, w_ref, b_ref, g_ref, beta_ref, o_ref, st_ref, xin_ref):
    t = pl.program_id(1)

    @pl.when(t == 0)
    def _():
        xin_ref[0:32, :] = jnp.zeros((32, CONV_CH), F32)

    @pl.when(t > 0)
    def _():
        xin_ref[0:32, :] = xin_ref[CONV_T:CONV_T + 32, :]

    c = c_ref[...]
    xin_ref[32:32 + CONV_T, :] = c[:, :CONV_CH] * jax.nn.sigmoid(c[:, CONV_CH:])

    w = w_ref[...]
    bias = b_ref[...]
    gam = g_ref[...]
    beta = beta_ref[...]
    for rb in range(CONV_T // CONV_RB):
        r0 = rb * CONV_RB
        acc = jnp.zeros((CONV_RB, CONV_CH), F32)
        for k in range(CONV_WIDTH):
            acc = acc + xin_ref[r0 + 2 + k:r0 + 2 + k + CONV_RB, :] * w[k:k + 1, :]
        y = acc + bias
        mu = jnp.mean(y, axis=-1, keepdims=True)
        d = y - mu
        var = jnp.mean(d * d, axis=-1, keepdims=True)
        yn = d * lax.rsqrt(var + EPS) * gam + beta
        o_ref[r0:r0 + CONV_RB, :] = yn * jax.nn.sigmoid(yn)

    @pl.when(t == pl.num_programs(1) - 1)
    def _():
        lo = 32 + (L_REAL - 32) - (LP - CONV_T)
        st_ref[0] = xin_ref[lo:lo + 32, :]


def _conv_prompt(p, w32, b, g, beta):
    nt = LP // CONV_T
    return pl.pallas_call(
        _conv_kernel,
        out_shape=(jax.ShapeDtypeStruct((N_ROWS, CONV_CH), F32),
                   jax.ShapeDtypeStruct((BATCH, 32, CONV_CH), F32)),
        grid=(BATCH, nt),
        in_specs=[pl.BlockSpec((CONV_T, 2 * CONV_CH), lambda bi, t: (bi * nt + t, 0)),
                  pl.BlockSpec((32, CONV_CH), lambda bi, t: (0, 0)),
                  pl.BlockSpec((1, CONV_CH), lambda bi, t: (0, 0)),
                  pl.BlockSpec((1, CONV_CH), lambda bi, t: (0, 0)),
                  pl.BlockSpec((1, CONV_CH), lambda bi, t: (0, 0))],
        out_specs=(pl.BlockSpec((CONV_T, CONV_CH), lambda bi, t: (bi * nt + t, 0)),
                   pl.BlockSpec((1, 32, CONV_CH), lambda bi, t: (bi, 0, 0))),
        scratch_shapes=[pltpu.VMEM((32 + CONV_T + 32, CONV_CH), F32)],
        compiler_params=_cparams(("arbitrary", "arbitrary")),
        name="conv_prompt",
    )(p, w32, b.reshape(1, -1), g.reshape(1, -1), beta.reshape(1, -1))


def _swap_halves(x, first_half):
    return jnp.where(first_half, pltpu.roll(x, x.shape[1] - 32, axis=1), pltpu.roll(x, 32, axis=1))


def _ret_kernel(q_ref, k_ref, v_ref, rg_ref, cos_ref, sin_ref, dmat_ref, dq_ref, dk_ref, sg_ref,
                gng_ref, gnb_ref, o_ref, st_ref, s_ref):
    j = pl.program_id(1)

    @pl.when(j == 0)
    def _():
        s_ref[...] = jnp.zeros_like(s_ref)

    cos = cos_ref[...]
    sin = sin_ref[...]
    lane = lax.broadcasted_iota(jnp.int32, (BLK, RET_HEADS * RET_DK), 1)
    first_half = (lane % RET_DK) < (RET_DK // 2)
    q = q_ref[...]
    k = k_ref[...]
    q = q * cos + _swap_halves(q, first_half) * sin
    k = (k * cos + _swap_halves(k, first_half) * sin) * (RET_DK ** -0.5)
    v = v_ref[...]
    rg = rg_ref[...]
    for h in range(RET_HEADS):
        qh = q[:, h * RET_DK:(h + 1) * RET_DK].astype(BF16)
        kh = k[:, h * RET_DK:(h + 1) * RET_DK]
        vh = v[:, h * RET_DV:(h + 1) * RET_DV].astype(BF16)
        s_old = s_ref[h]
        inner = lax.dot_general(qh, kh.astype(BF16), (((1,), (1,)), ((), ())),
                                preferred_element_type=F32) * dmat_ref[h]
        o = (jnp.dot(inner.astype(BF16), vh, preferred_element_type=F32)
             + dq_ref[h] * jnp.dot(qh, s_old.astype(BF16), preferred_element_type=F32))
        kdec_t = (kh * dk_ref[0, h]).T.astype(BF16)
        s_new = sg_ref[0, h, 0:1, :] * s_old + jnp.dot(kdec_t, vh, preferred_element_type=F32)
        s_ref[h] = s_new
        mu = jnp.mean(o, axis=-1, keepdims=True)
        d = o - mu
        var = jnp.mean(d * d, axis=-1, keepdims=True)
        y = d * lax.rsqrt(var + EPS) * gng_ref[:, h * RET_DV:(h + 1) * RET_DV] \
            + gnb_ref[:, h * RET_DV:(h + 1) * RET_DV]
        gate = rg[:, h * RET_DV:(h + 1) * RET_DV]
        o_ref[:, h * RET_DV:(h + 1) * RET_DV] = y * (gate * jax.nn.sigmoid(gate))

    @pl.when(j == pl.num_programs(1) - 1)
    def _():
        st_ref[0] = s_ref[...]


def _ret_tables():
    lg = jnp.log(1.0 - 2.0 ** (-5.0 - jnp.arange(RET_HEADS, dtype=F32)))
    i = jnp.arange(BLK, dtype=F32)
    diff = i[:, None] - i[None, :]
    dmat = jnp.where(diff >= 0, jnp.exp(jnp.maximum(diff, 0.0)[None] * lg[:, None, None]), 0.0)
    dq = jnp.broadcast_to(jnp.exp((i[None] + 1.0) * lg[:, None])[:, :, None], (RET_HEADS, BLK, RET_DV))

    def kdec(c_eff):
        e = jnp.where(i[None] < c_eff, jnp.exp((c_eff - 1.0 - i)[None] * lg[:, None]), 0.0)
        return jnp.broadcast_to(e[:, :, None], (RET_HEADS, BLK, RET_DK))

    def sgam(c_eff):
        return jnp.broadcast_to(jnp.exp(c_eff * lg)[:, None, None], (RET_HEADS, 8, RET_DV))

    dk = jnp.stack([kdec(float(BLK)), kdec(float(LAST_REAL))])
    sg = jnp.stack([sgam(float(BLK)), sgam(float(LAST_REAL))])
    return dmat, dq, dk, sg


def _rotary_tables(pos):
    half = RET_DK // 2
    inv = 1.0 / (10000.0 ** (jnp.arange(half, dtype=F32) / half))
    ang = pos.astype(F32)[:, None] * inv[None]
    cos, sin = jnp.cos(ang), jnp.sin(ang)
    cos_t = jnp.tile(jnp.concatenate([cos, cos], axis=-1), (1, RET_HEADS))
    sin_t = jnp.tile(jnp.concatenate([-sin, sin], axis=-1), (1, RET_HEADS))
    return cos_t, sin_t


def _ret_prompt(p, cos_t, sin_t, tabs, gn_g, gn_b):
    dmat, dq, dk, sg = tabs
    nb = N_BLK
    last = nb - 1
    return pl.pallas_call(
        _ret_kernel,
        out_shape=(jax.ShapeDtypeStruct((N_ROWS, RET_HEADS * RET_DV), F32),
                   jax.ShapeDtypeStruct((BATCH, RET_HEADS, RET_DK, RET_DV), F32)),
        grid=(BATCH, nb),
        in_specs=[pl.BlockSpec((BLK, 256), lambda b, j: (b * nb + j, C_RQ // 256)),
                  pl.BlockSpec((BLK, 256), lambda b, j: (b * nb + j, C_RK // 256)),
                  pl.BlockSpec((BLK, 512), lambda b, j: (b * nb + j, C_RV // 512)),
                  pl.BlockSpec((BLK, 512), lambda b, j: (b * nb + j, C_RG // 512)),
                  pl.BlockSpec((BLK, 256), lambda b, j: (j, 0)),
                  pl.BlockSpec((BLK, 256), lambda b, j: (j, 0)),
                  pl.BlockSpec((RET_HEADS, BLK, BLK), lambda b, j: (0, 0, 0)),
                  pl.BlockSpec((RET_HEADS, BLK, RET_DV), lambda b, j: (0, 0, 0)),
                  pl.BlockSpec((1, RET_HEADS, BLK, RET_DK), lambda b, j: (j // last, 0, 0, 0)),
                  pl.BlockSpec((1, RET_HEADS, 8, RET_DV), lambda b, j: (j // last, 0, 0, 0)),
                  pl.BlockSpec((1, 512), lambda b, j: (0, 0)),
                  pl.BlockSpec((1, 512), lambda b, j: (0, 0))],
        out_specs=(pl.BlockSpec((BLK, 512), lambda b, j: (b * nb + j, 0)),
                   pl.BlockSpec((1, RET_HEADS, RET_DK, RET_DV), lambda b, j: (b, 0, 0, 0))),
        scratch_shapes=[pltpu.VMEM((RET_HEADS, RET_DK, RET_DV), F32)],
        compiler_params=_cparams(("arbitrary", "arbitrary")),
        name="ret_prompt",
    )(p, p, p, p, cos_t, sin_t, dmat, dq, dk, sg, gn_g.reshape(1, -1), gn_b.reshape(1, -1))


def _group_rms(x, ones_bd, w):
    x2 = x * x
    hi = x2.astype(BF16)
    lo = (x2 - hi.astype(F32)).astype(BF16)
    ss = (jnp.dot(hi, ones_bd, preferred_element_type=F32)
          + jnp.dot(lo, ones_bd, preferred_element_type=F32))
    return x * lax.rsqrt(ss * (1.0 / SWA_HD) + EPS) * w


def _swa_kernel(sinks_ref, q_ref, kc_ref, vc_ref, kp_ref, vp_ref, km_ref, vm_ref, ones_ref,
                qn_ref, kn_ref, bc_ref, bp_ref, bm_ref, o_ref, kout_ref):
    j = pl.program_id(1)
    ones_q = ones_ref[...]
    ones_k = ones_ref[0:128, 0:128]
    qw = qn_ref[...]
    kw = kn_ref[...]
    q = _group_rms(q_ref[...], ones_q, qw) * (SWA_HD ** -0.5)
    kc = _group_rms(kc_ref[...], ones_k, kw)
    kp = _group_rms(kp_ref[...], ones_k, kw)
    km = _group_rms(km_ref[...], ones_k, kw)
    kout_ref[...] = kc
    vc = vc_ref[...].astype(BF16)
    vp = vp_ref[...].astype(BF16)
    vm = vm_ref[...].astype(BF16)

    qi = lax.broadcasted_iota(jnp.int32, (BLK, BLK), 0)
    kj = lax.broadcasted_iota(jnp.int32, (BLK, BLK), 1)
    valid_c = jnp.where(kj <= qi, j * BLK + kj, -1) >= N_META
    valid_p = jnp.where(kj > qi, (j - 1) * BLK + kj, -1) >= N_META
    qm = lax.broadcasted_iota(jnp.int32, (BLK, N_META), 0)
    mm = lax.broadcasted_iota(jnp.int32, (BLK, N_META), 1)
    valid_m = mm <= j * BLK + qm

    dn = (((1,), (1,)), ((), ()))
    for h in range(SWA_HEADS):
        kv = h // SWA_GROUP
        qh = q[:, h * SWA_HD:(h + 1) * SWA_HD].astype(BF16)
        ksl = slice(kv * SWA_HD, (kv + 1) * SWA_HD)
        s_c = lax.dot_general(qh, kc[:, ksl].astype(BF16), dn, preferred_element_type=F32)
        s_p = lax.dot_general(qh, kp[:, ksl].astype(BF16), dn, preferred_element_type=F32)
        s_m = lax.dot_general(qh, km[:, ksl].astype(BF16), dn, preferred_element_type=F32)
        s_c = jnp.where(valid_c, s_c + bc_ref[h], NEG)
        s_p = jnp.where(valid_p, s_p + bp_ref[h], NEG)
        s_m = jnp.where(valid_m, s_m + bm_ref[0, h], NEG)
        sink = sinks_ref[h]
        m = jnp.maximum(jnp.maximum(jnp.max(s_c, axis=-1, keepdims=True),
                                    jnp.max(s_p, axis=-1, keepdims=True)),
                        jnp.maximum(jnp.max(s_m, axis=-1, keepdims=True), sink))
        p_c = jnp.exp(s_c - m)
        p_p = jnp.exp(s_p - m)
        p_m = jnp.exp(s_m - m)
        den = (jnp.sum(p_c, axis=-1, keepdims=True) + jnp.sum(p_p, axis=-1, keepdims=True)
               + jnp.sum(p_m, axis=-1, keepdims=True) + jnp.exp(sink - m))
        acc = (jnp.dot(p_c.astype(BF16), vc[:, ksl], preferred_element_type=F32)
               + jnp.dot(p_p.astype(BF16), vp[:, ksl], preferred_element_type=F32)
               + jnp.dot(p_m.astype(BF16), vm[:, ksl], preferred_element_type=F32))
        o_ref[:, h * SWA_HD:(h + 1) * SWA_HD] = acc / den


def _t5_bucket_np(dist):
    max_exact = N_BUCKETS // 2
    df = np.maximum(dist, 1).astype(np.float64)
    large = max_exact + (np.log(df / max_exact) / math.log(REL_MAX_DIST / max_exact)
                         * (N_BUCKETS - max_exact)).astype(np.int64)
    large = np.minimum(large, N_BUCKETS - 1)
    return np.where(dist < max_exact, dist, large).astype(np.int32)


def _bias_tables(rel_bias):
    qi = np.arange(BLK)[:, None]
    kj = np.arange(BLK)[None, :]
    b_cur = _t5_bucket_np(np.maximum(qi - kj, 0))
    b_prev = _t5_bucket_np(np.maximum(qi - kj + BLK, 0))
    m = np.arange(N_META)[None, :]
    b_m0 = _t5_bucket_np(np.maximum(qi - m, 0))
    b_m1 = _t5_bucket_np(np.maximum(qi + BLK - m, 0))
    assert (b_m1 == N_BUCKETS - 1).all()
    rb = rel_bias.astype(F32)

    def look(bk):
        return jnp.moveaxis(jnp.take(rb, jnp.asarray(bk), axis=0), -1, 0)

    return look(b_cur), look(b_prev), jnp.stack([look(b_m0), look(b_m1)])


def _swa_prompt(p, sinks, q_norm, k_norm, bias_tabs):
    b_cur, b_prev, b_meta = bias_tabs
    nb = N_BLK
    ones_bd = jnp.asarray(np.kron(np.eye(SWA_HEADS), np.ones((SWA_HD, SWA_HD))), BF16)
    qn = jnp.tile(q_norm.astype(F32), SWA_HEADS).reshape(1, -1)
    kn = jnp.tile(k_norm.astype(F32), SWA_KV_HEADS).reshape(1, -1)
    ck, cv = C_SK // 128, C_SV // 128
    return pl.pallas_call(
        _swa_kernel,
        out_shape=(jax.ShapeDtypeStruct((N_ROWS, SWA_HEADS * SWA_HD), F32),
                   jax.ShapeDtypeStruct((NP_ROWS, SWA_KV_HEADS * SWA_HD), F32)),
        grid=(BATCH, nb),
        in_specs=[pl.BlockSpec(memory_space=pltpu.SMEM),
                  pl.BlockSpec((BLK, 512), lambda b, j: (b * nb + j, C_SQ // 512)),
                  pl.BlockSpec((BLK, 128), lambda b, j: (b * nb + j, ck)),
                  pl.BlockSpec((BLK, 128), lambda b, j: (b * nb + j, cv)),
                  pl.BlockSpec((BLK, 128), lambda b, j: (b * nb + jnp.maximum(j - 1, 0), ck)),
                  pl.BlockSpec((BLK, 128), lambda b, j: (b * nb + jnp.maximum(j - 1, 0), cv)),
                  pl.BlockSpec((N_META, 128), lambda b, j: (b * (LP // N_META), ck)),
                  pl.BlockSpec((N_META, 128), lambda b, j: (b * (LP // N_META), cv)),
                  pl.BlockSpec((512, 512), lambda b, j: (0, 0)),
                  pl.BlockSpec((1, 512), lambda b, j: (0, 0)),
                  pl.BlockSpec((1, 128), lambda b, j: (0, 0)),
                  pl.BlockSpec((SWA_HEADS, BLK, BLK), lambda b, j: (0, 0, 0)),
                  pl.BlockSpec((SWA_HEADS, BLK, BLK), lambda b, j: (0, 0, 0)),
                  pl.BlockSpec((1, SWA_HEADS, BLK, N_META), lambda b, j: (jnp.minimum(j, 1), 0, 0, 0))],
        out_specs=(pl.BlockSpec((BLK, 512), lambda b, j: (b * nb + j, 0)),
                   pl.BlockSpec((BLK, 128), lambda b, j: (b * nb + j, 0))),
        compiler_params=_cparams(("parallel", "parallel")),
        name="swa_prompt",
    )(sinks.astype(F32), p, p, p, p, p, p, p, ones_bd, qn, kn, b_cur, b_prev, b_meta)


def _out_ffn_kernel(h_ref, a_ref, b_ref, c_ref, wo_ref, g_ref, wq_ref, kbd_ref,
                    h1_ref, xn_ref, st_ref):
    h1 = (h_ref[...]
          + jnp.dot(a_ref[...].astype(BF16), wo_ref[0:512, :], preferred_element_type=F32)
          + jnp.dot(b_ref[...].astype(BF16), wo_ref[512:1024, :], preferred_element_type=F32)
          + jnp.dot(c_ref[...].astype(BF16), wo_ref[1024:1536, :], preferred_element_type=F32))
    h1_ref[...] = h1
    ms = jnp.mean(h1 * h1, axis=-1, keepdims=True)
    xn = (h1 * lax.rsqrt(ms + EPS) * g_ref[...]).astype(BF16)
    xn_ref[...] = xn
    q = jnp.dot(xn, wq_ref[...], preferred_element_type=F32).astype(BF16)
    kbd = kbd_ref[...]
    for hh in range(PEER_HEADS):
        st_ref[hh] = lax.dot_general(kbd, q[:, hh * 128:(hh + 1) * 128], (((1,), (1,)), ((), ())),
                                     preferred_element_type=F32)


def _out_ffn(h, conv_o, ret_o, swa_o, wo_bf, g, wq_bf, kbd_bf):
    return pl.pallas_call(
        _out_ffn_kernel,
        out_shape=(jax.ShapeDtypeStruct((N_ROWS, D_MODEL), F32),
                   jax.ShapeDtypeStruct((N_ROWS, D_MODEL), BF16),
                   jax.ShapeDtypeStruct((PEER_HEADS, 2 * N_KEYS, N_ROWS), F32)),
        grid=(N_ROWS // TM,),
        in_specs=[pl.BlockSpec((TM, D_MODEL), lambda i: (i, 0)),
                  pl.BlockSpec((TM, 512), lambda i: (i, 0)),
                  pl.BlockSpec((TM, 512), lambda i: (i, 0)),
                  pl.BlockSpec((TM, 512), lambda i: (i, 0)),
                  pl.BlockSpec((1536, D_MODEL), lambda i: (0, 0)),
                  pl.BlockSpec((1, D_MODEL), lambda i: (0, 0)),
                  pl.BlockSpec((D_MODEL, D_MODEL), lambda i: (0, 0)),
                  pl.BlockSpec((2 * N_KEYS, 128), lambda i: (0, 0))],
        out_specs=(pl.BlockSpec((TM, D_MODEL), lambda i: (i, 0)),
                   pl.BlockSpec((TM, D_MODEL), lambda i: (i, 0)),
                   pl.BlockSpec((PEER_HEADS, 2 * N_KEYS, TM), lambda i: (0, 0, i))),
        compiler_params=_cparams(("parallel",)),
        name="out_ffn",
    )(h, conv_o, ret_o, swa_o, wo_bf, g.reshape(1, -1), wq_bf, kbd_bf)


def _top_vals(x, n):
    vals = []
    for _ in range(n):
        mx = jnp.max(x, axis=0, keepdims=True)
        vals.append(mx)
        x = jnp.where(x == mx, NEG, x)
    return vals


def _topk_kernel(st_ref, th_ref, c1_ref, e2_ref):
    s1 = st_ref[0, 0:N_KEYS, :]
    s2 = st_ref[0, N_KEYS:2 * N_KEYS, :]
    v1 = _top_vals(s1, PEER_TOPK)
    v2 = _top_vals(s2, PEER_TOPK)
    sv2 = jnp.concatenate(v2, axis=0)
    cand = jnp.concatenate([v1[a] + sv2 for a in range(PEER_TOPK)], axis=0)
    top = _top_vals(cand, PEER_TOPK)
    tau = top[PEER_TOPK - 1]
    z = jnp.ones_like(tau)
    for r in range(1, PEER_TOPK):
        z = z + jnp.exp(top[r] - top[0])
    theta = jnp.full(s1.shape, -NEG, F32)
    for a in range(PEER_TOPK):
        th_a = jnp.min(jnp.where(v1[a] + sv2 >= tau, sv2, -NEG), axis=0, keepdims=True)
        theta = jnp.where(s1 == v1[a], th_a, theta)
    th_ref[0] = theta
    c1_ref[0] = jnp.exp(s1 - v1[0]) / z
    e2_ref[0] = jnp.exp(s2 - v2[0])


def _topk(st):
    shp = jax.ShapeDtypeStruct((PEER_HEADS, N_KEYS, N_ROWS), F32)
    spec = pl.BlockSpec((1, N_KEYS, TM), lambda i, h: (h, 0, i))
    return pl.pallas_call(
        _topk_kernel,
        out_shape=(shp, shp, shp),
        grid=(N_ROWS // TM, PEER_HEADS),
        in_specs=[pl.BlockSpec((1, 2 * N_KEYS, TM), lambda i, h: (h, 0, i))],
        out_specs=(spec, spec, spec),
        compiler_params=_cparams(("parallel", "parallel")),
        name="peer_topk",
    )(st)


def _peer_kernel(xn_ref, h1_ref, u_ref, vt_ref, th_ref, c1_ref, s2_ref, e2_ref, o_ref,
                 acc_ref, gt_ref, at_ref):
    ci = pl.program_id(1)

    @pl.when(ci == 0)
    def _():
        acc_ref[...] = jnp.zeros_like(acc_ref)

    at_ref[...] = lax.dot_general(u_ref[...], xn_ref[...], (((1,), (1,)), ((), ())),
                                  preferred_element_type=F32)
    for ii in range(EC // N_KEYS):
        rows = slice(ii * N_KEYS, (ii + 1) * N_KEYS)
        for tb in range(TT // PEER_LANES):
            cols = slice(tb * PEER_LANES, (tb + 1) * PEER_LANES)
            w = jnp.zeros((N_KEYS, PEER_LANES), F32)
            for hh in range(PEER_HEADS):
                th = th_ref[hh, ii:ii + 1, cols]
                cc = c1_ref[hh, ii:ii + 1, cols]
                w = w + jnp.where(s2_ref[hh, :, cols] >= th, e2_ref[hh, :, cols] * cc, 0.0)
            a = at_ref[rows, cols]
            act = 0.5 * a * (1.0 + lax.erf(a * (2.0 ** -0.5)))
            gt_ref[rows, cols] = (w * act).astype(BF16)
    acc_ref[...] += jnp.dot(vt_ref[...], gt_ref[...], preferred_element_type=F32)

    @pl.when(ci == pl.num_programs(1) - 1)
    def _():
        o_ref[...] = h1_ref[...] + acc_ref[...].T


def _peer(xn, h1, u_bf, vt_bf, theta, c1, st, e2):
    nk = EC // N_KEYS
    sel = pl.BlockSpec((PEER_HEADS, nk, TT), lambda t, c: (0, c, t))
    full = pl.BlockSpec((PEER_HEADS, N_KEYS, TT), lambda t, c: (0, 0, t))
    return pl.pallas_call(
        _peer_kernel,
        out_shape=jax.ShapeDtypeStruct((N_ROWS, D_MODEL), F32),
        grid=(N_ROWS // TT, N_EXPERTS // EC),
        in_specs=[pl.BlockSpec((TT, D_MODEL), lambda t, c: (t, 0)),
                  pl.BlockSpec((TT, D_MODEL), lambda t, c: (t, 0)),
                  pl.BlockSpec((EC, D_MODEL), lambda t, c: (c, 0)),
                  pl.BlockSpec((D_MODEL, EC), lambda t, c: (0, c)),
                  sel, sel,
                  pl.BlockSpec((PEER_HEADS, N_KEYS, TT), lambda t, c: (0, 1, t)),
                  full],
        out_specs=pl.BlockSpec((TT, D_MODEL), lambda t, c: (t, 0)),
        scratch_shapes=[pltpu.VMEM((D_MODEL, TT), F32), pltpu.VMEM((EC, TT), BF16),
                        pltpu.VMEM((EC, TT), F32)],
        compiler_params=_cparams(("parallel", "arbitrary")),
        name="peer_dense",
    )(xn, h1, u_bf, vt_bf, theta, c1, st, e2)


def _layernorm(xf, g, b):
    mu = jnp.mean(xf, axis=-1, keepdims=True)
    var = jnp.mean(jnp.square(xf - mu), axis=-1, keepdims=True)
    return (xf - mu) * lax.rsqrt(var + EPS) * g + b


def _rms(x, g):
    return x * lax.rsqrt(jnp.mean(x * x, axis=-1, keepdims=True) + EPS) * g


def _rotary(x, pos):
    half = x.shape[-1] // 2
    inv = 1.0 / (10000.0 ** (jnp.arange(half, dtype=F32) / half))
    ang = pos.astype(F32)[:, None] * inv[None]
    cos, sin = jnp.cos(ang)[:, None, :], jnp.sin(ang)[:, None, :]
    x1, x2 = x[..., :half], x[..., half:]
    return jnp.concatenate([x1 * cos - x2 * sin, x1 * sin + x2 * cos], axis=-1)


def _t5_bucket(dist):
    max_exact = N_BUCKETS // 2
    df = jnp.maximum(dist, 1).astype(F32)
    large = max_exact + (jnp.log(df / max_exact) / math.log(REL_MAX_DIST / max_exact)
                         * (N_BUCKETS - max_exact)).astype(jnp.int32)
    large = jnp.minimum(large, N_BUCKETS - 1)
    return jnp.where(dist < max_exact, dist, large)


def _sample_mixers(ps, l, cache_meta_kv, cache_swa_kv, state_ret, state_conv, lw, rel_bias):
    db, s = DEC_BATCH, DEC_SEQ
    pos = PAST_LEN + jnp.arange(s)
    c_in = ps[..., C_CONV:C_RQ]
    rq, rk = ps[..., C_RQ:C_RK], ps[..., C_RK:C_RV]
    rv, rg = ps[..., C_RV:C_RG], ps[..., C_RG:C_SQ]
    sq, sk, sv = ps[..., C_SQ:C_SK], ps[..., C_SK:C_SV], ps[..., C_SV:]
    glu = c_in[..., :CONV_CH] * jax.nn.sigmoid(c_in[..., CONV_CH:])
    xin = jnp.concatenate([state_conv[l], glu], axis=1)
    y = sum(xin[:, k:k + s] * lw["conv_w"][k][None, None] for k in range(CONV_WIDTH))
    yf = _layernorm(y + lw["conv_b"], lw["conv_ln_g"], lw["conv_ln_b"])
    conv_o = jax.nn.silu(yf)
    new_conv = xin[:, -(CONV_WIDTH - 1):]
    lg = jnp.log(1.0 - 2.0 ** (-5.0 - jnp.arange(RET_HEADS, dtype=F32)))
    q = _rotary(rq.reshape(db, s, RET_HEADS, RET_DK), pos).transpose(0, 2, 1, 3)
    k = (_rotary(rk.reshape(db, s, RET_HEADS, RET_DK), pos) * (RET_DK ** -0.5)).transpose(0, 2, 1, 3)
    v = rv.reshape(db, s, RET_HEADS, RET_DV).transpose(0, 2, 1, 3)
    st = state_ret[l]
    i = jnp.arange(s, dtype=F32)
    diff = i[:, None] - i[None, :]
    decay = jnp.where(diff >= 0, jnp.exp(jnp.maximum(diff, 0.0)[None] * lg[:, None, None]), 0.0)
    inner = jnp.einsum("bhid,bhjd->bhij", q, k) * decay
    o = (jnp.einsum("bhij,bhje->bhie", inner, v)
         + jnp.exp((i[None] + 1.0) * lg[:, None])[None, :, :, None] * jnp.einsum("bhid,bhde->bhie", q, st))
    k_dec = k * jnp.exp((s - 1.0 - i)[None] * lg[:, None])[None, :, :, None]
    new_ret = jnp.exp(s * lg)[None, :, None, None] * st + jnp.einsum("bhjd,bhje->bhde", k_dec, v)
    o = o.transpose(0, 2, 1, 3)
    mu = jnp.mean(o, axis=-1, keepdims=True)
    var = jnp.mean(jnp.square(o - mu), axis=-1, keepdims=True)
    yr = ((o - mu) * lax.rsqrt(var + EPS)).reshape(db, s, RET_HEADS * RET_DV)
    ret_o = (yr * lw["ret_gn_g"] + lw["ret_gn_b"]) * jax.nn.silu(rg)
    qs = _rms(sq.reshape(db, s, SWA_KV_HEADS, SWA_GROUP, SWA_HD), lw["swa_q_norm"])
    ks = _rms(sk.reshape(db, s, SWA_KV_HEADS, SWA_HD), lw["swa_k_norm"])
    vs = sv.reshape(db, s, SWA_KV_HEADS, SWA_HD)
    new_kv = jnp.stack([ks, vs], axis=2)
    kv = jnp.concatenate([cache_meta_kv[l], cache_swa_kv[l], new_kv], axis=1)
    kpos = jnp.concatenate([jnp.arange(N_META), PAST_LEN - WINDOW + jnp.arange(WINDOW), pos])
    is_meta = jnp.arange(kpos.shape[0]) < N_META
    sc = jnp.einsum("bqkgd,bskd->bkgqs", qs, kv[:, :, 0]) * (SWA_HD ** -0.5)
    dist = pos[:, None] - kpos[None, :]
    valid = jnp.where(is_meta, kpos[None, :] <= pos[:, None],
                      (dist >= 0) & (dist < WINDOW) & (kpos[None, :] >= N_META))
    bias = jnp.take(rel_bias.astype(F32), _t5_bucket(jnp.maximum(dist, 0)), axis=0)
    bias = jnp.moveaxis(bias, -1, 0).reshape(SWA_KV_HEADS, SWA_GROUP, s, -1)
    sc = jnp.where(valid[None, None, None], sc + bias[None], NEG)
    sink = jnp.broadcast_to(lw["swa_sinks"].reshape(1, SWA_KV_HEADS, SWA_GROUP, 1, 1), sc.shape[:-1] + (1,))
    pr = jax.nn.softmax(jnp.concatenate([sc, sink], axis=-1), axis=-1)[..., :-1]
    swa_o = jnp.einsum("bkgqs,bskd->bqkgd", pr, kv[:, :, 1]).reshape(db, s, SWA_HEADS * SWA_HD)
    new_win = jnp.concatenate([cache_swa_kv[l], new_kv], axis=1)[:, -WINDOW:]
    return conv_o, ret_o, swa_o, new_win, new_ret, new_conv


def kernel(x_prompt, x_sample, cache_meta_kv, cache_swa_kv, state_ret, state_conv, meta_tokens, rel_bias,
           norm_mix, w_in, conv_w, conv_b, conv_ln_g, conv_ln_b, ret_gn_g, ret_gn_b, swa_q_norm,
           swa_k_norm, swa_sinks, w_out, norm_ffn, peer_wq, peer_keys, peer_u, peer_v):
    meta = jnp.broadcast_to(meta_tokens.astype(F32)[None], (BATCH, N_META, D_MODEL))
    pad = jnp.zeros((BATCH, LP - L_REAL, D_MODEL), F32)
    hp = jnp.concatenate([meta, x_prompt, pad], axis=1).reshape(NP_ROWS, D_MODEL)
    h = jnp.concatenate([hp, x_sample.reshape(NS_ROWS, D_MODEL)], axis=0)

    cos_t, sin_t = _rotary_tables(jnp.arange(LP))
    ret_tabs = _ret_tables()
    bias_tabs = _bias_tables(rel_bias)

    meta_p, win_p, ret_p, conv_p, win_s, ret_s, conv_s = [], [], [], [], [], [], []
    for l in range(DEPTH):
        lw = {"conv_w": conv_w[l], "conv_b": conv_b[l], "conv_ln_g": conv_ln_g[l], "conv_ln_b": conv_ln_b[l],
              "ret_gn_g": ret_gn_g[l], "ret_gn_b": ret_gn_b[l], "swa_q_norm": swa_q_norm[l],
              "swa_k_norm": swa_k_norm[l], "swa_sinks": swa_sinks[l]}
        p = _norm_proj(h, norm_mix[l], w_in[l].astype(BF16))

        w32 = jnp.concatenate([conv_w[l], jnp.zeros((1, CONV_CH), F32)], axis=0)
        conv_o, conv_tail = _conv_prompt(p, w32, conv_b[l], conv_ln_g[l], conv_ln_b[l])
        ret_o, ret_state = _ret_prompt(p, cos_t, sin_t, ret_tabs, ret_gn_g[l], ret_gn_b[l])
        swa_o, k_normed = _swa_prompt(p, swa_sinks[l], swa_q_norm[l], swa_k_norm[l], bias_tabs)

        ps = p[NP_ROWS:].reshape(DEC_BATCH, DEC_SEQ, PROJ_COLS)
        s_conv, s_ret, s_swa, new_win, new_ret, new_conv = _sample_mixers(
            ps, l, cache_meta_kv, cache_swa_kv, state_ret, state_conv, lw, rel_bias)
        conv_o = lax.dynamic_update_slice(conv_o, s_conv.reshape(NS_ROWS, -1), (NP_ROWS, 0))
        ret_o = lax.dynamic_update_slice(ret_o, s_ret.reshape(NS_ROWS, -1), (NP_ROWS, 0))
        swa_o = lax.dynamic_update_slice(swa_o, s_swa.reshape(NS_ROWS, -1), (NP_ROWS, 0))

        kbd = jnp.zeros((2 * N_KEYS, 2 * PEER_HALF), F32)
        kbd = kbd.at[:N_KEYS, :PEER_HALF].set(peer_keys[l, 0]).at[N_KEYS:, PEER_HALF:].set(peer_keys[l, 1])
        h1, xn, st = _out_ffn(h, conv_o, ret_o, swa_o, w_out[l].astype(BF16), norm_ffn[l],
                              peer_wq[l].astype(BF16), kbd.astype(BF16))
        theta, c1, e2 = _topk(st)
        h = _peer(xn, h1, peer_u[l].astype(BF16), peer_v[l].T.astype(BF16), theta, c1, st, e2)

        kp = k_normed.reshape(BATCH, LP, SWA_KV_HEADS, SWA_HD)
        vp = p[:NP_ROWS, C_SV:].reshape(BATCH, LP, SWA_KV_HEADS, SWA_HD)
        kvp = jnp.stack([kp, vp], axis=2)
        meta_p.append(kvp[:, :N_META])
        win_p.append(kvp[:, L_REAL - WINDOW:L_REAL])
        ret_p.append(ret_state)
        conv_p.append(conv_tail[:, 32 - (CONV_WIDTH - 1):])
        win_s.append(new_win)
        ret_s.append(new_ret)
        conv_s.append(new_conv)

    y_prompt = h[:NP_ROWS].reshape(BATCH, LP, D_MODEL)[:, N_META:L_REAL]
    y_sample = h[NP_ROWS:].reshape(DEC_BATCH, DEC_SEQ, D_MODEL)
    return (y_prompt, y_sample, jnp.stack(meta_p), jnp.stack(win_p), jnp.stack(ret_p), jnp.stack(conv_p),
            jnp.stack(win_s), jnp.stack(ret_s), jnp.stack(conv_s))
```

```python
import functools
import math

import numpy as np
import jax
import jax.numpy as jnp
from jax import lax
from jax.experimental import pallas as pl
from jax.experimental.pallas import tpu as pltpu

F32 = jnp.float32
BF16 = jnp.bfloat16

D_MODEL = 1024
BATCH = 2
SEQ = 8192
DEPTH = 2
DEC_BATCH = 128
DEC_SEQ = 8
PAST_LEN = 8192
N_META = 16
CONV_CH = 512
CONV_WIDTH = 31
RET_HEADS = 4
RET_DK = 64
RET_DV = 128
SWA_HEADS = 8
SWA_KV_HEADS = 2
SWA_GROUP = SWA_HEADS // SWA_KV_HEADS
SWA_HD = 64
WINDOW = 128
N_BUCKETS = 32
REL_MAX_DIST = 128
PEER_HEADS = 8
N_KEYS = 128
N_EXPERTS = N_KEYS * N_KEYS
PEER_TOPK = 16
PEER_HALF = 64
EPS = 1e-6
NEG = -1e30

PROJ_COLS = 3328
C_CONV, C_RQ, C_RK, C_RV, C_RG, C_SQ, C_SK, C_SV = 0, 1024, 1280, 1536, 2048, 2560, 3072, 3200

L_REAL = N_META + SEQ
BLK = 128
N_BLK = 65
LP = N_BLK * BLK
NP_ROWS = BATCH * LP
NS_ROWS = DEC_BATCH * DEC_SEQ
N_ROWS = NP_ROWS + NS_ROWS
LAST_REAL = L_REAL - (N_BLK - 1) * BLK

TM = 384
TT = 768
EC = 1024
PEER_LANES = 256
CONV_T = 640
CONV_RB = 64
VMEM_LIMIT = 56 * 1024 * 1024


def _cparams(sem):
    return pltpu.CompilerParams(dimension_semantics=sem, vmem_limit_bytes=VMEM_LIMIT)


def _norm_proj_kernel(x_ref, g_ref, w_ref, o_ref):
    x = x_ref[...]
    ms = jnp.mean(x * x, axis=-1, keepdims=True)
    xn = x * lax.rsqrt(ms + EPS) * g_ref[...]
    o_ref[...] = jnp.dot(xn.astype(BF16), w_ref[...], preferred_element_type=F32)


def _norm_proj(h, g, w_bf):
    return pl.pallas_call(
        _norm_proj_kernel,
        out_shape=jax.ShapeDtypeStruct((N_ROWS, PROJ_COLS), F32),
        grid=(N_ROWS // TM,),
        in_specs=[pl.BlockSpec((TM, D_MODEL), lambda i: (i, 0)),
                  pl.BlockSpec((1, D_MODEL), lambda i: (0, 0)),
                  pl.BlockSpec((D_MODEL, PROJ_COLS), lambda i: (0, 0))],
        out_specs=pl.BlockSpec((TM, PROJ_COLS), lambda i: (i, 0)),
        compiler_params=_cparams(("parallel",)),
        name="norm_proj",
    )(h, g.reshape(1, D_MODEL), w_bf)


def _conv_kernel(c_ref, w_ref, b_ref, g_ref, beta_ref, o_ref, st_ref, xin_ref):
    t = pl.program_id(1)

    @pl.when(t == 0)
    def _():
        xin_ref[0:32, :] = jnp.zeros((32, CONV_CH), F32)

    @pl.when(t > 0)
    def _():
        xin_ref[0:32, :] = xin_ref[CONV_T:CONV_T + 32, :]

    c = c_ref[...]
    xin_ref[32:32 + CONV_T, :] = c[:, :CONV_CH] * jax.nn.sigmoid(c[:, CONV_CH:])

    w = w_ref[...]
    bias = b_ref[...]
    gam = g_ref[...]
    beta = beta_ref[...]
    for rb in range(CONV_T // CONV_RB):
        r0 = rb * CONV_RB
        acc = jnp.zeros((CONV_RB, CONV_CH), F32)
        for k in range(CONV_WIDTH):
            acc = acc + xin_ref[r0 + 2 + k:r0 + 2 + k + CONV_RB, :] * w[k:k + 1, :]
        y = acc + bias
        mu = jnp.mean(y, axis=-1, keepdims=True)
        d = y - mu
        var = jnp.mean(d * d, axis=-1, keepdims=True)
        yn = d * lax.rsqrt(var + EPS) * gam + beta
        o_ref[r0:r0 + CONV_RB, :] = yn * jax.nn.sigmoid(yn)

    @pl.when(t == pl.num_programs(1) - 1)
    def _():
        lo = 32 + (L_REAL - 32) - (LP - CONV_T)
        st_ref[0] = xin_ref[lo:lo + 32, :]


def _conv_prompt(p, w32, b, g, beta):
    nt = LP // CONV_T
    return pl.pallas_call(
        _conv_kernel,
        out_shape=(jax.ShapeDtypeStruct((N_ROWS, CONV_CH), F32),
                   jax.ShapeDtypeStruct((BATCH, 32, CONV_CH), F32)),
        grid=(BATCH, nt),
        in_specs=[pl.BlockSpec((CONV_T, 2 * CONV_CH), lambda bi, t: (bi * nt + t, 0)),
                  pl.BlockSpec((32, CONV_CH), lambda bi, t: (0, 0)),
                  pl.BlockSpec((1, CONV_CH), lambda bi, t: (0, 0)),
                  pl.BlockSpec((1, CONV_CH), lambda bi, t: (0, 0)),
                  pl.BlockSpec((1, CONV_CH), lambda bi, t: (0, 0))],
        out_specs=(pl.BlockSpec((CONV_T, CONV_CH), lambda bi, t: (bi * nt + t, 0)),
                   pl.BlockSpec((1, 32, CONV_CH), lambda bi, t: (bi, 0, 0))),
        scratch_shapes=[pltpu.VMEM((32 + CONV_T + 32, CONV_CH), F32)],
        compiler_params=_cparams(("arbitrary", "arbitrary")),
        name="conv_prompt",
    )(p, w32, b.reshape(1, -1), g.reshape(1, -1), beta.reshape(1, -1))


def _swap_halves(x, first_half):
    return jnp.where(first_half, pltpu.roll(x, x.shape[1] - 32, axis=1), pltpu.roll(x, 32, axis=1))


def _ret_kernel(q_ref, k_ref, v_ref, rg_ref, cos_ref, sin_ref, dmat_ref, dq_ref, dk_ref, sg_ref,
                gng_ref, gnb_ref, o_ref, st_ref, s_ref):
    j = pl.program_id(1)

    @pl.when(j == 0)
    def _():
        s_ref[...] = jnp.zeros_like(s_ref)

    cos = cos_ref[...]
    sin = sin_ref[...]
    lane = lax.broadcasted_iota(jnp.int32, (BLK, RET_HEADS * RET_DK), 1)
    first_half = (lane % RET_DK) < (RET_DK // 2)
    q = q_ref[...]
    k = k_ref[...]
    q = q * cos + _swap_halves(q, first_half) * sin
    k = (k * cos + _swap_halves(k, first_half) * sin) * (RET_DK ** -0.5)
    v = v_ref[...]
    rg = rg_ref[...]
    for h in range(RET_HEADS):
        qh = q[:, h * RET_DK:(h + 1) * RET_DK].astype(BF16)
        kh = k[:, h * RET_DK:(h + 1) * RET_DK]
        vh = v[:, h * RET_DV:(h + 1) * RET_DV].astype(BF16)
        s_old = s_ref[h]
        inner = lax.dot_general(qh, kh.astype(BF16), (((1,), (1,)), ((), ())),
                                preferred_element_type=F32) * dmat_ref[h]
        o = (jnp.dot(inner.astype(BF16), vh, preferred_element_type=F32)
             + dq_ref[h] * jnp.dot(qh, s_old.astype(BF16), preferred_element_type=F32))
        kdec_t = (kh * dk_ref[0, h]).T.astype(BF16)
        s_new = sg_ref[0, h, 0:1, :] * s_old + jnp.dot(kdec_t, vh, preferred_element_type=F32)
        s_ref[h] = s_new
        mu = jnp.mean(o, axis=-1, keepdims=True)
        d = o - mu
        var = jnp.mean(d * d, axis=-1, keepdims=True)
        y = d * lax.rsqrt(var + EPS) * gng_ref[:, h * RET_DV:(h + 1) * RET_DV] \
            + gnb_ref[:, h * RET_DV:(h + 1) * RET_DV]
        gate = rg[:, h * RET_DV:(h + 1) * RET_DV]
        o_ref[:, h * RET_DV:(h + 1) * RET_DV] = y * (gate * jax.nn.sigmoid(gate))

    @pl.when(j == pl.num_programs(1) - 1)
    def _():
        st_ref[0] = s_ref[...]


def _ret_tables():
    lg = jnp.log(1.0 - 2.0 ** (-5.0 - jnp.arange(RET_HEADS, dtype=F32)))
    i = jnp.arange(BLK, dtype=F32)
    diff = i[:, None] - i[None, :]
    dmat = jnp.where(diff >= 0, jnp.exp(jnp.maximum(diff, 0.0)[None] * lg[:, None, None]), 0.0)
    dq = jnp.broadcast_to(jnp.exp((i[None] + 1.0) * lg[:, None])[:, :, None], (RET_HEADS, BLK, RET_DV))

    def kdec(c_eff):
        e = jnp.where(i[None] < c_eff, jnp.exp((c_eff - 1.0 - i)[None] * lg[:, None]), 0.0)
        return jnp.broadcast_to(e[:, :, None], (RET_HEADS, BLK, RET_DK))

    def sgam(c_eff):
        return jnp.broadcast_to(jnp.exp(c_eff * lg)[:, None, None], (RET_HEADS, 8, RET_DV))

    dk = jnp.stack([kdec(float(BLK)), kdec(float(LAST_REAL))])
    sg = jnp.stack([sgam(float(BLK)), sgam(float(LAST_REAL))])
    return dmat, dq, dk, sg


def _rotary_tables(pos):
    half = RET_DK // 2
    inv = 1.0 / (10000.0 ** (jnp.arange(half, dtype=F32) / half))
    ang = pos.astype(F32)[:, None] * inv[None]
    cos, sin = jnp.cos(ang), jnp.sin(ang)
    cos_t = jnp.tile(jnp.concatenate([cos, cos], axis=-1), (1, RET_HEADS))
    sin_t = jnp.tile(jnp.concatenate([-sin, sin], axis=-1), (1, RET_HEADS))
    return cos_t, sin_t


def _ret_prompt(p, cos_t, sin_t, tabs, gn_g, gn_b):
    dmat, dq, dk, sg = tabs
    nb = N_BLK
    last = nb - 1
    return pl.pallas_call(
        _ret_kernel,
        out_shape=(jax.ShapeDtypeStruct((N_ROWS, RET_HEADS * RET_DV), F32),
                   jax.ShapeDtypeStruct((BATCH, RET_HEADS, RET_DK, RET_DV), F32)),
        grid=(BATCH, nb),
        in_specs=[pl.BlockSpec((BLK, 256), lambda b, j: (b * nb + j, C_RQ // 256)),
                  pl.BlockSpec((BLK, 256), lambda b, j: (b * nb + j, C_RK // 256)),
                  pl.BlockSpec((BLK, 512), lambda b, j: (b * nb + j, C_RV // 512)),
                  pl.BlockSpec((BLK, 512), lambda b, j: (b * nb + j, C_RG // 512)),
                  pl.BlockSpec((BLK, 256), lambda b, j: (j, 0)),
                  pl.BlockSpec((BLK, 256), lambda b, j: (j, 0)),
                  pl.BlockSpec((RET_HEADS, BLK, BLK), lambda b, j: (0, 0, 0)),
                  pl.BlockSpec((RET_HEADS, BLK, RET_DV), lambda b, j: (0, 0, 0)),
                  pl.BlockSpec((1, RET_HEADS, BLK, RET_DK), lambda b, j: (j // last, 0, 0, 0)),
                  pl.BlockSpec((1, RET_HEADS, 8, RET_DV), lambda b, j: (j // last, 0, 0, 0)),
                  pl.BlockSpec((1, 512), lambda b, j: (0, 0)),
                  pl.BlockSpec((1, 512), lambda b, j: (0, 0))],
        out_specs=(pl.BlockSpec((BLK, 512), lambda b, j: (b * nb + j, 0)),
                   pl.BlockSpec((1, RET_HEADS, RET_DK, RET_DV), lambda b, j: (b, 0, 0, 0))),
        scratch_shapes=[pltpu.VMEM((RET_HEADS, RET_DK, RET_DV), F32)],
        compiler_params=_cparams(("arbitrary", "arbitrary")),
        name="ret_prompt",
    )(p, p, p, p, cos_t, sin_t, dmat, dq, dk, sg, gn_g.reshape(1, -1), gn_b.reshape(1, -1))


def _group_rms(x, ones_bd, w):
    x2 = x * x
    hi = x2.astype(BF16)
    lo = (x2 - hi.astype(F32)).astype(BF16)
    ss = (jnp.dot(hi, ones_bd, preferred_element_type=F32)
          + jnp.dot(lo, ones_bd, preferred_element_type=F32))
    return x * lax.rsqrt(ss * (1.0 / SWA_HD) + EPS) * w


def _swa_kernel(sinks_ref, q_ref, kc_ref, vc_ref, kp_ref, vp_ref, km_ref, vm_ref, ones_ref,
                qn_ref, kn_ref, bc_ref, bp_ref, bm_ref, o_ref, kout_ref):
    j = pl.program_id(1)
    ones_q = ones_ref[...]
    ones_k = ones_ref[0:128, 0:128]
    qw = qn_ref[...]
    kw = kn_ref[...]
    q = _group_rms(q_ref[...], ones_q, qw) * (SWA_HD ** -0.5)
    kc = _group_rms(kc_ref[...], ones_k, kw)
    kp = _group_rms(kp_ref[...], ones_k, kw)
    km = _group_rms(km_ref[...], ones_k, kw)
    kout_ref[...] = kc
    vc = vc_ref[...].astype(BF16)
    vp = vp_ref[...].astype(BF16)
    vm = vm_ref[...].astype(BF16)

    qi = lax.broadcasted_iota(jnp.int32, (BLK, BLK), 0)
    kj = lax.broadcasted_iota(jnp.int32, (BLK, BLK), 1)
    valid_c = jnp.where(kj <= qi, j * BLK + kj, -1) >= N_META
    valid_p = jnp.where(kj > qi, (j - 1) * BLK + kj, -1) >= N_META
    qm = lax.broadcasted_iota(jnp.int32, (BLK, N_META), 0)
    mm = lax.broadcasted_iota(jnp.int32, (BLK, N_META), 1)
    valid_m = mm <= j * BLK + qm

    dn = (((1,), (1,)), ((), ()))
    for h in range(SWA_HEADS):
        kv = h // SWA_GROUP
        qh = q[:, h * SWA_HD:(h + 1) * SWA_HD].astype(BF16)
        ksl = slice(kv * SWA_HD, (kv + 1) * SWA_HD)
        s_c = lax.dot_general(qh, kc[:, ksl].astype(BF16), dn, preferred_element_type=F32)
        s_p = lax.dot_general(qh, kp[:, ksl].astype(BF16), dn, preferred_element_type=F32)
        s_m = lax.dot_general(qh, km[:, ksl].astype(BF16), dn, preferred_element_type=F32)
        s_c = jnp.where(valid_c, s_c + bc_ref[h], NEG)
        s_p = jnp.where(valid_p, s_p + bp_ref[h], NEG)
        s_m = jnp.where(valid_m, s_m + bm_ref[0, h], NEG)
        sink = sinks_ref[h]
        m = jnp.maximum(jnp.maximum(jnp.max(s_c, axis=-1, keepdims=True),
                                    jnp.max(s_p, axis=-1, keepdims=True)),
                        jnp.maximum(jnp.max(s_m, axis=-1, keepdims=True), sink))
        p_c = jnp.exp(s_c - m)
        p_p = jnp.exp(s_p - m)
        p_m = jnp.exp(s_m - m)
        den = (jnp.sum(p_c, axis=-1, keepdims=True) + jnp.sum(p_p, axis=-1, keepdims=True)
               + jnp.sum(p_m, axis=-1, keepdims=True) + jnp.exp(sink - m))
        acc = (jnp.dot(p_c.astype(BF16), vc[:, ksl], preferred_element_type=F32)
               + jnp.dot(p_p.astype(BF16), vp[:, ksl], preferred_element_type=F32)
               + jnp.dot(p_m.astype(BF16), vm[:, ksl], preferred_element_type=F32))
        o_ref[:, h * SWA_HD:(h + 1) * SWA_HD] = acc / den


def _t5_bucket_np(dist):
    max_exact = N_BUCKETS // 2
    df = np.maximum(dist, 1).astype(np.float64)
    large = max_exact + (np.log(df / max_exact) / math.log(REL_MAX_DIST / max_exact)
                         * (N_BUCKETS - max_exact)).astype(np.int64)
    large = np.minimum(large, N_BUCKETS - 1)
    return np.where(dist < max_exact, dist, large).astype(np.int32)


def _bias_tables(rel_bias):
    qi = np.arange(BLK)[:, None]
    kj = np.arange(BLK)[None, :]
    b_cur = _t5_bucket_np(np.maximum(qi - kj, 0))
    b_prev = _t5_bucket_np(np.maximum(qi - kj + BLK, 0))
    m = np.arange(N_META)[None, :]
    b_m0 = _t5_bucket_np(np.maximum(qi - m, 0))
    b_m1 = _t5_bucket_np(np.maximum(qi + BLK - m, 0))
    assert (b_m1 == N_BUCKETS - 1).all()
    rb = rel_bias.astype(F32)

    def look(bk):
        return jnp.moveaxis(jnp.take(rb, jnp.asarray(bk), axis=0), -1, 0)

    return look(b_cur), look(b_prev), jnp.stack([look(b_m0), look(b_m1)])


def _swa_prompt(p, sinks, q_norm, k_norm, bias_tabs):
    b_cur, b_prev, b_meta = bias_tabs
    nb = N_BLK
    ones_bd = jnp.asarray(np.kron(np.eye(SWA_HEADS), np.ones((SWA_HD, SWA_HD))), BF16)
    qn = jnp.tile(q_norm.astype(F32), SWA_HEADS).reshape(1, -1)
    kn = jnp.tile(k_norm.astype(F32), SWA_KV_HEADS).reshape(1, -1)
    ck, cv = C_SK // 128, C_SV // 128
    return pl.pallas_call(
        _swa_kernel,
        out_shape=(jax.ShapeDtypeStruct((N_ROWS, SWA_HEADS * SWA_HD), F32),
                   jax.ShapeDtypeStruct((NP_ROWS, SWA_KV_HEADS * SWA_HD), F32)),
        grid=(BATCH, nb),
        in_specs=[pl.BlockSpec(memory_space=pltpu.SMEM),
                  pl.BlockSpec((BLK, 512), lambda b, j: (b * nb + j, C_SQ // 512)),
                  pl.BlockSpec((BLK, 128), lambda b, j: (b * nb + j, ck)),
                  pl.BlockSpec((BLK, 128), lambda b, j: (b * nb + j, cv)),
                  pl.BlockSpec((BLK, 128), lambda b, j: (b * nb + jnp.maximum(j - 1, 0), ck)),
                  pl.BlockSpec((BLK, 128), lambda b, j: (b * nb + jnp.maximum(j - 1, 0), cv)),
                  pl.BlockSpec((N_META, 128), lambda b, j: (b * (LP // N_META), ck)),
                  pl.BlockSpec((N_META, 128), lambda b, j: (b * (LP // N_META), cv)),
                  pl.BlockSpec((512, 512), lambda b, j: (0, 0)),
                  pl.BlockSpec((1, 512), lambda b, j: (0, 0)),
                  pl.BlockSpec((1, 128), lambda b, j: (0, 0)),
                  pl.BlockSpec((SWA_HEADS, BLK, BLK), lambda b, j: (0, 0, 0)),
                  pl.BlockSpec((SWA_HEADS, BLK, BLK), lambda b, j: (0, 0, 0)),
                  pl.BlockSpec((1, SWA_HEADS, BLK, N_META), lambda b, j: (jnp.minimum(j, 1), 0, 0, 0))],
        out_specs=(pl.BlockSpec((BLK, 512), lambda b, j: (b * nb + j, 0)),
                   pl.BlockSpec((BLK, 128), lambda b, j: (b * nb + j, 0))),
        compiler_params=_cparams(("parallel", "parallel")),
        name="swa_prompt",
    )(sinks.astype(F32), p, p, p, p, p, p, p, ones_bd, qn, kn, b_cur, b_prev, b_meta)


def _out_ffn_kernel(h_ref, a_ref, b_ref, c_ref, wo_ref, g_ref, wq_ref, kbd_ref,
                    h1_ref, xn_ref, st_ref):
    h1 = (h_ref[...]
          + jnp.dot(a_ref[...].astype(BF16), wo_ref[0:512, :], preferred_element_type=F32)
          + jnp.dot(b_ref[...].astype(BF16), wo_ref[512:1024, :], preferred_element_type=F32)
          + jnp.dot(c_ref[...].astype(BF16), wo_ref[1024:1536, :], preferred_element_type=F32))
    h1_ref[...] = h1
    ms = jnp.mean(h1 * h1, axis=-1, keepdims=True)
    xn = (h1 * lax.rsqrt(ms + EPS) * g_ref[...]).astype(BF16)
    xn_ref[...] = xn
    q = jnp.dot(xn, wq_ref[...], preferred_element_type=F32).astype(BF16)
    kbd = kbd_ref[...]
    for hh in range(PEER_HEADS):
        st_ref[hh] = lax.dot_general(kbd, q[:, hh * 128:(hh + 1) * 128], (((1,), (1,)), ((), ())),
                                     preferred_element_type=F32)


def _out_ffn(h, conv_o, ret_o, swa_o, wo_bf, g, wq_bf, kbd_bf):
    return pl.pallas_call(
        _out_ffn_kernel,
        out_shape=(jax.ShapeDtypeStruct((N_ROWS, D_MODEL), F32),
                   jax.ShapeDtypeStruct((N_ROWS, D_MODEL), BF16),
                   jax.ShapeDtypeStruct((PEER_HEADS, 2 * N_KEYS, N_ROWS), F32)),
        grid=(N_ROWS // TM,),
        in_specs=[pl.BlockSpec((TM, D_MODEL), lambda i: (i, 0)),
                  pl.BlockSpec((TM, 512), lambda i: (i, 0)),
                  pl.BlockSpec((TM, 512), lambda i: (i, 0)),
                  pl.BlockSpec((TM, 512), lambda i: (i, 0)),
                  pl.BlockSpec((1536, D_MODEL), lambda i: (0, 0)),
                  pl.BlockSpec((1, D_MODEL), lambda i: (0, 0)),
                  pl.BlockSpec((D_MODEL, D_MODEL), lambda i: (0, 0)),
                  pl.BlockSpec((2 * N_KEYS, 128), lambda i: (0, 0))],
        out_specs=(pl.BlockSpec((TM, D_MODEL), lambda i: (i, 0)),
                   pl.BlockSpec((TM, D_MODEL), lambda i: (i, 0)),
                   pl.BlockSpec((PEER_HEADS, 2 * N_KEYS, TM), lambda i: (0, 0, i))),
        compiler_params=_cparams(("parallel",)),
        name="out_ffn",
    )(h, conv_o, ret_o, swa_o, wo_bf, g.reshape(1, -1), wq_bf, kbd_bf)


def _top_vals(x, n, with_rank=False):
    vals = []
    rank = jnp.full(x.shape, float(n), F32)
    for r in range(n):
        mx = jnp.max(x, axis=0, keepdims=True)
        vals.append(mx)
        hit = x == mx
        if with_rank:
            rank = jnp.where(hit, float(r), rank)
        x = jnp.where(hit, NEG, x)
    return (vals, rank) if with_rank else vals


def _topk_kernel(st_ref, rho_ref, c1_ref, r2_ref, e2_ref):
    s1 = st_ref[0, 0:N_KEYS, :]
    s2 = st_ref[0, N_KEYS:2 * N_KEYS, :]
    v1 = _top_vals(s1, PEER_TOPK)
    v2, rank2 = _top_vals(s2, PEER_TOPK, with_rank=True)
    sv1 = jnp.concatenate(v1, axis=0)
    sv2 = jnp.concatenate(v2, axis=0)
    cand = jnp.concatenate([v1[0] + sv2, v1[1] + sv2]
                           + [v1[a] + sv2[0:8] for a in range(2, 8)]
                           + [sv1[8:16] + v2[0]], axis=0)
    top = _top_vals(cand, PEER_TOPK)
    tau = top[PEER_TOPK - 1]
    z = jnp.ones_like(tau)
    for r in range(1, PEER_TOPK):
        z = z + jnp.exp(top[r] - top[0])
    rho = jnp.zeros(s1.shape, F32)
    for a in range(PEER_TOPK):
        n_a = jnp.sum(jnp.where(v1[a] + sv2 >= tau, 1.0, 0.0), axis=0, keepdims=True)
        rho = jnp.where(s1 == v1[a], n_a, rho)
    rho_ref[0] = rho
    c1_ref[0] = jnp.exp(s1 - v1[0]) / z
    r2_ref[0] = rank2.astype(BF16)
    e2_ref[0] = jnp.exp(s2 - v2[0]).astype(BF16)


def _topk(st):
    shp = jax.ShapeDtypeStruct((PEER_HEADS, N_KEYS, N_ROWS), F32)
    shp_bf = jax.ShapeDtypeStruct((PEER_HEADS, N_KEYS, N_ROWS), BF16)
    spec = pl.BlockSpec((1, N_KEYS, TM), lambda i, h: (h, 0, i))
    return pl.pallas_call(
        _topk_kernel,
        out_shape=(shp, shp, shp_bf, shp_bf),
        grid=(N_ROWS // TM, PEER_HEADS),
        in_specs=[pl.BlockSpec((1, 2 * N_KEYS, TM), lambda i, h: (h, 0, i))],
        out_specs=(spec, spec, spec, spec),
        compiler_params=_cparams(("parallel", "parallel")),
        name="peer_topk",
    )(st)


def _peer_kernel(xn_ref, h1_ref, u_ref, vt_ref, rho_ref, c1_ref, r2_ref, e2_ref, o_ref,
                 acc_ref, gt_ref, at_ref):
    ci = pl.program_id(1)
    slot = ci % 2

    @pl.when(ci == 0)
    def _():
        acc_ref[...] = jnp.zeros_like(acc_ref)
        gt_ref[1] = jnp.zeros((EC, TT), BF16)

    at_ref[...] = lax.dot_general(u_ref[...], xn_ref[...], (((1,), (1,)), ((), ())),
                                  preferred_element_type=F32)
    acc_ref[...] += jnp.dot(vt_ref[...], gt_ref[1 - slot], preferred_element_type=F32)

    sub = 16
    for ii in range(EC // N_KEYS):
        for tb in range(TT // PEER_LANES):
            cols = slice(tb * PEER_LANES, (tb + 1) * PEER_LANES)
            w = [jnp.zeros((sub, PEER_LANES), BF16) for _ in range(N_KEYS // sub)]
            for hh in range(PEER_HEADS):
                rho = jnp.broadcast_to(rho_ref[hh, ii:ii + 1, cols], (sub, PEER_LANES)).astype(BF16)
                cc = jnp.broadcast_to(c1_ref[hh, ii:ii + 1, cols], (sub, PEER_LANES)).astype(BF16)
                for jb in range(N_KEYS // sub):
                    jr = slice(jb * sub, (jb + 1) * sub)
                    w[jb] = w[jb] + jnp.where(r2_ref[hh, jr, cols] < rho, e2_ref[hh, jr, cols] * cc, 0.0)
            for jb in range(N_KEYS // sub):
                rows = slice(ii * N_KEYS + jb * sub, ii * N_KEYS + (jb + 1) * sub)
                a = at_ref[rows, cols]
                act = 0.5 * a * (1.0 + lax.erf(a * (2.0 ** -0.5)))
                gt_ref[slot, rows, cols] = w[jb] * act.astype(BF16)

    @pl.when(ci == pl.num_programs(1) - 1)
    def _():
        o_ref[...] = h1_ref[...] + acc_ref[...].T


def _peer(xn, h1, u_bf, vt_bf, rho, c1, r2, e2):
    nk = EC // N_KEYS
    nc = N_EXPERTS // EC
    sel = pl.BlockSpec((PEER_HEADS, nk, TT), lambda t, c: (0, jnp.minimum(c, nc - 1), t))
    full = pl.BlockSpec((PEER_HEADS, N_KEYS, TT), lambda t, c: (0, 0, t))
    return pl.pallas_call(
        _peer_kernel,
        out_shape=jax.ShapeDtypeStruct((N_ROWS, D_MODEL), F32),
        grid=(N_ROWS // TT, nc + 1),
        in_specs=[pl.BlockSpec((TT, D_MODEL), lambda t, c: (t, 0)),
                  pl.BlockSpec((TT, D_MODEL), lambda t, c: (t, 0)),
                  pl.BlockSpec((EC, D_MODEL), lambda t, c: (jnp.minimum(c, nc - 1), 0)),
                  pl.BlockSpec((D_MODEL, EC), lambda t, c: (0, jnp.maximum(c - 1, 0))),
                  sel, sel, full, full],
        out_specs=pl.BlockSpec((TT, D_MODEL), lambda t, c: (t, 0)),
        scratch_shapes=[pltpu.VMEM((D_MODEL, TT), F32), pltpu.VMEM((2, EC, TT), BF16),
                        pltpu.VMEM((EC, TT), F32)],
        compiler_params=_cparams(("parallel", "arbitrary")),
        name="peer_dense",
    )(xn, h1, u_bf, vt_bf, rho, c1, r2, e2)


def _layernorm(xf, g, b):
    mu = jnp.mean(xf, axis=-1, keepdims=True)
    var = jnp.mean(jnp.square(xf - mu), axis=-1, keepdims=True)
    return (xf - mu) * lax.rsqrt(var + EPS) * g + b


def _rms(x, g):
    return x * lax.rsqrt(jnp.mean(x * x, axis=-1, keepdims=True) + EPS) * g


def _rotary(x, pos):
    half = x.shape[-1] // 2
    inv = 1.0 / (10000.0 ** (jnp.arange(half, dtype=F32) / half))
    ang = pos.astype(F32)[:, None] * inv[None]
    cos, sin = jnp.cos(ang)[:, None, :], jnp.sin(ang)[:, None, :]
    x1, x2 = x[..., :half], x[..., half:]
    return jnp.concatenate([x1 * cos - x2 * sin, x1 * sin + x2 * cos], axis=-1)


def _t5_bucket(dist):
    max_exact = N_BUCKETS // 2
    df = jnp.maximum(dist, 1).astype(F32)
    large = max_exact + (jnp.log(df / max_exact) / math.log(REL_MAX_DIST / max_exact)
                         * (N_BUCKETS - max_exact)).astype(jnp.int32)
    large = jnp.minimum(large, N_BUCKETS - 1)
    return jnp.where(dist < max_exact, dist, large)


def _sample_mixers(ps, l, cache_meta_kv, cache_swa_kv, state_ret, state_conv, lw, rel_bias):
    db, s = DEC_BATCH, DEC_SEQ
    pos = PAST_LEN + jnp.arange(s)
    c_in = ps[..., C_CONV:C_RQ]
    rq, rk = ps[..., C_RQ:C_RK], ps[..., C_RK:C_RV]
    rv, rg = ps[..., C_RV:C_RG], ps[..., C_RG:C_SQ]
    sq, sk, sv = ps[..., C_SQ:C_SK], ps[..., C_SK:C_SV], ps[..., C_SV:]
    glu = c_in[..., :CONV_CH] * jax.nn.sigmoid(c_in[..., CONV_CH:])
    xin = jnp.concatenate([state_conv[l], glu], axis=1)
    y = sum(xin[:, k:k + s] * lw["conv_w"][k][None, None] for k in range(CONV_WIDTH))
    yf = _layernorm(y + lw["conv_b"], lw["conv_ln_g"], lw["conv_ln_b"])
    conv_o = jax.nn.silu(yf)
    new_conv = xin[:, -(CONV_WIDTH - 1):]
    lg = jnp.log(1.0 - 2.0 ** (-5.0 - jnp.arange(RET_HEADS, dtype=F32)))
    q = _rotary(rq.reshape(db, s, RET_HEADS, RET_DK), pos).transpose(0, 2, 1, 3)
    k = (_rotary(rk.reshape(db, s, RET_HEADS, RET_DK), pos) * (RET_DK ** -0.5)).transpose(0, 2, 1, 3)
    v = rv.reshape(db, s, RET_HEADS, RET_DV).transpose(0, 2, 1, 3)
    st = state_ret[l]
    i = jnp.arange(s, dtype=F32)
    diff = i[:, None] - i[None, :]
    decay = jnp.where(diff >= 0, jnp.exp(jnp.maximum(diff, 0.0)[None] * lg[:, None, None]), 0.0)
    inner = jnp.einsum("bhid,bhjd->bhij", q, k) * decay
    o = (jnp.einsum("bhij,bhje->bhie", inner, v)
         + jnp.exp((i[None] + 1.0) * lg[:, None])[None, :, :, None] * jnp.einsum("bhid,bhde->bhie", q, st))
    k_dec = k * jnp.exp((s - 1.0 - i)[None] * lg[:, None])[None, :, :, None]
    new_ret = jnp.exp(s * lg)[None, :, None, None] * st + jnp.einsum("bhjd,bhje->bhde", k_dec, v)
    o = o.transpose(0, 2, 1, 3)
    mu = jnp.mean(o, axis=-1, keepdims=True)
    var = jnp.mean(jnp.square(o - mu), axis=-1, keepdims=True)
    yr = ((o - mu) * lax.rsqrt(var + EPS)).reshape(db, s, RET_HEADS * RET_DV)
    ret_o = (yr * lw["ret_gn_g"] + lw["ret_gn_b"]) * jax.nn.silu(rg)
    qs = _rms(sq.reshape(db, s, SWA_KV_HEADS, SWA_GROUP, SWA_HD), lw["swa_q_norm"])
    ks = _rms(sk.reshape(db, s, SWA_KV_HEADS, SWA_HD), lw["swa_k_norm"])
    vs = sv.reshape(db, s, SWA_KV_HEADS, SWA_HD)
    new_kv = jnp.stack([ks, vs], axis=2)
    kv = jnp.concatenate([cache_meta_kv[l], cache_swa_kv[l], new_kv], axis=1)
    kpos = jnp.concatenate([jnp.arange(N_META), PAST_LEN - WINDOW + jnp.arange(WINDOW), pos])
    is_meta = jnp.arange(kpos.shape[0]) < N_META
    sc = jnp.einsum("bqkgd,bskd->bkgqs", qs, kv[:, :, 0]) * (SWA_HD ** -0.5)
    dist = pos[:, None] - kpos[None, :]
    valid = jnp.where(is_meta, kpos[None, :] <= pos[:, None],
                      (dist >= 0) & (dist < WINDOW) & (kpos[None, :] >= N_META))
    bias = jnp.take(rel_bias.astype(F32), _t5_bucket(jnp.maximum(dist, 0)), axis=0)
    bias = jnp.moveaxis(bias, -1, 0).reshape(SWA_KV_HEADS, SWA_GROUP, s, -1)
    sc = jnp.where(valid[None, None, None], sc + bias[None], NEG)
    sink = jnp.broadcast_to(lw["swa_sinks"].reshape(1, SWA_KV_HEADS, SWA_GROUP, 1, 1), sc.shape[:-1] + (1,))
    pr = jax.nn.softmax(jnp.concatenate([sc, sink], axis=-1), axis=-1)[..., :-1]
    swa_o = jnp.einsum("bkgqs,bskd->bqkgd", pr, kv[:, :, 1]).reshape(db, s, SWA_HEADS * SWA_HD)
    new_win = jnp.concatenate([cache_swa_kv[l], new_kv], axis=1)[:, -WINDOW:]
    return conv_o, ret_o, swa_o, new_win, new_ret, new_conv


def kernel(x_prompt, x_sample, cache_meta_kv, cache_swa_kv, state_ret, state_conv, meta_tokens, rel_bias,
           norm_mix, w_in, conv_w, conv_b, conv_ln_g, conv_ln_b, ret_gn_g, ret_gn_b, swa_q_norm,
           swa_k_norm, swa_sinks, w_out, norm_ffn, peer_wq, peer_keys, peer_u, peer_v):
    meta = jnp.broadcast_to(meta_tokens.astype(F32)[None], (BATCH, N_META, D_MODEL))
    pad = jnp.zeros((BATCH, LP - L_REAL, D_MODEL), F32)
    hp = jnp.concatenate([meta, x_prompt, pad], axis=1).reshape(NP_ROWS, D_MODEL)
    h = jnp.concatenate([hp, x_sample.reshape(NS_ROWS, D_MODEL)], axis=0)

    cos_t, sin_t = _rotary_tables(jnp.arange(LP))
    ret_tabs = _ret_tables()
    bias_tabs = _bias_tables(rel_bias)

    meta_p, win_p, ret_p, conv_p, win_s, ret_s, conv_s = [], [], [], [], [], [], []
    for l in range(DEPTH):
        lw = {"conv_w": conv_w[l], "conv_b": conv_b[l], "conv_ln_g": conv_ln_g[l], "conv_ln_b": conv_ln_b[l],
              "ret_gn_g": ret_gn_g[l], "ret_gn_b": ret_gn_b[l], "swa_q_norm": swa_q_norm[l],
              "swa_k_norm": swa_k_norm[l], "swa_sinks": swa_sinks[l]}
        p = _norm_proj(h, norm_mix[l], w_in[l].astype(BF16))

        w32 = jnp.concatenate([conv_w[l], jnp.zeros((1, CONV_CH), F32)], axis=0)
        conv_o, conv_tail = _conv_prompt(p, w32, conv_b[l], conv_ln_g[l], conv_ln_b[l])
        ret_o, ret_state = _ret_prompt(p, cos_t, sin_t, ret_tabs, ret_gn_g[l], ret_gn_b[l])
        swa_o, k_normed = _swa_prompt(p, swa_sinks[l], swa_q_norm[l], swa_k_norm[l], bias_tabs)

        ps = p[NP_ROWS:].reshape(DEC_BATCH, DEC_SEQ, PROJ_COLS)
        s_conv, s_ret, s_swa, new_win, new_ret, new_conv = _sample_mixers(
            ps, l, cache_meta_kv, cache_swa_kv, state_ret, state_conv, lw, rel_bias)
        conv_o = lax.dynamic_update_slice(conv_o, s_conv.reshape(NS_ROWS, -1), (NP_ROWS, 0))
        ret_o = lax.dynamic_update_slice(ret_o, s_ret.reshape(NS_ROWS, -1), (NP_ROWS, 0))
        swa_o = lax.dynamic_update_slice(swa_o, s_swa.reshape(NS_ROWS, -1), (NP_ROWS, 0))

        kbd = jnp.zeros((2 * N_KEYS, 2 * PEER_HALF), F32)
        kbd = kbd.at[:N_KEYS, :PEER_HALF].set(peer_keys[l, 0]).at[N_KEYS:, PEER_HALF:].set(peer_keys[l, 1])
        h1, xn, st = _out_ffn(h, conv_o, ret_o, swa_o, w_out[l].astype(BF16), norm_ffn[l],
                              peer_wq[l].astype(BF16), kbd.astype(BF16))
        rho, c1, r2, e2 = _topk(st)
        h = _peer(xn, h1, peer_u[l].astype(BF16), peer_v[l].T.astype(BF16), rho, c1, r2, e2)

        kp = k_normed.reshape(BATCH, LP, SWA_KV_HEADS, SWA_HD)
        vp = p[:NP_ROWS, C_SV:].reshape(BATCH, LP, SWA_KV_HEADS, SWA_HD)
        kvp = jnp.stack([kp, vp], axis=2)
        meta_p.append(kvp[:, :N_META])
        win_p.append(kvp[:, L_REAL - WINDOW:L_REAL])
        ret_p.append(ret_state)
        conv_p.append(conv_tail[:, 32 - (CONV_WIDTH - 1):])
        win_s.append(new_win)
        ret_s.append(new_ret)
        conv_s.append(new_conv)

    y_prompt = h[:NP_ROWS].reshape(BATCH, LP, D_MODEL)[:, N_META:L_REAL]
    y_sample = h[NP_ROWS:].reshape(DEC_BATCH, DEC_SEQ, D_MODEL)
    return (y_prompt, y_sample, jnp.stack(meta_p), jnp.stack(win_p), jnp.stack(ret_p), jnp.stack(conv_p),
            jnp.stack(win_s), jnp.stack(ret_s), jnp.stack(conv_s))
```

```python
import functools
import math

import numpy as np
import jax
import jax.numpy as jnp
from jax import lax
from jax.experimental import pallas as pl
from jax.experimental.pallas import tpu as pltpu

F32 = jnp.float32
BF16 = jnp.bfloat16

D_MODEL = 1024
BATCH = 2
SEQ = 8192
DEPTH = 2
DEC_BATCH = 128
DEC_SEQ = 8
PAST_LEN = 8192
N_META = 16
CONV_CH = 512
CONV_WIDTH = 31
RET_HEADS = 4
RET_DK = 64
RET_DV = 128
SWA_HEADS = 8
SWA_KV_HEADS = 2
SWA_GROUP = SWA_HEADS // SWA_KV_HEADS
SWA_HD = 64
WINDOW = 128
N_BUCKETS = 32
REL_MAX_DIST = 128
PEER_HEADS = 8
N_KEYS = 128
N_EXPERTS = N_KEYS * N_KEYS
PEER_TOPK = 16
PEER_HALF = 64
EPS = 1e-6
NEG = -1e30

PROJ_COLS = 3328
C_CONV, C_RQ, C_RK, C_RV, C_RG, C_SQ, C_SK, C_SV = 0, 1024, 1280, 1536, 2048, 2560, 3072, 3200

L_REAL = N_META + SEQ
BLK = 128
N_BLK = 65
LP = N_BLK * BLK
NP_ROWS = BATCH * LP
NS_ROWS = DEC_BATCH * DEC_SEQ
N_ROWS = NP_ROWS + NS_ROWS
LAST_REAL = L_REAL - (N_BLK - 1) * BLK

TM = 384
TO = 256
TT = 768
SG = 8
EC = 1024
PEER_LANES = 256
CONV_T = 640
CONV_RB = 64
VMEM_LIMIT = 56 * 1024 * 1024


def _cparams(sem):
    return pltpu.CompilerParams(dimension_semantics=sem, vmem_limit_bytes=VMEM_LIMIT)


def _norm_proj_kernel(x_ref, g_ref, w_ref, o_ref):
    x = x_ref[...]
    ms = jnp.mean(x * x, axis=-1, keepdims=True)
    xn = x * lax.rsqrt(ms + EPS) * g_ref[...]
    o_ref[...] = jnp.dot(xn.astype(BF16), w_ref[...], preferred_element_type=F32)


def _norm_proj(h, g, w_bf):
    return pl.pallas_call(
        _norm_proj_kernel,
        out_shape=jax.ShapeDtypeStruct((N_ROWS, PROJ_COLS), F32),
        grid=(N_ROWS // TM,),
        in_specs=[pl.BlockSpec((TM, D_MODEL), lambda i: (i, 0)),
                  pl.BlockSpec((1, D_MODEL), lambda i: (0, 0)),
                  pl.BlockSpec((D_MODEL, PROJ_COLS), lambda i: (0, 0))],
        out_specs=pl.BlockSpec((TM, PROJ_COLS), lambda i: (i, 0)),
        compiler_params=_cparams(("parallel",)),
        name="norm_proj",
    )(h, g.reshape(1, D_MODEL), w_bf)


def _conv_kernel(c_ref, w_ref, b_ref, g_ref, beta_ref, o_ref, st_ref, xin_ref):
    t = pl.program_id(1)

    @pl.when(t == 0)
    def _():
        xin_ref[0:32, :] = jnp.zeros((32, CONV_CH), F32)

    @pl.when(t > 0)
    def _():
        xin_ref[0:32, :] = xin_ref[CONV_T:CONV_T + 32, :]

    c = c_ref[...]
    xin_ref[32:32 + CONV_T, :] = c[:, :CONV_CH] * jax.nn.sigmoid(c[:, CONV_CH:])

    w = w_ref[...]
    bias = b_ref[...]
    gam = g_ref[...]
    beta = beta_ref[...]
    for rb in range(CONV_T // CONV_RB):
        r0 = rb * CONV_RB
        acc = jnp.zeros((CONV_RB, CONV_CH), F32)
        for k in range(CONV_WIDTH):
            acc = acc + xin_ref[r0 + 2 + k:r0 + 2 + k + CONV_RB, :] * w[k:k + 1, :]
        y = acc + bias
        mu = jnp.mean(y, axis=-1, keepdims=True)
        d = y - mu
        var = jnp.mean(d * d, axis=-1, keepdims=True)
        yn = d * lax.rsqrt(var + EPS) * gam + beta
        o_ref[r0:r0 + CONV_RB, :] = yn * jax.nn.sigmoid(yn)

    @pl.when(t == pl.num_programs(1) - 1)
    def _():
        lo = 32 + (L_REAL - 32) - (LP - CONV_T)
        st_ref[0] = xin_ref[lo:lo + 32, :]


def _conv_prompt(p, w32, b, g, beta):
    nt = LP // CONV_T
    return pl.pallas_call(
        _conv_kernel,
        out_shape=(jax.ShapeDtypeStruct((NP_ROWS, CONV_CH), F32),
                   jax.ShapeDtypeStruct((BATCH, 32, CONV_CH), F32)),
        grid=(BATCH, nt),
        in_specs=[pl.BlockSpec((CONV_T, 2 * CONV_CH), lambda bi, t: (bi * nt + t, 0)),
                  pl.BlockSpec((32, CONV_CH), lambda bi, t: (0, 0)),
                  pl.BlockSpec((1, CONV_CH), lambda bi, t: (0, 0)),
                  pl.BlockSpec((1, CONV_CH), lambda bi, t: (0, 0)),
                  pl.BlockSpec((1, CONV_CH), lambda bi, t: (0, 0))],
        out_specs=(pl.BlockSpec((CONV_T, CONV_CH), lambda bi, t: (bi * nt + t, 0)),
                   pl.BlockSpec((1, 32, CONV_CH), lambda bi, t: (bi, 0, 0))),
        scratch_shapes=[pltpu.VMEM((32 + CONV_T + 32, CONV_CH), F32)],
        compiler_params=_cparams(("arbitrary", "arbitrary")),
        name="conv_prompt",
    )(p, w32, b.reshape(1, -1), g.reshape(1, -1), beta.reshape(1, -1))


def _swap_halves(x, first_half):
    return jnp.where(first_half, pltpu.roll(x, x.shape[1] - 32, axis=1), pltpu.roll(x, 32, axis=1))


def _ret_kernel(q_ref, k_ref, v_ref, rg_ref, cos_ref, sin_ref, dmat_ref, dq_ref, dk_ref, sg_ref,
                gng_ref, gnb_ref, o_ref, st_ref, s_ref):
    j = pl.program_id(1)

    @pl.when(j == 0)
    def _():
        s_ref[...] = jnp.zeros_like(s_ref)

    cos = cos_ref[...]
    sin = sin_ref[...]
    lane = lax.broadcasted_iota(jnp.int32, (BLK, RET_HEADS * RET_DK), 1)
    first_half = (lane % RET_DK) < (RET_DK // 2)
    q = q_ref[...]
    k = k_ref[...]
    q = q * cos + _swap_halves(q, first_half) * sin
    k = (k * cos + _swap_halves(k, first_half) * sin) * (RET_DK ** -0.5)
    v = v_ref[...]
    rg = rg_ref[...]
    for h in range(RET_HEADS):
        qh = q[:, h * RET_DK:(h + 1) * RET_DK].astype(BF16)
        kh = k[:, h * RET_DK:(h + 1) * RET_DK]
        vh = v[:, h * RET_DV:(h + 1) * RET_DV].astype(BF16)
        s_old = s_ref[h]
        inner = lax.dot_general(qh, kh.astype(BF16), (((1,), (1,)), ((), ())),
                                preferred_element_type=F32) * dmat_ref[h]
        o = (jnp.dot(inner.astype(BF16), vh, preferred_element_type=F32)
             + dq_ref[h] * jnp.dot(qh, s_old.astype(BF16), preferred_element_type=F32))
        kdec_t = (kh * dk_ref[0, h]).T.astype(BF16)
        s_new = sg_ref[0, h, 0:1, :] * s_old + jnp.dot(kdec_t, vh, preferred_element_type=F32)
        s_ref[h] = s_new
        mu = jnp.mean(o, axis=-1, keepdims=True)
        d = o - mu
        var = jnp.mean(d * d, axis=-1, keepdims=True)
        y = d * lax.rsqrt(var + EPS) * gng_ref[:, h * RET_DV:(h + 1) * RET_DV] \
            + gnb_ref[:, h * RET_DV:(h + 1) * RET_DV]
        gate = rg[:, h * RET_DV:(h + 1) * RET_DV]
        o_ref[:, h * RET_DV:(h + 1) * RET_DV] = y * (gate * jax.nn.sigmoid(gate))

    @pl.when(j == pl.num_programs(1) - 1)
    def _():
        st_ref[0] = s_ref[...]


def _ret_tables():
    lg = jnp.log(1.0 - 2.0 ** (-5.0 - jnp.arange(RET_HEADS, dtype=F32)))
    i = jnp.arange(BLK, dtype=F32)
    diff = i[:, None] - i[None, :]
    dmat = jnp.where(diff >= 0, jnp.exp(jnp.maximum(diff, 0.0)[None] * lg[:, None, None]), 0.0)
    dq = jnp.broadcast_to(jnp.exp((i[None] + 1.0) * lg[:, None])[:, :, None], (RET_HEADS, BLK, RET_DV))

    def kdec(c_eff):
        e = jnp.where(i[None] < c_eff, jnp.exp((c_eff - 1.0 - i)[None] * lg[:, None]), 0.0)
        return jnp.broadcast_to(e[:, :, None], (RET_HEADS, BLK, RET_DK))

    def sgam(c_eff):
        return jnp.broadcast_to(jnp.exp(c_eff * lg)[:, None, None], (RET_HEADS, 8, RET_DV))

    dk = jnp.stack([kdec(float(BLK)), kdec(float(LAST_REAL))])
    sg = jnp.stack([sgam(float(BLK)), sgam(float(LAST_REAL))])
    return dmat, dq, dk, sg


def _rotary_tables(pos):
    half = RET_DK // 2
    inv = 1.0 / (10000.0 ** (jnp.arange(half, dtype=F32) / half))
    ang = pos.astype(F32)[:, None] * inv[None]
    cos, sin = jnp.cos(ang), jnp.sin(ang)
    cos_t = jnp.tile(jnp.concatenate([cos, cos], axis=-1), (1, RET_HEADS))
    sin_t = jnp.tile(jnp.concatenate([-sin, sin], axis=-1), (1, RET_HEADS))
    return cos_t, sin_t


def _ret_prompt(p, cos_t, sin_t, tabs, gn_g, gn_b):
    dmat, dq, dk, sg = tabs
    nb = N_BLK
    last = nb - 1
    return pl.pallas_call(
        _ret_kernel,
        out_shape=(jax.ShapeDtypeStruct((NP_ROWS, RET_HEADS * RET_DV), F32),
                   jax.ShapeDtypeStruct((BATCH, RET_HEADS, RET_DK, RET_DV), F32)),
        grid=(BATCH, nb),
        in_specs=[pl.BlockSpec((BLK, 256), lambda b, j: (b * nb + j, C_RQ // 256)),
                  pl.BlockSpec((BLK, 256), lambda b, j: (b * nb + j, C_RK // 256)),
                  pl.BlockSpec((BLK, 512), lambda b, j: (b * nb + j, C_RV // 512)),
                  pl.BlockSpec((BLK, 512), lambda b, j: (b * nb + j, C_RG // 512)),
                  pl.BlockSpec((BLK, 256), lambda b, j: (j, 0)),
                  pl.BlockSpec((BLK, 256), lambda b, j: (j, 0)),
                  pl.BlockSpec((RET_HEADS, BLK, BLK), lambda b, j: (0, 0, 0)),
                  pl.BlockSpec((RET_HEADS, BLK, RET_DV), lambda b, j: (0, 0, 0)),
                  pl.BlockSpec((1, RET_HEADS, BLK, RET_DK), lambda b, j: (j // last, 0, 0, 0)),
                  pl.BlockSpec((1, RET_HEADS, 8, RET_DV), lambda b, j: (j // last, 0, 0, 0)),
                  pl.BlockSpec((1, 512), lambda b, j: (0, 0)),
                  pl.BlockSpec((1, 512), lambda b, j: (0, 0))],
        out_specs=(pl.BlockSpec((BLK, 512), lambda b, j: (b * nb + j, 0)),
                   pl.BlockSpec((1, RET_HEADS, RET_DK, RET_DV), lambda b, j: (b, 0, 0, 0))),
        scratch_shapes=[pltpu.VMEM((RET_HEADS, RET_DK, RET_DV), F32)],
        compiler_params=_cparams(("arbitrary", "arbitrary")),
        name="ret_prompt",
    )(p, p, p, p, cos_t, sin_t, dmat, dq, dk, sg, gn_g.reshape(1, -1), gn_b.reshape(1, -1))


def _group_rms(x, ones_bd, w):
    x2 = x * x
    hi = x2.astype(BF16)
    lo = (x2 - hi.astype(F32)).astype(BF16)
    ss = (jnp.dot(hi, ones_bd, preferred_element_type=F32)
          + jnp.dot(lo, ones_bd, preferred_element_type=F32))
    return x * lax.rsqrt(ss * (1.0 / SWA_HD) + EPS) * w


def _swa_kernel(sinks_ref, q_ref, kc_ref, vc_ref, kp_ref, vp_ref, km_ref, vm_ref, ones_ref,
                qn_ref, kn_ref, bc_ref, bp_ref, bm_ref, o_ref, kout_ref):
    j = pl.program_id(1)
    ones_q = ones_ref[...]
    ones_k = ones_ref[0:128, 0:128]
    qw = qn_ref[...]
    kw = kn_ref[...]
    q = _group_rms(q_ref[...], ones_q, qw) * (SWA_HD ** -0.5)
    kc = _group_rms(kc_ref[...], ones_k, kw)
    kp = _group_rms(kp_ref[...], ones_k, kw)
    km = _group_rms(km_ref[...], ones_k, kw)
    kout_ref[...] = kc
    vc = vc_ref[...].astype(BF16)
    vp = vp_ref[...].astype(BF16)
    vm = vm_ref[...].astype(BF16)

    qi = lax.broadcasted_iota(jnp.int32, (BLK, BLK), 0)
    kj = lax.broadcasted_iota(jnp.int32, (BLK, BLK), 1)
    valid_c = jnp.where(kj <= qi, j * BLK + kj, -1) >= N_META
    valid_p = jnp.where(kj > qi, (j - 1) * BLK + kj, -1) >= N_META
    qm = lax.broadcasted_iota(jnp.int32, (BLK, N_META), 0)
    mm = lax.broadcasted_iota(jnp.int32, (BLK, N_META), 1)
    valid_m = mm <= j * BLK + qm

    dn = (((1,), (1,)), ((), ()))
    for h in range(SWA_HEADS):
        kv = h // SWA_GROUP
        qh = q[:, h * SWA_HD:(h + 1) * SWA_HD].astype(BF16)
        ksl = slice(kv * SWA_HD, (kv + 1) * SWA_HD)
        s_c = lax.dot_general(qh, kc[:, ksl].astype(BF16), dn, preferred_element_type=F32)
        s_p = lax.dot_general(qh, kp[:, ksl].astype(BF16), dn, preferred_element_type=F32)
        s_m = lax.dot_general(qh, km[:, ksl].astype(BF16), dn, preferred_element_type=F32)
        s_c = jnp.where(valid_c, s_c + bc_ref[h], NEG)
        s_p = jnp.where(valid_p, s_p + bp_ref[h], NEG)
        s_m = jnp.where(valid_m, s_m + bm_ref[0, h], NEG)
        sink = sinks_ref[h]
        m = jnp.maximum(jnp.maximum(jnp.max(s_c, axis=-1, keepdims=True),
                                    jnp.max(s_p, axis=-1, keepdims=True)),
                        jnp.maximum(jnp.max(s_m, axis=-1, keepdims=True), sink))
        p_c = jnp.exp(s_c - m)
        p_p = jnp.exp(s_p - m)
        p_m = jnp.exp(s_m - m)
        den = (jnp.sum(p_c, axis=-1, keepdims=True) + jnp.sum(p_p, axis=-1, keepdims=True)
               + jnp.sum(p_m, axis=-1, keepdims=True) + jnp.exp(sink - m))
        acc = (jnp.dot(p_c.astype(BF16), vc[:, ksl], preferred_element_type=F32)
               + jnp.dot(p_p.astype(BF16), vp[:, ksl], preferred_element_type=F32)
               + jnp.dot(p_m.astype(BF16), vm[:, ksl], preferred_element_type=F32))
        o_ref[:, h * SWA_HD:(h + 1) * SWA_HD] = acc / den


def _t5_bucket_np(dist):
    max_exact = N_BUCKETS // 2
    df = np.maximum(dist, 1).astype(np.float64)
    large = max_exact + (np.log(df / max_exact) / math.log(REL_MAX_DIST / max_exact)
                         * (N_BUCKETS - max_exact)).astype(np.int64)
    large = np.minimum(large, N_BUCKETS - 1)
    return np.where(dist < max_exact, dist, large).astype(np.int32)


def _bias_tables(rel_bias):
    qi = np.arange(BLK)[:, None]
    kj = np.arange(BLK)[None, :]
    b_cur = _t5_bucket_np(np.maximum(qi - kj, 0))
    b_prev = _t5_bucket_np(np.maximum(qi - kj + BLK, 0))
    m = np.arange(N_META)[None, :]
    b_m0 = _t5_bucket_np(np.maximum(qi - m, 0))
    b_m1 = _t5_bucket_np(np.maximum(qi + BLK - m, 0))
    assert (b_m1 == N_BUCKETS - 1).all()
    rb = rel_bias.astype(F32)

    def look(bk):
        return jnp.moveaxis(jnp.take(rb, jnp.asarray(bk), axis=0), -1, 0)

    return look(b_cur), look(b_prev), jnp.stack([look(b_m0), look(b_m1)])


def _swa_prompt(p, sinks, q_norm, k_norm, bias_tabs):
    b_cur, b_prev, b_meta = bias_tabs
    nb = N_BLK
    ones_bd = jnp.asarray(np.kron(np.eye(SWA_HEADS), np.ones((SWA_HD, SWA_HD))), BF16)
    qn = jnp.tile(q_norm.astype(F32), SWA_HEADS).reshape(1, -1)
    kn = jnp.tile(k_norm.astype(F32), SWA_KV_HEADS).reshape(1, -1)
    ck, cv = C_SK // 128, C_SV // 128
    return pl.pallas_call(
        _swa_kernel,
        out_shape=(jax.ShapeDtypeStruct((NP_ROWS, SWA_HEADS * SWA_HD), F32),
                   jax.ShapeDtypeStruct((NP_ROWS, SWA_KV_HEADS * SWA_HD), F32)),
        grid=(BATCH, nb),
        in_specs=[pl.BlockSpec(memory_space=pltpu.SMEM),
                  pl.BlockSpec((BLK, 512), lambda b, j: (b * nb + j, C_SQ // 512)),
                  pl.BlockSpec((BLK, 128), lambda b, j: (b * nb + j, ck)),
                  pl.BlockSpec((BLK, 128), lambda b, j: (b * nb + j, cv)),
                  pl.BlockSpec((BLK, 128), lambda b, j: (b * nb + jnp.maximum(j - 1, 0), ck)),
                  pl.BlockSpec((BLK, 128), lambda b, j: (b * nb + jnp.maximum(j - 1, 0), cv)),
                  pl.BlockSpec((N_META, 128), lambda b, j: (b * (LP // N_META), ck)),
                  pl.BlockSpec((N_META, 128), lambda b, j: (b * (LP // N_META), cv)),
                  pl.BlockSpec((512, 512), lambda b, j: (0, 0)),
                  pl.BlockSpec((1, 512), lambda b, j: (0, 0)),
                  pl.BlockSpec((1, 128), lambda b, j: (0, 0)),
                  pl.BlockSpec((SWA_HEADS, BLK, BLK), lambda b, j: (0, 0, 0)),
                  pl.BlockSpec((SWA_HEADS, BLK, BLK), lambda b, j: (0, 0, 0)),
                  pl.BlockSpec((1, SWA_HEADS, BLK, N_META), lambda b, j: (jnp.minimum(j, 1), 0, 0, 0))],
        out_specs=(pl.BlockSpec((BLK, 512), lambda b, j: (b * nb + j, 0)),
                   pl.BlockSpec((BLK, 128), lambda b, j: (b * nb + j, 0))),
        compiler_params=_cparams(("parallel", "parallel")),
        name="swa_prompt",
    )(sinks.astype(F32), p, p, p, p, p, p, p, ones_bd, qn, kn, b_cur, b_prev, b_meta)


def _out_ffn_kernel(h_ref, a_ref, b_ref, c_ref, as_ref, bs_ref, cs_ref, wo_ref, g_ref, wq_ref, kbd_ref,
                    h1_ref, xn_ref, st_ref):
    is_prompt = pl.program_id(0) < NP_ROWS // TO
    a = jnp.where(is_prompt, a_ref[...], as_ref[...])
    b = jnp.where(is_prompt, b_ref[...], bs_ref[...])
    c = jnp.where(is_prompt, c_ref[...], cs_ref[...])
    h1 = (h_ref[...]
          + jnp.dot(a.astype(BF16), wo_ref[0:512, :], preferred_element_type=F32)
          + jnp.dot(b.astype(BF16), wo_ref[512:1024, :], preferred_element_type=F32)
          + jnp.dot(c.astype(BF16), wo_ref[1024:1536, :], preferred_element_type=F32))
    h1_ref[...] = h1
    ms = jnp.mean(h1 * h1, axis=-1, keepdims=True)
    xn_f = h1 * lax.rsqrt(ms + EPS) * g_ref[...]
    xn_ref[...] = xn_f.T.astype(BF16)
    q = jnp.dot(xn_f.astype(BF16), wq_ref[...], preferred_element_type=F32).astype(BF16)
    kbd = kbd_ref[...]
    for hh in range(PEER_HEADS):
        st_ref[hh] = lax.dot_general(kbd, q[:, hh * 128:(hh + 1) * 128], (((1,), (1,)), ((), ())),
                                     preferred_element_type=F32)


def _out_ffn(h, mix_p, mix_s, wo_bf, g, wq_bf, kbd_bf):
    npb = NP_ROWS // TO
    pspec = pl.BlockSpec((TO, 512), lambda i: (jnp.minimum(i, npb - 1), 0))
    sspec = pl.BlockSpec((TO, 512), lambda i: (jnp.maximum(i - npb, 0), 0))
    return pl.pallas_call(
        _out_ffn_kernel,
        out_shape=(jax.ShapeDtypeStruct((N_ROWS, D_MODEL), F32),
                   jax.ShapeDtypeStruct((D_MODEL, N_ROWS), BF16),
                   jax.ShapeDtypeStruct((PEER_HEADS, 2 * N_KEYS, N_ROWS), F32)),
        grid=(N_ROWS // TO,),
        in_specs=[pl.BlockSpec((TO, D_MODEL), lambda i: (i, 0)),
                  pspec, pspec, pspec, sspec, sspec, sspec,
                  pl.BlockSpec((1536, D_MODEL), lambda i: (0, 0)),
                  pl.BlockSpec((1, D_MODEL), lambda i: (0, 0)),
                  pl.BlockSpec((D_MODEL, D_MODEL), lambda i: (0, 0)),
                  pl.BlockSpec((2 * N_KEYS, 128), lambda i: (0, 0))],
        out_specs=(pl.BlockSpec((TO, D_MODEL), lambda i: (i, 0)),
                   pl.BlockSpec((D_MODEL, TO), lambda i: (0, i)),
                   pl.BlockSpec((PEER_HEADS, 2 * N_KEYS, TO), lambda i: (0, 0, i))),
        compiler_params=_cparams(("parallel",)),
        name="out_ffn",
    )(h, *mix_p, *mix_s, wo_bf, g.reshape(1, -1), wq_bf, kbd_bf)


def _top_vals(x, n, with_rank=False):
    vals = []
    rank = jnp.full(x.shape, float(n), F32)
    for r in range(n):
        mx = jnp.max(x, axis=0, keepdims=True)
        vals.append(mx)
        hit = x == mx
        if with_rank:
            rank = jnp.where(hit, float(r), rank)
        x = jnp.where(hit, NEG, x)
    return (vals, rank) if with_rank else vals


def _topk_kernel(st_ref, rho_ref, c1_ref, r2_ref, e2_ref):
    s1 = st_ref[0, 0:N_KEYS, :]
    s2 = st_ref[0, N_KEYS:2 * N_KEYS, :]
    v1 = _top_vals(s1, PEER_TOPK)
    v2, rank2 = _top_vals(s2, PEER_TOPK, with_rank=True)
    sv1 = jnp.concatenate(v1, axis=0)
    sv2 = jnp.concatenate(v2, axis=0)
    cand = jnp.concatenate([v1[0] + sv2, v1[1] + sv2]
                           + [v1[a] + sv2[0:8] for a in range(2, 8)]
                           + [sv1[8:16] + v2[0]], axis=0)
    top = _top_vals(cand, PEER_TOPK)
    tau = top[PEER_TOPK - 1]
    z = jnp.ones_like(tau)
    for r in range(1, PEER_TOPK):
        z = z + jnp.exp(top[r] - top[0])
    rho = jnp.zeros(s1.shape, F32)
    for a in range(PEER_TOPK):
        n_a = jnp.sum(jnp.where(v1[a] + sv2 >= tau, 1.0, 0.0), axis=0, keepdims=True)
        rho = jnp.where(s1 == v1[a], n_a, rho)
    rho_ref[0] = rho
    c1_ref[0] = jnp.exp(s1 - v1[0]) / z
    r2_ref[0] = rank2.astype(BF16)
    e2_ref[0] = jnp.exp(s2 - v2[0]).astype(BF16)


def _topk(st):
    shp = jax.ShapeDtypeStruct((PEER_HEADS, N_KEYS, N_ROWS), F32)
    shp_bf = jax.ShapeDtypeStruct((PEER_HEADS, N_KEYS, N_ROWS), BF16)
    spec = pl.BlockSpec((1, N_KEYS, TM), lambda i, h: (h, 0, i))
    return pl.pallas_call(
        _topk_kernel,
        out_shape=(shp, shp, shp_bf, shp_bf),
        grid=(N_ROWS // TM, PEER_HEADS),
        in_specs=[pl.BlockSpec((1, 2 * N_KEYS, TM), lambda i, h: (h, 0, i))],
        out_specs=(spec, spec, spec, spec),
        compiler_params=_cparams(("parallel", "parallel")),
        name="peer_topk",
    )(st)


N_CHUNK = N_EXPERTS // EC
N_WORK = (N_ROWS // TT) * N_CHUNK


def _peer_stages(xn_ref, u_ref, vt_ref, rho_ref, c1_ref, r2_ref, e2_ref, acc_ref,
                 at_w, at_r, gt_w, gt_r):
    sub = 16
    n_tb = TT // PEER_LANES
    n_ii = EC // N_KEYS

    mrows = 128

    def key_matmul(ib, tb):
        rows = slice(ib * mrows, (ib + 1) * mrows)
        cols = slice(tb * PEER_LANES, (tb + 1) * PEER_LANES)
        at_w[rows, cols] = jnp.dot(u_ref[rows, :], xn_ref[:, cols], preferred_element_type=F32)

    def value_matmul(ib, tb):
        rows = slice(ib * mrows, (ib + 1) * mrows)
        cols = slice(tb * PEER_LANES, (tb + 1) * PEER_LANES)
        acc_ref[rows, cols] += jnp.dot(vt_ref[rows, :], gt_r[:, cols], preferred_element_type=F32)

    def gate_build(ii, tb, jbs):
        cols = slice(tb * PEER_LANES, (tb + 1) * PEER_LANES)
        w = {jb: jnp.zeros((sub, PEER_LANES), BF16) for jb in jbs}
        for hh in range(PEER_HEADS):
            rho = jnp.broadcast_to(rho_ref[hh, ii:ii + 1, cols], (sub, PEER_LANES)).astype(BF16)
            cc = jnp.broadcast_to(c1_ref[hh, ii:ii + 1, cols], (sub, PEER_LANES)).astype(BF16)
            for jb in jbs:
                jr = slice(jb * sub, (jb + 1) * sub)
                w[jb] = w[jb] + jnp.where(r2_ref[hh, jr, cols] < rho, e2_ref[hh, jr, cols] * cc, 0.0)
        for jb in jbs:
            rows = slice(ii * N_KEYS + jb * sub, ii * N_KEYS + (jb + 1) * sub)
            a = at_r[rows, cols]
            act = 0.5 * a * (1.0 + lax.erf(a * (2.0 ** -0.5)))
            gt_w[rows, cols] = w[jb] * act.astype(BF16)

    half = N_KEYS // sub // 2
    for tb in range(n_tb):
        for ii in range(n_ii):
            key_matmul(ii, tb)
            gate_build(ii, tb, range(0, half))
            value_matmul(ii, tb)
            gate_build(ii, tb, range(half, 2 * half))


def _peer_kernel(xn_ref, h1_ref, u_ref, vt_ref, rho_ref, c1_ref, r2_ref, e2_ref, o_ref,
                 acc_ref, at0, at1, gt0, gt1):
    g = pl.program_id(0)
    cv = (g - 2) % N_CHUNK

    @pl.when(g == 0)
    def _():
        at1[...] = jnp.zeros_like(at1)
        gt0[...] = jnp.zeros_like(gt0)
        gt1[...] = jnp.zeros_like(gt1)

    @pl.when(jnp.logical_or(g < 2, cv == 0))
    def _():
        acc_ref[...] = jnp.zeros_like(acc_ref)

    args = (xn_ref, u_ref, vt_ref, rho_ref, c1_ref, r2_ref, e2_ref, acc_ref)

    @pl.when(g % 2 == 0)
    def _():
        _peer_stages(*args, at0, at1, gt1, gt0)

    @pl.when(g % 2 == 1)
    def _():
        _peer_stages(*args, at1, at0, gt0, gt1)

    @pl.when(jnp.logical_and(g >= 2, cv == N_CHUNK - 1))
    def _():
        o_ref[...] = h1_ref[...] + acc_ref[...].T


def _peer(xn, h1, u_bf, vt_bf, rho, c1, r2, e2):
    nk = EC // N_KEYS

    def item(g, lag):
        w = jnp.clip(g - lag, 0, N_WORK - 1)
        return w // N_CHUNK, w % N_CHUNK

    return pl.pallas_call(
        _peer_kernel,
        out_shape=jax.ShapeDtypeStruct((N_ROWS, D_MODEL), F32),
        grid=(N_WORK + 2,),
        in_specs=[pl.BlockSpec((D_MODEL, TT), lambda g: (0, item(g, 0)[0])),
                  pl.BlockSpec((TT, D_MODEL), lambda g: (item(g, 2)[0], 0)),
                  pl.BlockSpec((EC, D_MODEL), lambda g: (item(g, 0)[1], 0)),
                  pl.BlockSpec((D_MODEL, EC), lambda g: (0, item(g, 2)[1])),
                  pl.BlockSpec((PEER_HEADS, nk, TT), lambda g: (0, item(g, 1)[1], item(g, 1)[0])),
                  pl.BlockSpec((PEER_HEADS, nk, TT), lambda g: (0, item(g, 1)[1], item(g, 1)[0])),
                  pl.BlockSpec((PEER_HEADS, N_KEYS, TT), lambda g: (0, 0, item(g, 1)[0])),
                  pl.BlockSpec((PEER_HEADS, N_KEYS, TT), lambda g: (0, 0, item(g, 1)[0]))],
        out_specs=pl.BlockSpec((TT, D_MODEL), lambda g: (item(g, 2)[0], 0)),
        scratch_shapes=[pltpu.VMEM((D_MODEL, TT), F32),
                        pltpu.VMEM((EC, TT), F32), pltpu.VMEM((EC, TT), F32),
                        pltpu.VMEM((EC, TT), BF16), pltpu.VMEM((EC, TT), BF16)],
        compiler_params=_cparams(("arbitrary",)),
        name="peer_dense",
    )(xn, h1, u_bf, vt_bf, rho, c1, r2, e2)


S_STACK = SWA_GROUP * DEC_SEQ
S_SMALL = N_META + DEC_SEQ
NT_DIMS = (((1,), (1,)), ((), ()))


def _sample_kernel(p_ref, cs_ref, rs_ref, meta_ref, win_ref,
                   cw_ref, cb_ref, cg_ref, cbeta_ref,
                   cos_ref, sin_ref, dmat_ref, dq_ref, dk_ref, sgam_ref, gng_ref, gnb_ref, eye_ref,
                   ones_ref, qn_ref, kn_ref, bwin_ref, bsm_ref, sink_ref,
                   conv_o_ref, ret_o_ref, swa_o_ref, ncs_ref, nrs_ref, nwin_ref,
                   xin_ref, kpad_ref, vpad_ref, ksm_ref, vsm_ref):
    kpad_ref[...] = jnp.zeros_like(kpad_ref)
    vpad_ref[...] = jnp.zeros_like(vpad_ref)
    ksm_ref[...] = jnp.zeros_like(ksm_ref)
    vsm_ref[...] = jnp.zeros_like(vsm_ref)
    lane = lax.broadcasted_iota(jnp.int32, (DEC_SEQ, RET_HEADS * RET_DK), 1)
    first_half = (lane % RET_DK) < (RET_DK // 2)
    cos = cos_ref[...]
    sin = sin_ref[...]
    ones_q = ones_ref[...]
    ones_k = ones_ref[0:128, 0:128]

    for s in range(SG):
        rows = slice(s * DEC_SEQ, (s + 1) * DEC_SEQ)
        p = p_ref[rows, :]

        glu = p[:, C_CONV:C_CONV + CONV_CH] * jax.nn.sigmoid(p[:, C_CONV + CONV_CH:C_RQ])
        xin_ref[0:CONV_WIDTH - 1, :] = cs_ref[0, s]
        xin_ref[CONV_WIDTH - 1:CONV_WIDTH - 1 + DEC_SEQ, :] = glu
        acc = jnp.zeros((DEC_SEQ, CONV_CH), F32)
        for k in range(CONV_WIDTH):
            acc = acc + xin_ref[k:k + DEC_SEQ, :] * cw_ref[k:k + 1, :]
        y = acc + cb_ref[...]
        mu = jnp.mean(y, axis=-1, keepdims=True)
        d = y - mu
        var = jnp.mean(d * d, axis=-1, keepdims=True)
        yn = d * lax.rsqrt(var + EPS) * cg_ref[...] + cbeta_ref[...]
        conv_o_ref[rows, :] = yn * jax.nn.sigmoid(yn)
        ncs_ref[s] = xin_ref[DEC_SEQ:DEC_SEQ + CONV_WIDTH - 1, :]

        q = p[:, C_RQ:C_RK]
        k = p[:, C_RK:C_RV]
        q = q * cos + _swap_halves(q, first_half) * sin
        k = (k * cos + _swap_halves(k, first_half) * sin) * (RET_DK ** -0.5)
        rg = p[:, C_RG:C_SQ]
        kpad_ref[0:DEC_SEQ, :] = k
        vpad_ref[0:DEC_SEQ, :] = p[:, C_RV:C_RG]
        for h in range(RET_HEADS):
            qh = q[:, h * RET_DK:(h + 1) * RET_DK]
            kp = kpad_ref[:, h * RET_DK:(h + 1) * RET_DK]
            vp = vpad_ref[:, h * RET_DV:(h + 1) * RET_DV]
            s_old = rs_ref[0, s, h]
            inner = lax.dot_general(qh, kp, NT_DIMS, preferred_element_type=F32) * dmat_ref[h]
            o = (jnp.dot(inner, vp, preferred_element_type=F32)
                 + dq_ref[h] * jnp.dot(qh, s_old, preferred_element_type=F32))
            kdec_t = lax.dot_general(eye_ref[...], kp * dk_ref[h], NT_DIMS,
                                     preferred_element_type=F32)
            nrs_ref[s, h] = sgam_ref[h] * s_old + jnp.dot(kdec_t, vp, preferred_element_type=F32)
            mu = jnp.mean(o, axis=-1, keepdims=True)
            d = o - mu
            var = jnp.mean(d * d, axis=-1, keepdims=True)
            hs = slice(h * RET_DV, (h + 1) * RET_DV)
            yr = d * lax.rsqrt(var + EPS) * gng_ref[:, hs] + gnb_ref[:, hs]
            gate = rg[:, hs]
            ret_o_ref[rows, hs] = yr * (gate * jax.nn.sigmoid(gate))

        sq = p[:, C_SQ:C_SK]
        sk = p[:, C_SK:C_SV]
        qss = jnp.dot(sq * sq, ones_q, preferred_element_type=F32)
        kss = jnp.dot(sk * sk, ones_k, preferred_element_type=F32)
        qa = sq * lax.rsqrt(qss * (1.0 / SWA_HD) + EPS) * qn_ref[...] * (SWA_HD ** -0.5)
        kn = sk * lax.rsqrt(kss * (1.0 / SWA_HD) + EPS) * kn_ref[...]
        vn = p[:, C_SV:PROJ_COLS]
        ksm_ref[0:N_META, :] = meta_ref[0, s, :, 0:128]
        vsm_ref[0:N_META, :] = meta_ref[0, s, :, 128:256]
        ksm_ref[N_META:S_SMALL, :] = kn
        vsm_ref[N_META:S_SMALL, :] = vn
        for kv in range(SWA_KV_HEADS):
            ksl = slice(kv * SWA_HD, (kv + 1) * SWA_HD)
            vsl = slice(128 + kv * SWA_HD, 128 + (kv + 1) * SWA_HD)
            qs = jnp.concatenate([qa[:, (kv * SWA_GROUP + g) * SWA_HD:(kv * SWA_GROUP + g + 1) * SWA_HD]
                                  for g in range(SWA_GROUP)], axis=0)
            s_w = lax.dot_general(qs, win_ref[0, s, :, ksl], NT_DIMS, preferred_element_type=F32) + bwin_ref[kv]
            s_s = lax.dot_general(qs, ksm_ref[:, ksl], NT_DIMS, preferred_element_type=F32) + bsm_ref[kv]
            sink = sink_ref[kv, :, 0:1]
            m = jnp.maximum(jnp.maximum(jnp.max(s_w, axis=-1, keepdims=True),
                                        jnp.max(s_s, axis=-1, keepdims=True)), sink)
            p_w = jnp.exp(s_w - m)
            p_s = jnp.exp(s_s - m)
            den = (jnp.sum(p_w, axis=-1, keepdims=True) + jnp.sum(p_s, axis=-1, keepdims=True)
                   + jnp.exp(sink - m))
            o = (jnp.dot(p_w, win_ref[0, s, :, vsl], preferred_element_type=F32)
                 + jnp.dot(p_s, vsm_ref[:, ksl], preferred_element_type=F32)) / den
            for g in range(SWA_GROUP):
                hh = kv * SWA_GROUP + g
                swa_o_ref[rows, hh * SWA_HD:(hh + 1) * SWA_HD] = o[g * DEC_SEQ:(g + 1) * DEC_SEQ, :]
        nwin_ref[s, 0:WINDOW - DEC_SEQ, :] = win_ref[0, s, DEC_SEQ:WINDOW, :]
        nwin_ref[s, WINDOW - DEC_SEQ:WINDOW, 0:128] = kn
        nwin_ref[s, WINDOW - DEC_SEQ:WINDOW, 128:256] = vn


def _sample_tables(rel_bias):
    lg = jnp.log(1.0 - 2.0 ** (-5.0 - jnp.arange(RET_HEADS, dtype=F32)))
    i = jnp.arange(DEC_SEQ, dtype=F32)
    diff = i[:, None] - i[None, :]
    dm = jnp.where(diff >= 0, jnp.exp(jnp.maximum(diff, 0.0)[None] * lg[:, None, None]), 0.0)
    dmat = jnp.zeros((RET_HEADS, DEC_SEQ, 128), F32).at[:, :, :DEC_SEQ].set(dm)
    dq = jnp.broadcast_to(jnp.exp((i[None] + 1.0) * lg[:, None])[:, :, None], (RET_HEADS, DEC_SEQ, RET_DV))
    kd = jnp.exp((DEC_SEQ - 1.0 - i)[None] * lg[:, None])
    dk = jnp.zeros((RET_HEADS, 128, RET_DK), F32).at[:, :DEC_SEQ, :].set(
        jnp.broadcast_to(kd[:, :, None], (RET_HEADS, DEC_SEQ, RET_DK)))
    sg = jnp.broadcast_to(jnp.exp(DEC_SEQ * lg)[:, None, None], (RET_HEADS, 1, RET_DV))
    cos_t, sin_t = _rotary_tables(PAST_LEN + jnp.arange(DEC_SEQ))

    ti = np.arange(S_STACK)[:, None] % DEC_SEQ
    j = np.arange(WINDOW)[None, :]
    bk_win = _t5_bucket_np(np.maximum(ti + WINDOW - j, 0))
    ok_win = j > ti
    c = np.arange(128)[None, :]
    jn = c - N_META
    bk_new = _t5_bucket_np(np.clip(ti - jn, 0, None))
    ok_sm = (c < N_META) | ((c < S_SMALL) & (jn <= ti))
    bk_sm = np.where(c < N_META, N_BUCKETS - 1, bk_new)
    rb = rel_bias.astype(F32)
    head = (np.arange(SWA_KV_HEADS)[:, None, None] * SWA_GROUP + (np.arange(S_STACK) // DEC_SEQ)[None, :, None])

    def look(bk, ok):
        b = rb[jnp.asarray(np.broadcast_to(bk[None], (SWA_KV_HEADS,) + bk.shape)),
               jnp.asarray(np.broadcast_to(head, (SWA_KV_HEADS,) + bk.shape))]
        return jnp.where(jnp.asarray(np.broadcast_to(ok[None], b.shape)), b, NEG)

    return dict(dmat=dmat, dq=dq, dk=dk, sg=sg, cos=cos_t, sin=sin_t,
                bwin=look(bk_win, ok_win), bsm=look(bk_sm, ok_sm), head=head)


def _sample_mixers(p, l, cache_meta_kv, cache_swa_kv, state_ret, state_conv, tabs, conv_w32, conv_b,
                   conv_g, conv_beta, gn_g, gn_b, q_norm, k_norm, sinks):
    row0 = NP_ROWS // (SG * DEC_SEQ)
    nrow = SG * DEC_SEQ
    ones_bd = jnp.asarray(np.kron(np.eye(SWA_HEADS), np.ones((SWA_HD, SWA_HD))), F32)
    qn = jnp.tile(q_norm.astype(F32), SWA_HEADS).reshape(1, -1)
    kn = jnp.tile(k_norm.astype(F32), SWA_KV_HEADS).reshape(1, -1)
    sink_t = jnp.broadcast_to(sinks.astype(F32)[jnp.asarray(tabs["head"])], (SWA_KV_HEADS, S_STACK, 128))
    meta = cache_meta_kv.reshape(DEPTH, DEC_BATCH, N_META, 256)
    win = cache_swa_kv.reshape(DEPTH, DEC_BATCH, WINDOW, 256)

    def const(shape):
        return pl.BlockSpec(shape, lambda i: (0,) * len(shape))

    return pl.pallas_call(
        _sample_kernel,
        out_shape=(jax.ShapeDtypeStruct((NS_ROWS, CONV_CH), F32),
                   jax.ShapeDtypeStruct((NS_ROWS, RET_HEADS * RET_DV), F32),
                   jax.ShapeDtypeStruct((NS_ROWS, SWA_HEADS * SWA_HD), F32),
                   jax.ShapeDtypeStruct((DEC_BATCH, CONV_WIDTH - 1, CONV_CH), F32),
                   jax.ShapeDtypeStruct((DEC_BATCH, RET_HEADS, RET_DK, RET_DV), F32),
                   jax.ShapeDtypeStruct((DEC_BATCH, WINDOW, 256), F32)),
        grid=(DEC_BATCH // SG,),
        in_specs=[pl.BlockSpec((nrow, PROJ_COLS), lambda i: (row0 + i, 0)),
                  pl.BlockSpec((1, SG, CONV_WIDTH - 1, CONV_CH), lambda i: (l, i, 0, 0)),
                  pl.BlockSpec((1, SG, RET_HEADS, RET_DK, RET_DV), lambda i: (l, i, 0, 0, 0)),
                  pl.BlockSpec((1, SG, N_META, 256), lambda i: (l, i, 0, 0)),
                  pl.BlockSpec((1, SG, WINDOW, 256), lambda i: (l, i, 0, 0)),
                  const((32, CONV_CH)), const((1, CONV_CH)), const((1, CONV_CH)), const((1, CONV_CH)),
                  const((DEC_SEQ, 256)), const((DEC_SEQ, 256)),
                  const((RET_HEADS, DEC_SEQ, 128)), const((RET_HEADS, DEC_SEQ, RET_DV)),
                  const((RET_HEADS, 128, RET_DK)), const((RET_HEADS, 1, RET_DV)),
                  const((1, 512)), const((1, 512)), const((RET_DK, RET_DK)),
                  const((512, 512)), const((1, 512)), const((1, 128)),
                  const((SWA_KV_HEADS, S_STACK, 128)), const((SWA_KV_HEADS, S_STACK, 128)),
                  const((SWA_KV_HEADS, S_STACK, 128))],
        out_specs=(pl.BlockSpec((nrow, CONV_CH), lambda i: (i, 0)),
                   pl.BlockSpec((nrow, 512), lambda i: (i, 0)),
                   pl.BlockSpec((nrow, 512), lambda i: (i, 0)),
                   pl.BlockSpec((SG, CONV_WIDTH - 1, CONV_CH), lambda i: (i, 0, 0)),
                   pl.BlockSpec((SG, RET_HEADS, RET_DK, RET_DV), lambda i: (i, 0, 0, 0)),
                   pl.BlockSpec((SG, WINDOW, 256), lambda i: (i, 0, 0))),
        scratch_shapes=[pltpu.VMEM((40, CONV_CH), F32), pltpu.VMEM((128, RET_HEADS * RET_DK), F32),
                        pltpu.VMEM((128, RET_HEADS * RET_DV), F32),
                        pltpu.VMEM((128, 128), F32), pltpu.VMEM((128, 128), F32)],
        compiler_params=_cparams(("parallel",)),
        name="sample_mixers",
    )(p, state_conv, state_ret, meta, win,
      conv_w32, conv_b.reshape(1, -1), conv_g.reshape(1, -1), conv_beta.reshape(1, -1),
      tabs["cos"], tabs["sin"], tabs["dmat"], tabs["dq"], tabs["dk"], tabs["sg"],
      gn_g.reshape(1, -1), gn_b.reshape(1, -1), jnp.eye(RET_DK, dtype=F32),
      ones_bd, qn, kn, tabs["bwin"], tabs["bsm"], sink_t)


def kernel(x_prompt, x_sample, cache_meta_kv, cache_swa_kv, state_ret, state_conv, meta_tokens, rel_bias,
           norm_mix, w_in, conv_w, conv_b, conv_ln_g, conv_ln_b, ret_gn_g, ret_gn_b, swa_q_norm,
           swa_k_norm, swa_sinks, w_out, norm_ffn, peer_wq, peer_keys, peer_u, peer_v):
    meta = jnp.broadcast_to(meta_tokens.astype(F32)[None], (BATCH, N_META, D_MODEL))
    pad = jnp.zeros((BATCH, LP - L_REAL, D_MODEL), F32)
    hp = jnp.concatenate([meta, x_prompt, pad], axis=1).reshape(NP_ROWS, D_MODEL)
    h = jnp.concatenate([hp, x_sample.reshape(NS_ROWS, D_MODEL)], axis=0)

    cos_t, sin_t = _rotary_tables(jnp.arange(LP))
    ret_tabs = _ret_tables()
    bias_tabs = _bias_tables(rel_bias)
    sample_tabs = _sample_tables(rel_bias)

    meta_p, win_p, ret_p, conv_p, win_s, ret_s, conv_s = [], [], [], [], [], [], []
    for l in range(DEPTH):
        p = _norm_proj(h, norm_mix[l], w_in[l].astype(BF16))

        w32 = jnp.concatenate([conv_w[l], jnp.zeros((1, CONV_CH), F32)], axis=0)
        conv_o, conv_tail = _conv_prompt(p, w32, conv_b[l], conv_ln_g[l], conv_ln_b[l])
        ret_o, ret_state = _ret_prompt(p, cos_t, sin_t, ret_tabs, ret_gn_g[l], ret_gn_b[l])
        swa_o, k_normed = _swa_prompt(p, swa_sinks[l], swa_q_norm[l], swa_k_norm[l], bias_tabs)
        s_conv, s_ret, s_swa, new_conv, new_ret, new_win = _sample_mixers(
            p, l, cache_meta_kv, cache_swa_kv, state_ret, state_conv, sample_tabs, w32, conv_b[l],
            conv_ln_g[l], conv_ln_b[l], ret_gn_g[l], ret_gn_b[l], swa_q_norm[l], swa_k_norm[l], swa_sinks[l])
        new_win = new_win.reshape(DEC_BATCH, WINDOW, 2, SWA_KV_HEADS, SWA_HD)

        kbd = jnp.zeros((2 * N_KEYS, 2 * PEER_HALF), F32)
        kbd = kbd.at[:N_KEYS, :PEER_HALF].set(peer_keys[l, 0]).at[N_KEYS:, PEER_HALF:].set(peer_keys[l, 1])
        h1, xn, st = _out_ffn(h, (conv_o, ret_o, swa_o), (s_conv, s_ret, s_swa), w_out[l].astype(BF16),
                              norm_ffn[l], peer_wq[l].astype(BF16), kbd.astype(BF16))
        rho, c1, r2, e2 = _topk(st)
        h = _peer(xn, h1, peer_u[l].astype(BF16), peer_v[l].T.astype(BF16), rho, c1, r2, e2)

        kp = k_normed.reshape(BATCH, LP, SWA_KV_HEADS, SWA_HD)
        vp = p[:NP_ROWS, C_SV:].reshape(BATCH, LP, SWA_KV_HEADS, SWA_HD)
        kvp = jnp.stack([kp, vp], axis=2)
        meta_p.append(kvp[:, :N_META])
        win_p.append(kvp[:, L_REAL - WINDOW:L_REAL])
        ret_p.append(ret_state)
        conv_p.append(conv_tail[:, 32 - (CONV_WIDTH - 1):])
        win_s.append(new_win)
        ret_s.append(new_ret)
        conv_s.append(new_conv)

    y_prompt = h[:NP_ROWS].reshape(BATCH, LP, D_MODEL)[:, N_META:L_REAL]
    y_sample = h[NP_ROWS:].reshape(DEC_BATCH, DEC_SEQ, D_MODEL)
    return (y_prompt, y_sample, jnp.stack(meta_p), jnp.stack(win_p), jnp.stack(ret_p), jnp.stack(conv_p),
            jnp.stack(win_s), jnp.stack(ret_s), jnp.stack(conv_s))
```

```python
import functools
import math

import numpy as np
import jax
import jax.numpy as jnp
from jax import lax
from jax.experimental import pallas as pl
from jax.experimental.pallas import tpu as pltpu

F32 = jnp.float32
BF16 = jnp.bfloat16

D_MODEL = 1024
BATCH = 2
SEQ = 8192
DEPTH = 2
DEC_BATCH = 128
DEC_SEQ = 8
PAST_LEN = 8192
N_META = 16
CONV_CH = 512
CONV_WIDTH = 31
RET_HEADS = 4
RET_DK = 64
RET_DV = 128
SWA_HEADS = 8
SWA_KV_HEADS = 2
SWA_GROUP = SWA_HEADS // SWA_KV_HEADS
SWA_HD = 64
WINDOW = 128
N_BUCKETS = 32
REL_MAX_DIST = 128
PEER_HEADS = 8
N_KEYS = 128
N_EXPERTS = N_KEYS * N_KEYS
PEER_TOPK = 16
PEER_HALF = 64
EPS = 1e-6
NEG = -1e30

PROJ_COLS = 3328
C_CONV, C_RQ, C_RK, C_RV, C_RG, C_SQ, C_SK, C_SV = 0, 1024, 1280, 1536, 2048, 2560, 3072, 3200

L_REAL = N_META + SEQ
BLK = 128
N_BLK = 65
LP = N_BLK * BLK
NP_ROWS = BATCH * LP
NS_ROWS = DEC_BATCH * DEC_SEQ
N_ROWS = NP_ROWS + NS_ROWS
LAST_REAL = L_REAL - (N_BLK - 1) * BLK

TM = 384
TO = 256
TT = 768
SG = 8
EC = 1024
PEER_LANES = 256
CONV_T = 640
CONV_RB = 64
VMEM_LIMIT = 56 * 1024 * 1024


def _cparams(sem, flags=None):
    return pltpu.CompilerParams(dimension_semantics=sem, vmem_limit_bytes=VMEM_LIMIT, flags=flags)


def _norm_proj_kernel(x_ref, g_ref, w_ref, o_ref):
    x = x_ref[...]
    ms = jnp.mean(x * x, axis=-1, keepdims=True)
    xn = x * lax.rsqrt(ms + EPS) * g_ref[...]
    o_ref[...] = jnp.dot(xn.astype(BF16), w_ref[...], preferred_element_type=F32)


def _norm_proj(h, g, w_bf):
    return pl.pallas_call(
        _norm_proj_kernel,
        out_shape=jax.ShapeDtypeStruct((N_ROWS, PROJ_COLS), F32),
        grid=(N_ROWS // TM,),
        in_specs=[pl.BlockSpec((TM, D_MODEL), lambda i: (i, 0)),
                  pl.BlockSpec((1, D_MODEL), lambda i: (0, 0)),
                  pl.BlockSpec((D_MODEL, PROJ_COLS), lambda i: (0, 0))],
        out_specs=pl.BlockSpec((TM, PROJ_COLS), lambda i: (i, 0)),
        compiler_params=_cparams(("parallel",)),
        name="norm_proj",
    )(h, g.reshape(1, D_MODEL), w_bf)


def _conv_kernel(c_ref, w_ref, b_ref, g_ref, beta_ref, o_ref, st_ref, xin_ref):
    t = pl.program_id(1)

    @pl.when(t == 0)
    def _():
        xin_ref[0:32, :] = jnp.zeros((32, CONV_CH), F32)

    @pl.when(t > 0)
    def _():
        xin_ref[0:32, :] = xin_ref[CONV_T:CONV_T + 32, :]

    c = c_ref[...]
    xin_ref[32:32 + CONV_T, :] = c[:, :CONV_CH] * jax.nn.sigmoid(c[:, CONV_CH:])

    w = w_ref[...]
    bias = b_ref[...]
    gam = g_ref[...]
    beta = beta_ref[...]
    for rb in range(CONV_T // CONV_RB):
        r0 = rb * CONV_RB
        acc = jnp.zeros((CONV_RB, CONV_CH), F32)
        for k in range(CONV_WIDTH):
            acc = acc + xin_ref[r0 + 2 + k:r0 + 2 + k + CONV_RB, :] * w[k:k + 1, :]
        y = acc + bias
        mu = jnp.mean(y, axis=-1, keepdims=True)
        d = y - mu
        var = jnp.mean(d * d, axis=-1, keepdims=True)
        yn = d * lax.rsqrt(var + EPS) * gam + beta
        o_ref[r0:r0 + CONV_RB, :] = yn * jax.nn.sigmoid(yn)

    @pl.when(t == pl.num_programs(1) - 1)
    def _():
        lo = 32 + (L_REAL - 32) - (LP - CONV_T)
        st_ref[0] = xin_ref[lo:lo + 32, :]


def _conv_prompt(p, w32, b, g, beta):
    nt = LP // CONV_T
    return pl.pallas_call(
        _conv_kernel,
        out_shape=(jax.ShapeDtypeStruct((NP_ROWS, CONV_CH), F32),
                   jax.ShapeDtypeStruct((BATCH, 32, CONV_CH), F32)),
        grid=(BATCH, nt),
        in_specs=[pl.BlockSpec((CONV_T, 2 * CONV_CH), lambda bi, t: (bi * nt + t, 0)),
                  pl.BlockSpec((32, CONV_CH), lambda bi, t: (0, 0)),
                  pl.BlockSpec((1, CONV_CH), lambda bi, t: (0, 0)),
                  pl.BlockSpec((1, CONV_CH), lambda bi, t: (0, 0)),
                  pl.BlockSpec((1, CONV_CH), lambda bi, t: (0, 0))],
        out_specs=(pl.BlockSpec((CONV_T, CONV_CH), lambda bi, t: (bi * nt + t, 0)),
                   pl.BlockSpec((1, 32, CONV_CH), lambda bi, t: (bi, 0, 0))),
        scratch_shapes=[pltpu.VMEM((32 + CONV_T + 32, CONV_CH), F32)],
        compiler_params=_cparams(("arbitrary", "arbitrary")),
        name="conv_prompt",
    )(p, w32, b.reshape(1, -1), g.reshape(1, -1), beta.reshape(1, -1))


def _swap_halves(x, first_half):
    return jnp.where(first_half, pltpu.roll(x, x.shape[1] - 32, axis=1), pltpu.roll(x, 32, axis=1))


def _ret_kernel(q_ref, k_ref, v_ref, rg_ref, cos_ref, sin_ref, dmat_ref, dq_ref, dk_ref, sg_ref,
                gng_ref, gnb_ref, o_ref, st_ref, s_ref):
    j = pl.program_id(1)

    @pl.when(j == 0)
    def _():
        s_ref[...] = jnp.zeros_like(s_ref)

    cos = cos_ref[...]
    sin = sin_ref[...]
    lane = lax.broadcasted_iota(jnp.int32, (BLK, RET_HEADS * RET_DK), 1)
    first_half = (lane % RET_DK) < (RET_DK // 2)
    q = q_ref[...]
    k = k_ref[...]
    q = q * cos + _swap_halves(q, first_half) * sin
    k = (k * cos + _swap_halves(k, first_half) * sin) * (RET_DK ** -0.5)
    v = v_ref[...]
    rg = rg_ref[...]
    for h in range(RET_HEADS):
        qh = q[:, h * RET_DK:(h + 1) * RET_DK].astype(BF16)
        kh = k[:, h * RET_DK:(h + 1) * RET_DK]
        vh = v[:, h * RET_DV:(h + 1) * RET_DV].astype(BF16)
        s_old = s_ref[h]
        inner = lax.dot_general(qh, kh.astype(BF16), (((1,), (1,)), ((), ())),
                                preferred_element_type=F32) * dmat_ref[h]
        o = (jnp.dot(inner.astype(BF16), vh, preferred_element_type=F32)
             + dq_ref[h] * jnp.dot(qh, s_old.astype(BF16), preferred_element_type=F32))
        kdec_t = (kh * dk_ref[0, h]).T.astype(BF16)
        s_new = sg_ref[0, h, 0:1, :] * s_old + jnp.dot(kdec_t, vh, preferred_element_type=F32)
        s_ref[h] = s_new
        mu = jnp.mean(o, axis=-1, keepdims=True)
        d = o - mu
        var = jnp.mean(d * d, axis=-1, keepdims=True)
        y = d * lax.rsqrt(var + EPS) * gng_ref[:, h * RET_DV:(h + 1) * RET_DV] \
            + gnb_ref[:, h * RET_DV:(h + 1) * RET_DV]
        gate = rg[:, h * RET_DV:(h + 1) * RET_DV]
        o_ref[:, h * RET_DV:(h + 1) * RET_DV] = y * (gate * jax.nn.sigmoid(gate))

    @pl.when(j == pl.num_programs(1) - 1)
    def _():
        st_ref[0] = s_ref[...]


def _ret_tables():
    lg = jnp.log(1.0 - 2.0 ** (-5.0 - jnp.arange(RET_HEADS, dtype=F32)))
    i = jnp.arange(BLK, dtype=F32)
    diff = i[:, None] - i[None, :]
    dmat = jnp.where(diff >= 0, jnp.exp(jnp.maximum(diff, 0.0)[None] * lg[:, None, None]), 0.0)
    dq = jnp.broadcast_to(jnp.exp((i[None] + 1.0) * lg[:, None])[:, :, None], (RET_HEADS, BLK, RET_DV))

    def kdec(c_eff):
        e = jnp.where(i[None] < c_eff, jnp.exp((c_eff - 1.0 - i)[None] * lg[:, None]), 0.0)
        return jnp.broadcast_to(e[:, :, None], (RET_HEADS, BLK, RET_DK))

    def sgam(c_eff):
        return jnp.broadcast_to(jnp.exp(c_eff * lg)[:, None, None], (RET_HEADS, 8, RET_DV))

    dk = jnp.stack([kdec(float(BLK)), kdec(float(LAST_REAL))])
    sg = jnp.stack([sgam(float(BLK)), sgam(float(LAST_REAL))])
    return dmat, dq, dk, sg


def _rotary_tables(pos):
    half = RET_DK // 2
    inv = 1.0 / (10000.0 ** (jnp.arange(half, dtype=F32) / half))
    ang = pos.astype(F32)[:, None] * inv[None]
    cos, sin = jnp.cos(ang), jnp.sin(ang)
    cos_t = jnp.tile(jnp.concatenate([cos, cos], axis=-1), (1, RET_HEADS))
    sin_t = jnp.tile(jnp.concatenate([-sin, sin], axis=-1), (1, RET_HEADS))
    return cos_t, sin_t


def _ret_prompt(p, cos_t, sin_t, tabs, gn_g, gn_b):
    dmat, dq, dk, sg = tabs
    nb = N_BLK
    last = nb - 1
    return pl.pallas_call(
        _ret_kernel,
        out_shape=(jax.ShapeDtypeStruct((NP_ROWS, RET_HEADS * RET_DV), F32),
                   jax.ShapeDtypeStruct((BATCH, RET_HEADS, RET_DK, RET_DV), F32)),
        grid=(BATCH, nb),
        in_specs=[pl.BlockSpec((BLK, 256), lambda b, j: (b * nb + j, C_RQ // 256)),
                  pl.BlockSpec((BLK, 256), lambda b, j: (b * nb + j, C_RK // 256)),
                  pl.BlockSpec((BLK, 512), lambda b, j: (b * nb + j, C_RV // 512)),
                  pl.BlockSpec((BLK, 512), lambda b, j: (b * nb + j, C_RG // 512)),
                  pl.BlockSpec((BLK, 256), lambda b, j: (j, 0)),
                  pl.BlockSpec((BLK, 256), lambda b, j: (j, 0)),
                  pl.BlockSpec((RET_HEADS, BLK, BLK), lambda b, j: (0, 0, 0)),
                  pl.BlockSpec((RET_HEADS, BLK, RET_DV), lambda b, j: (0, 0, 0)),
                  pl.BlockSpec((1, RET_HEADS, BLK, RET_DK), lambda b, j: (j // last, 0, 0, 0)),
                  pl.BlockSpec((1, RET_HEADS, 8, RET_DV), lambda b, j: (j // last, 0, 0, 0)),
                  pl.BlockSpec((1, 512), lambda b, j: (0, 0)),
                  pl.BlockSpec((1, 512), lambda b, j: (0, 0))],
        out_specs=(pl.BlockSpec((BLK, 512), lambda b, j: (b * nb + j, 0)),
                   pl.BlockSpec((1, RET_HEADS, RET_DK, RET_DV), lambda b, j: (b, 0, 0, 0))),
        scratch_shapes=[pltpu.VMEM((RET_HEADS, RET_DK, RET_DV), F32)],
        compiler_params=_cparams(("arbitrary", "arbitrary")),
        name="ret_prompt",
    )(p, p, p, p, cos_t, sin_t, dmat, dq, dk, sg, gn_g.reshape(1, -1), gn_b.reshape(1, -1))


def _group_rms(x, ones_bd, w):
    x2 = x * x
    hi = x2.astype(BF16)
    lo = (x2 - hi.astype(F32)).astype(BF16)
    ss = (jnp.dot(hi, ones_bd, preferred_element_type=F32)
          + jnp.dot(lo, ones_bd, preferred_element_type=F32))
    return x * lax.rsqrt(ss * (1.0 / SWA_HD) + EPS) * w


def _swa_kernel(sinks_ref, q_ref, kc_ref, vc_ref, kp_ref, vp_ref, km_ref, vm_ref, ones_ref,
                qn_ref, kn_ref, bc_ref, bp_ref, bm_ref, o_ref, kout_ref):
    j = pl.program_id(1)
    ones_q = ones_ref[...]
    ones_k = ones_ref[0:128, 0:128]
    qw = qn_ref[...]
    kw = kn_ref[...]
    q = _group_rms(q_ref[...], ones_q, qw) * (SWA_HD ** -0.5)
    kc = _group_rms(kc_ref[...], ones_k, kw)
    kp = _group_rms(kp_ref[...], ones_k, kw)
    km = _group_rms(km_ref[...], ones_k, kw)
    kout_ref[...] = kc
    vc = vc_ref[...].astype(BF16)
    vp = vp_ref[...].astype(BF16)
    vm = vm_ref[...].astype(BF16)

    rows = SWA_GROUP * BLK
    qi = lax.broadcasted_iota(jnp.int32, (rows, BLK), 0) % BLK
    kj = lax.broadcasted_iota(jnp.int32, (rows, BLK), 1)
    valid_c = jnp.where(kj <= qi, j * BLK + kj, -1) >= N_META
    valid_p = jnp.where(kj > qi, (j - 1) * BLK + kj, -1) >= N_META
    qm = lax.broadcasted_iota(jnp.int32, (rows, N_META), 0) % BLK
    mm = lax.broadcasted_iota(jnp.int32, (rows, N_META), 1)
    valid_m = mm <= j * BLK + qm

    dn = (((1,), (1,)), ((), ()))
    for kv in range(SWA_KV_HEADS):
        heads = range(kv * SWA_GROUP, (kv + 1) * SWA_GROUP)
        qs = jnp.concatenate([q[:, h * SWA_HD:(h + 1) * SWA_HD] for h in heads], axis=0).astype(BF16)
        ksl = slice(kv * SWA_HD, (kv + 1) * SWA_HD)
        s_c = lax.dot_general(qs, kc[:, ksl].astype(BF16), dn, preferred_element_type=F32)
        s_p = lax.dot_general(qs, kp[:, ksl].astype(BF16), dn, preferred_element_type=F32)
        s_m = lax.dot_general(qs, km[:, ksl].astype(BF16), dn, preferred_element_type=F32)
        s_c = jnp.where(valid_c, s_c + bc_ref[kv], NEG)
        s_p = jnp.where(valid_p, s_p + bp_ref[kv], NEG)
        s_m = jnp.where(valid_m, s_m + bm_ref[0, kv], NEG)
        sink = jnp.concatenate([jnp.full((BLK, 1), sinks_ref[h], F32) for h in heads], axis=0)
        m = jnp.maximum(jnp.maximum(jnp.max(s_c, axis=-1, keepdims=True),
                                    jnp.max(s_p, axis=-1, keepdims=True)),
                        jnp.maximum(jnp.max(s_m, axis=-1, keepdims=True), sink))
        p_c = jnp.exp(s_c - m)
        p_p = jnp.exp(s_p - m)
        p_m = jnp.exp(s_m - m)
        den = (jnp.sum(p_c, axis=-1, keepdims=True) + jnp.sum(p_p, axis=-1, keepdims=True)
               + jnp.sum(p_m, axis=-1, keepdims=True) + jnp.exp(sink - m))
        acc = (jnp.dot(p_c.astype(BF16), vc[:, ksl], preferred_element_type=F32)
               + jnp.dot(p_p.astype(BF16), vp[:, ksl], preferred_element_type=F32)
               + jnp.dot(p_m.astype(BF16), vm[:, ksl], preferred_element_type=F32))
        out = acc / den
        for g, h in enumerate(heads):
            o_ref[:, h * SWA_HD:(h + 1) * SWA_HD] = out[g * BLK:(g + 1) * BLK, :]


def _one_hot(idx, n):
    return jnp.asarray(np.asarray(idx)[..., None] == np.arange(n), F32)


def _t5_bucket_np(dist):
    max_exact = N_BUCKETS // 2
    df = np.maximum(dist, 1).astype(np.float64)
    large = max_exact + (np.log(df / max_exact) / math.log(REL_MAX_DIST / max_exact)
                         * (N_BUCKETS - max_exact)).astype(np.int64)
    large = np.minimum(large, N_BUCKETS - 1)
    return np.where(dist < max_exact, dist, large).astype(np.int32)


def _bias_tables(rel_bias):
    qi = np.arange(BLK)[:, None]
    kj = np.arange(BLK)[None, :]
    b_cur = _t5_bucket_np(np.maximum(qi - kj, 0))
    b_prev = _t5_bucket_np(np.maximum(qi - kj + BLK, 0))
    m = np.arange(N_META)[None, :]
    b_m0 = _t5_bucket_np(np.maximum(qi - m, 0))
    b_m1 = _t5_bucket_np(np.maximum(qi + BLK - m, 0))
    assert (b_m1 == N_BUCKETS - 1).all()
    rb = rel_bias.astype(F32)

    def look(bk):
        return jnp.einsum("...b,bh->h...", _one_hot(bk, N_BUCKETS), rb, precision=lax.Precision.HIGHEST)

    return look(b_cur), look(b_prev), jnp.stack([look(b_m0), look(b_m1)])


def _swa_prompt(p, sinks, q_norm, k_norm, bias_tabs):
    srows = SWA_GROUP * BLK
    b_cur = bias_tabs[0].reshape(SWA_KV_HEADS, srows, BLK)
    b_prev = bias_tabs[1].reshape(SWA_KV_HEADS, srows, BLK)
    b_meta = bias_tabs[2].reshape(2, SWA_KV_HEADS, srows, N_META)
    nb = N_BLK
    ones_bd = jnp.asarray(np.kron(np.eye(SWA_HEADS), np.ones((SWA_HD, SWA_HD))), BF16)
    qn = jnp.tile(q_norm.astype(F32), SWA_HEADS).reshape(1, -1)
    kn = jnp.tile(k_norm.astype(F32), SWA_KV_HEADS).reshape(1, -1)
    ck, cv = C_SK // 128, C_SV // 128
    return pl.pallas_call(
        _swa_kernel,
        out_shape=(jax.ShapeDtypeStruct((NP_ROWS, SWA_HEADS * SWA_HD), F32),
                   jax.ShapeDtypeStruct((NP_ROWS, SWA_KV_HEADS * SWA_HD), F32)),
        grid=(BATCH, nb),
        in_specs=[pl.BlockSpec(memory_space=pltpu.SMEM),
                  pl.BlockSpec((BLK, 512), lambda b, j: (b * nb + j, C_SQ // 512)),
                  pl.BlockSpec((BLK, 128), lambda b, j: (b * nb + j, ck)),
                  pl.BlockSpec((BLK, 128), lambda b, j: (b * nb + j, cv)),
                  pl.BlockSpec((BLK, 128), lambda b, j: (b * nb + jnp.maximum(j - 1, 0), ck)),
                  pl.BlockSpec((BLK, 128), lambda b, j: (b * nb + jnp.maximum(j - 1, 0), cv)),
                  pl.BlockSpec((N_META, 128), lambda b, j: (b * (LP // N_META), ck)),
                  pl.BlockSpec((N_META, 128), lambda b, j: (b * (LP // N_META), cv)),
                  pl.BlockSpec((512, 512), lambda b, j: (0, 0)),
                  pl.BlockSpec((1, 512), lambda b, j: (0, 0)),
                  pl.BlockSpec((1, 128), lambda b, j: (0, 0)),
                  pl.BlockSpec((SWA_KV_HEADS, srows, BLK), lambda b, j: (0, 0, 0)),
                  pl.BlockSpec((SWA_KV_HEADS, srows, BLK), lambda b, j: (0, 0, 0)),
                  pl.BlockSpec((1, SWA_KV_HEADS, srows, N_META), lambda b, j: (jnp.minimum(j, 1), 0, 0, 0))],
        out_specs=(pl.BlockSpec((BLK, 512), lambda b, j: (b * nb + j, 0)),
                   pl.BlockSpec((BLK, 128), lambda b, j: (b * nb + j, 0))),
        compiler_params=_cparams(("parallel", "parallel")),
        name="swa_prompt",
    )(sinks.astype(F32), p, p, p, p, p, p, p, ones_bd, qn, kn, b_cur, b_prev, b_meta)


def _out_ffn_kernel(h_ref, a_ref, b_ref, c_ref, as_ref, bs_ref, cs_ref, wo_ref, g_ref, wq_ref, kbd_ref,
                    h1_ref, xn_ref, st_ref):
    is_prompt = pl.program_id(0) < NP_ROWS // TO
    a = jnp.where(is_prompt, a_ref[...], as_ref[...])
    b = jnp.where(is_prompt, b_ref[...], bs_ref[...])
    c = jnp.where(is_prompt, c_ref[...], cs_ref[...])
    h1 = (h_ref[...]
          + jnp.dot(a.astype(BF16), wo_ref[0:512, :], preferred_element_type=F32)
          + jnp.dot(b.astype(BF16), wo_ref[512:1024, :], preferred_element_type=F32)
          + jnp.dot(c.astype(BF16), wo_ref[1024:1536, :], preferred_element_type=F32))
    h1_ref[...] = h1
    ms = jnp.mean(h1 * h1, axis=-1, keepdims=True)
    xn_f = h1 * lax.rsqrt(ms + EPS) * g_ref[...]
    xn_ref[...] = xn_f.T.astype(BF16)
    q = jnp.dot(xn_f.astype(BF16), wq_ref[...], preferred_element_type=F32).astype(BF16)
    kbd = kbd_ref[...]
    for hh in range(PEER_HEADS):
        st_ref[hh] = lax.dot_general(kbd, q[:, hh * 128:(hh + 1) * 128], (((1,), (1,)), ((), ())),
                                     preferred_element_type=F32)


def _out_ffn(h, mix_p, mix_s, wo_bf, g, wq_bf, kbd_bf):
    npb = NP_ROWS // TO
    pspec = pl.BlockSpec((TO, 512), lambda i: (jnp.minimum(i, npb - 1), 0))
    sspec = pl.BlockSpec((TO, 512), lambda i: (jnp.maximum(i - npb, 0), 0))
    return pl.pallas_call(
        _out_ffn_kernel,
        out_shape=(jax.ShapeDtypeStruct((N_ROWS, D_MODEL), F32),
                   jax.ShapeDtypeStruct((D_MODEL, N_ROWS), BF16),
                   jax.ShapeDtypeStruct((PEER_HEADS, 2 * N_KEYS, N_ROWS), F32)),
        grid=(N_ROWS // TO,),
        in_specs=[pl.BlockSpec((TO, D_MODEL), lambda i: (i, 0)),
                  pspec, pspec, pspec, sspec, sspec, sspec,
                  pl.BlockSpec((1536, D_MODEL), lambda i: (0, 0)),
                  pl.BlockSpec((1, D_MODEL), lambda i: (0, 0)),
                  pl.BlockSpec((D_MODEL, D_MODEL), lambda i: (0, 0)),
                  pl.BlockSpec((2 * N_KEYS, 128), lambda i: (0, 0))],
        out_specs=(pl.BlockSpec((TO, D_MODEL), lambda i: (i, 0)),
                   pl.BlockSpec((D_MODEL, TO), lambda i: (0, i)),
                   pl.BlockSpec((PEER_HEADS, 2 * N_KEYS, TO), lambda i: (0, 0, i))),
        compiler_params=_cparams(("parallel",)),
        name="out_ffn",
    )(h, *mix_p, *mix_s, wo_bf, g.reshape(1, -1), wq_bf, kbd_bf)


def _top_vals(x, n, with_rank=False):
    vals = []
    rank = jnp.full(x.shape, float(n), F32)
    for r in range(n):
        mx = jnp.max(x, axis=0, keepdims=True)
        vals.append(mx)
        hit = x == mx
        if with_rank:
            rank = jnp.where(hit, float(r), rank)
        x = jnp.where(hit, NEG, x)
    return (vals, rank) if with_rank else vals


def _topk_kernel(st_ref, rho_ref, c1_ref, r2_ref, e2_ref):
    s1 = st_ref[0, 0:N_KEYS, :]
    s2 = st_ref[0, N_KEYS:2 * N_KEYS, :]
    v1 = _top_vals(s1, PEER_TOPK)
    v2, rank2 = _top_vals(s2, PEER_TOPK, with_rank=True)
    sv1 = jnp.concatenate(v1, axis=0)
    sv2 = jnp.concatenate(v2, axis=0)
    cand = jnp.concatenate([v1[0] + sv2, v1[1] + sv2]
                           + [v1[a] + sv2[0:8] for a in range(2, 8)]
                           + [sv1[8:16] + v2[0]], axis=0)
    top = _top_vals(cand, PEER_TOPK)
    tau = top[PEER_TOPK - 1]
    z = jnp.ones_like(tau)
    for r in range(1, PEER_TOPK):
        z = z + jnp.exp(top[r] - top[0])
    rho = jnp.zeros(s1.shape, F32)
    for a in range(PEER_TOPK):
        n_a = jnp.sum(jnp.where(v1[a] + sv2 >= tau, 1.0, 0.0), axis=0, keepdims=True)
        rho = jnp.where(s1 == v1[a], n_a, rho)
    rho_ref[0] = rho
    c1_ref[0] = jnp.exp(s1 - v1[0]) / z
    r2_ref[0] = rank2.astype(BF16)
    e2_ref[0] = jnp.exp(s2 - v2[0]).astype(BF16)


def _topk(st):
    shp = jax.ShapeDtypeStruct((PEER_HEADS, N_KEYS, N_ROWS), F32)
    shp_bf = jax.ShapeDtypeStruct((PEER_HEADS, N_KEYS, N_ROWS), BF16)
    spec = pl.BlockSpec((1, N_KEYS, TM), lambda i, h: (h, 0, i))
    return pl.pallas_call(
        _topk_kernel,
        out_shape=(shp, shp, shp_bf, shp_bf),
        grid=(N_ROWS // TM, PEER_HEADS),
        in_specs=[pl.BlockSpec((1, 2 * N_KEYS, TM), lambda i, h: (h, 0, i))],
        out_specs=(spec, spec, spec, spec),
        compiler_params=_cparams(("parallel", "parallel")),
        name="peer_topk",
    )(st)


N_CHUNK = N_EXPERTS // EC
N_WORK = (N_ROWS // TT) * N_CHUNK


def _peer_stages(xn_ref, u_ref, vt_ref, rho_ref, c1_ref, r2_ref, e2_ref, acc_ref,
                 at_w, at_r, gt_w, gt_r):
    sub = 16
    n_tb = TT // PEER_LANES
    n_ii = EC // N_KEYS

    mrows = 1024

    def key_matmul(ib, tb):
        rows = slice(ib * mrows, (ib + 1) * mrows)
        cols = slice(tb * PEER_LANES, (tb + 1) * PEER_LANES)
        at_w[rows, cols] = jnp.dot(u_ref[rows, :], xn_ref[:, cols], preferred_element_type=F32)

    def value_matmul(ib, tb):
        rows = slice(ib * mrows, (ib + 1) * mrows)
        cols = slice(tb * PEER_LANES, (tb + 1) * PEER_LANES)
        acc_ref[rows, cols] += jnp.dot(vt_ref[rows, :], gt_r[:, cols], preferred_element_type=F32)

    def gate_build(ii, tb, jbs):
        cols = slice(tb * PEER_LANES, (tb + 1) * PEER_LANES)
        w = {jb: jnp.zeros((sub, PEER_LANES), BF16) for jb in jbs}
        for hh in range(PEER_HEADS):
            rho = jnp.broadcast_to(rho_ref[hh, ii:ii + 1, cols], (sub, PEER_LANES)).astype(BF16)
            cc = jnp.broadcast_to(c1_ref[hh, ii:ii + 1, cols], (sub, PEER_LANES)).astype(BF16)
            for jb in jbs:
                jr = slice(jb * sub, (jb + 1) * sub)
                w[jb] = w[jb] + jnp.where(r2_ref[hh, jr, cols] < rho, e2_ref[hh, jr, cols] * cc, 0.0)
        for jb in jbs:
            rows = slice(ii * N_KEYS + jb * sub, ii * N_KEYS + (jb + 1) * sub)
            a = at_r[rows, cols]
            act = 0.5 * a * (1.0 + lax.erf(a * (2.0 ** -0.5)))
            gt_w[rows, cols] = w[jb] * act.astype(BF16)

    all_jb = range(N_KEYS // sub)
    for tb in range(n_tb):
        pieces = [functools.partial(f, ib, tb) for ib in range(EC // mrows) for f in (key_matmul, value_matmul)]
        per = n_ii // len(pieces)
        for k, piece in enumerate(pieces):
            piece()
            for ii in range(k * per, (k + 1) * per):
                gate_build(ii, tb, all_jb)


def _peer_kernel(xn_ref, h1_ref, u_ref, vt_ref, rho_ref, c1_ref, r2_ref, e2_ref, o_ref,
                 acc_ref, at, gt0, gt1):
    g = pl.program_id(0)
    cv = (g - 1) % N_CHUNK

    @pl.when(g == 0)
    def _():
        gt1[...] = jnp.zeros_like(gt1)

    @pl.when(jnp.logical_or(g < 1, cv == 0))
    def _():
        acc_ref[...] = jnp.zeros_like(acc_ref)

    args = (xn_ref, u_ref, vt_ref, rho_ref, c1_ref, r2_ref, e2_ref, acc_ref)

    @pl.when(g % 2 == 0)
    def _():
        _peer_stages(*args, at, at, gt0, gt1)

    @pl.when(g % 2 == 1)
    def _():
        _peer_stages(*args, at, at, gt1, gt0)

    @pl.when(jnp.logical_and(g >= 1, cv == N_CHUNK - 1))
    def _():
        o_ref[...] = h1_ref[...] + acc_ref[...].T


def _peer(xn, h1, u_bf, vt_bf, rho, c1, r2, e2):
    nk = EC // N_KEYS

    def item(g, lag):
        w = jnp.clip(g - lag, 0, N_WORK - 1)
        return w // N_CHUNK, w % N_CHUNK

    return pl.pallas_call(
        _peer_kernel,
        out_shape=jax.ShapeDtypeStruct((N_ROWS, D_MODEL), F32),
        grid=(N_WORK + 1,),
        in_specs=[pl.BlockSpec((D_MODEL, TT), lambda g: (0, item(g, 0)[0])),
                  pl.BlockSpec((TT, D_MODEL), lambda g: (item(g, 1)[0], 0)),
                  pl.BlockSpec((EC, D_MODEL), lambda g: (item(g, 0)[1], 0)),
                  pl.BlockSpec((D_MODEL, EC), lambda g: (0, item(g, 1)[1])),
                  pl.BlockSpec((PEER_HEADS, nk, TT), lambda g: (0, item(g, 0)[1], item(g, 0)[0])),
                  pl.BlockSpec((PEER_HEADS, nk, TT), lambda g: (0, item(g, 0)[1], item(g, 0)[0])),
                  pl.BlockSpec((PEER_HEADS, N_KEYS, TT), lambda g: (0, 0, item(g, 0)[0])),
                  pl.BlockSpec((PEER_HEADS, N_KEYS, TT), lambda g: (0, 0, item(g, 0)[0]))],
        out_specs=pl.BlockSpec((TT, D_MODEL), lambda g: (item(g, 1)[0], 0)),
        scratch_shapes=[pltpu.VMEM((D_MODEL, TT), F32), pltpu.VMEM((EC, TT), F32),
                        pltpu.VMEM((EC, TT), BF16), pltpu.VMEM((EC, TT), BF16)],
        compiler_params=_cparams(("arbitrary",)),
        name="peer_dense",
    )(xn, h1, u_bf, vt_bf, rho, c1, r2, e2)


S_STACK = SWA_GROUP * DEC_SEQ
S_SMALL = N_META + DEC_SEQ
NT_DIMS = (((1,), (1,)), ((), ()))


def _sample_kernel(p_ref, cs_ref, rs_ref, meta_ref, win_ref,
                   cw_ref, cb_ref, cg_ref, cbeta_ref,
                   cos_ref, sin_ref, dmat_ref, dq_ref, dk_ref, sgam_ref, gng_ref, gnb_ref, eye_ref,
                   ones_ref, qn_ref, kn_ref, bwin_ref, bsm_ref, sink_ref,
                   conv_o_ref, ret_o_ref, swa_o_ref, ncs_ref, nrs_ref, nwin_ref,
                   xin_ref, kpad_ref, vpad_ref, ksm_ref, vsm_ref):
    kpad_ref[...] = jnp.zeros_like(kpad_ref)
    vpad_ref[...] = jnp.zeros_like(vpad_ref)
    ksm_ref[...] = jnp.zeros_like(ksm_ref)
    vsm_ref[...] = jnp.zeros_like(vsm_ref)
    lane = lax.broadcasted_iota(jnp.int32, (DEC_SEQ, RET_HEADS * RET_DK), 1)
    first_half = (lane % RET_DK) < (RET_DK // 2)
    cos = cos_ref[...]
    sin = sin_ref[...]
    ones_q = ones_ref[...]
    ones_k = ones_ref[0:128, 0:128]

    for s in range(SG):
        rows = slice(s * DEC_SEQ, (s + 1) * DEC_SEQ)
        p = p_ref[rows, :]

        glu = p[:, C_CONV:C_CONV + CONV_CH] * jax.nn.sigmoid(p[:, C_CONV + CONV_CH:C_RQ])
        xin_ref[0:CONV_WIDTH - 1, :] = cs_ref[0, s]
        xin_ref[CONV_WIDTH - 1:CONV_WIDTH - 1 + DEC_SEQ, :] = glu
        acc = jnp.zeros((DEC_SEQ, CONV_CH), F32)
        for k in range(CONV_WIDTH):
            acc = acc + xin_ref[k:k + DEC_SEQ, :] * cw_ref[k:k + 1, :]
        y = acc + cb_ref[...]
        mu = jnp.mean(y, axis=-1, keepdims=True)
        d = y - mu
        var = jnp.mean(d * d, axis=-1, keepdims=True)
        yn = d * lax.rsqrt(var + EPS) * cg_ref[...] + cbeta_ref[...]
        conv_o_ref[rows, :] = yn * jax.nn.sigmoid(yn)
        ncs_ref[s] = xin_ref[DEC_SEQ:DEC_SEQ + CONV_WIDTH - 1, :]

        q = p[:, C_RQ:C_RK]
        k = p[:, C_RK:C_RV]
        q = q * cos + _swap_halves(q, first_half) * sin
        k = (k * cos + _swap_halves(k, first_half) * sin) * (RET_DK ** -0.5)
        rg = p[:, C_RG:C_SQ]
        kpad_ref[0:DEC_SEQ, :] = k
        vpad_ref[0:DEC_SEQ, :] = p[:, C_RV:C_RG]
        for h in range(RET_HEADS):
            qh = q[:, h * RET_DK:(h + 1) * RET_DK]
            kp = kpad_ref[:, h * RET_DK:(h + 1) * RET_DK]
            vp = vpad_ref[:, h * RET_DV:(h + 1) * RET_DV]
            s_old = rs_ref[0, s, h]
            inner = lax.dot_general(qh, kp, NT_DIMS, preferred_element_type=F32) * dmat_ref[h]
            o = (jnp.dot(inner, vp, preferred_element_type=F32)
                 + dq_ref[h] * jnp.dot(qh, s_old, preferred_element_type=F32))
            kdec_t = lax.dot_general(eye_ref[...], kp * dk_ref[h], NT_DIMS,
                                     preferred_element_type=F32)
            nrs_ref[s, h] = sgam_ref[h] * s_old + jnp.dot(kdec_t, vp, preferred_element_type=F32)
            mu = jnp.mean(o, axis=-1, keepdims=True)
            d = o - mu
            var = jnp.mean(d * d, axis=-1, keepdims=True)
            hs = slice(h * RET_DV, (h + 1) * RET_DV)
            yr = d * lax.rsqrt(var + EPS) * gng_ref[:, hs] + gnb_ref[:, hs]
            gate = rg[:, hs]
            ret_o_ref[rows, hs] = yr * (gate * jax.nn.sigmoid(gate))

        sq = p[:, C_SQ:C_SK]
        sk = p[:, C_SK:C_SV]
        qss = jnp.dot(sq * sq, ones_q, preferred_element_type=F32)
        kss = jnp.dot(sk * sk, ones_k, preferred_element_type=F32)
        qa = sq * lax.rsqrt(qss * (1.0 / SWA_HD) + EPS) * qn_ref[...] * (SWA_HD ** -0.5)
        kn = sk * lax.rsqrt(kss * (1.0 / SWA_HD) + EPS) * kn_ref[...]
        vn = p[:, C_SV:PROJ_COLS]
        ksm_ref[0:N_META, :] = meta_ref[0, s, :, 0:128]
        vsm_ref[0:N_META, :] = meta_ref[0, s, :, 128:256]
        ksm_ref[N_META:S_SMALL, :] = kn
        vsm_ref[N_META:S_SMALL, :] = vn
        for kv in range(SWA_KV_HEADS):
            ksl = slice(kv * SWA_HD, (kv + 1) * SWA_HD)
            vsl = slice(128 + kv * SWA_HD, 128 + (kv + 1) * SWA_HD)
            qs = jnp.concatenate([qa[:, (kv * SWA_GROUP + g) * SWA_HD:(kv * SWA_GROUP + g + 1) * SWA_HD]
                                  for g in range(SWA_GROUP)], axis=0)
            s_w = lax.dot_general(qs, win_ref[0, s, :, ksl], NT_DIMS, preferred_element_type=F32) + bwin_ref[kv]
            s_s = lax.dot_general(qs, ksm_ref[:, ksl], NT_DIMS, preferred_element_type=F32) + bsm_ref[kv]
            sink = sink_ref[kv, :, 0:1]
            m = jnp.maximum(jnp.maximum(jnp.max(s_w, axis=-1, keepdims=True),
                                        jnp.max(s_s, axis=-1, keepdims=True)), sink)
            p_w = jnp.exp(s_w - m)
            p_s = jnp.exp(s_s - m)
            den = (jnp.sum(p_w, axis=-1, keepdims=True) + jnp.sum(p_s, axis=-1, keepdims=True)
                   + jnp.exp(sink - m))
            o = (jnp.dot(p_w, win_ref[0, s, :, vsl], preferred_element_type=F32)
                 + jnp.dot(p_s, vsm_ref[:, ksl], preferred_element_type=F32)) / den
            for g in range(SWA_GROUP):
                hh = kv * SWA_GROUP + g
                swa_o_ref[rows, hh * SWA_HD:(hh + 1) * SWA_HD] = o[g * DEC_SEQ:(g + 1) * DEC_SEQ, :]
        nwin_ref[s, 0:WINDOW - DEC_SEQ, :] = win_ref[0, s, DEC_SEQ:WINDOW, :]
        nwin_ref[s, WINDOW - DEC_SEQ:WINDOW, 0:128] = kn
        nwin_ref[s, WINDOW - DEC_SEQ:WINDOW, 128:256] = vn


def _sample_tables(rel_bias):
    lg = jnp.log(1.0 - 2.0 ** (-5.0 - jnp.arange(RET_HEADS, dtype=F32)))
    i = jnp.arange(DEC_SEQ, dtype=F32)
    diff = i[:, None] - i[None, :]
    dm = jnp.where(diff >= 0, jnp.exp(jnp.maximum(diff, 0.0)[None] * lg[:, None, None]), 0.0)
    dmat = jnp.zeros((RET_HEADS, DEC_SEQ, 128), F32).at[:, :, :DEC_SEQ].set(dm)
    dq = jnp.broadcast_to(jnp.exp((i[None] + 1.0) * lg[:, None])[:, :, None], (RET_HEADS, DEC_SEQ, RET_DV))
    kd = jnp.exp((DEC_SEQ - 1.0 - i)[None] * lg[:, None])
    dk = jnp.zeros((RET_HEADS, 128, RET_DK), F32).at[:, :DEC_SEQ, :].set(
        jnp.broadcast_to(kd[:, :, None], (RET_HEADS, DEC_SEQ, RET_DK)))
    sg = jnp.broadcast_to(jnp.exp(DEC_SEQ * lg)[:, None, None], (RET_HEADS, 1, RET_DV))
    cos_t, sin_t = _rotary_tables(PAST_LEN + jnp.arange(DEC_SEQ))

    ti = np.arange(S_STACK)[:, None] % DEC_SEQ
    j = np.arange(WINDOW)[None, :]
    bk_win = _t5_bucket_np(np.maximum(ti + WINDOW - j, 0))
    ok_win = j > ti
    c = np.arange(128)[None, :]
    jn = c - N_META
    bk_new = _t5_bucket_np(np.clip(ti - jn, 0, None))
    ok_sm = (c < N_META) | ((c < S_SMALL) & (jn <= ti))
    bk_sm = np.where(c < N_META, N_BUCKETS - 1, bk_new)
    rb = rel_bias.astype(F32)
    head = np.arange(SWA_KV_HEADS)[:, None] * SWA_GROUP + (np.arange(S_STACK) // DEC_SEQ)[None, :]
    head_oh = _one_hot(head, SWA_HEADS)

    def look(bk, ok):
        b = jnp.einsum("rcb,bh,krh->krc", _one_hot(bk, N_BUCKETS), rb, head_oh,
                       precision=lax.Precision.HIGHEST)
        return jnp.where(jnp.asarray(np.broadcast_to(ok[None], b.shape)), b, NEG)

    return dict(dmat=dmat, dq=dq, dk=dk, sg=sg, cos=cos_t, sin=sin_t,
                bwin=look(bk_win, ok_win), bsm=look(bk_sm, ok_sm), head_oh=head_oh)


def _sample_mixers(p, l, cache_meta_kv, cache_swa_kv, state_ret, state_conv, tabs, conv_w32, conv_b,
                   conv_g, conv_beta, gn_g, gn_b, q_norm, k_norm, sinks):
    row0 = NP_ROWS // (SG * DEC_SEQ)
    nrow = SG * DEC_SEQ
    ones_bd = jnp.asarray(np.kron(np.eye(SWA_HEADS), np.ones((SWA_HD, SWA_HD))), F32)
    qn = jnp.tile(q_norm.astype(F32), SWA_HEADS).reshape(1, -1)
    kn = jnp.tile(k_norm.astype(F32), SWA_KV_HEADS).reshape(1, -1)
    sink_rows = jnp.einsum("h,krh->kr", sinks.astype(F32), tabs["head_oh"], precision=lax.Precision.HIGHEST)
    sink_t = jnp.broadcast_to(sink_rows[:, :, None], (SWA_KV_HEADS, S_STACK, 128))
    meta = cache_meta_kv.reshape(DEPTH, DEC_BATCH, N_META, 256)
    win = cache_swa_kv.reshape(DEPTH, DEC_BATCH, WINDOW, 256)

    def const(shape):
        return pl.BlockSpec(shape, lambda i: (0,) * len(shape))

    return pl.pallas_call(
        _sample_kernel,
        out_shape=(jax.ShapeDtypeStruct((NS_ROWS, CONV_CH), F32),
                   jax.ShapeDtypeStruct((NS_ROWS, RET_HEADS * RET_DV), F32),
                   jax.ShapeDtypeStruct((NS_ROWS, SWA_HEADS * SWA_HD), F32),
                   jax.ShapeDtypeStruct((DEC_BATCH, CONV_WIDTH - 1, CONV_CH), F32),
                   jax.ShapeDtypeStruct((DEC_BATCH, RET_HEADS, RET_DK, RET_DV), F32),
                   jax.ShapeDtypeStruct((DEC_BATCH, WINDOW, 256), F32)),
        grid=(DEC_BATCH // SG,),
        in_specs=[pl.BlockSpec((nrow, PROJ_COLS), lambda i: (row0 + i, 0)),
                  pl.BlockSpec((1, SG, CONV_WIDTH - 1, CONV_CH), lambda i: (l, i, 0, 0)),
                  pl.BlockSpec((1, SG, RET_HEADS, RET_DK, RET_DV), lambda i: (l, i, 0, 0, 0)),
                  pl.BlockSpec((1, SG, N_META, 256), lambda i: (l, i, 0, 0)),
                  pl.BlockSpec((1, SG, WINDOW, 256), lambda i: (l, i, 0, 0)),
                  const((32, CONV_CH)), const((1, CONV_CH)), const((1, CONV_CH)), const((1, CONV_CH)),
                  const((DEC_SEQ, 256)), const((DEC_SEQ, 256)),
                  const((RET_HEADS, DEC_SEQ, 128)), const((RET_HEADS, DEC_SEQ, RET_DV)),
                  const((RET_HEADS, 128, RET_DK)), const((RET_HEADS, 1, RET_DV)),
                  const((1, 512)), const((1, 512)), const((RET_DK, RET_DK)),
                  const((512, 512)), const((1, 512)), const((1, 128)),
                  const((SWA_KV_HEADS, S_STACK, 128)), const((SWA_KV_HEADS, S_STACK, 128)),
                  const((SWA_KV_HEADS, S_STACK, 128))],
        out_specs=(pl.BlockSpec((nrow, CONV_CH), lambda i: (i, 0)),
                   pl.BlockSpec((nrow, 512), lambda i: (i, 0)),
                   pl.BlockSpec((nrow, 512), lambda i: (i, 0)),
                   pl.BlockSpec((SG, CONV_WIDTH - 1, CONV_CH), lambda i: (i, 0, 0)),
                   pl.BlockSpec((SG, RET_HEADS, RET_DK, RET_DV), lambda i: (i, 0, 0, 0)),
                   pl.BlockSpec((SG, WINDOW, 256), lambda i: (i, 0, 0))),
        scratch_shapes=[pltpu.VMEM((40, CONV_CH), F32), pltpu.VMEM((128, RET_HEADS * RET_DK), F32),
                        pltpu.VMEM((128, RET_HEADS * RET_DV), F32),
                        pltpu.VMEM((128, 128), F32), pltpu.VMEM((128, 128), F32)],
        compiler_params=_cparams(("parallel",)),
        name="sample_mixers",
    )(p, state_conv, state_ret, meta, win,
      conv_w32, conv_b.reshape(1, -1), conv_g.reshape(1, -1), conv_beta.reshape(1, -1),
      tabs["cos"], tabs["sin"], tabs["dmat"], tabs["dq"], tabs["dk"], tabs["sg"],
      gn_g.reshape(1, -1), gn_b.reshape(1, -1), jnp.eye(RET_DK, dtype=F32),
      ones_bd, qn, kn, tabs["bwin"], tabs["bsm"], sink_t)


def kernel(x_prompt, x_sample, cache_meta_kv, cache_swa_kv, state_ret, state_conv, meta_tokens, rel_bias,
           norm_mix, w_in, conv_w, conv_b, conv_ln_g, conv_ln_b, ret_gn_g, ret_gn_b, swa_q_norm,
           swa_k_norm, swa_sinks, w_out, norm_ffn, peer_wq, peer_keys, peer_u, peer_v):
    meta = jnp.broadcast_to(meta_tokens.astype(F32)[None], (BATCH, N_META, D_MODEL))
    pad = jnp.zeros((BATCH, LP - L_REAL, D_MODEL), F32)
    hp = jnp.concatenate([meta, x_prompt, pad], axis=1).reshape(NP_ROWS, D_MODEL)
    h = jnp.concatenate([hp, x_sample.reshape(NS_ROWS, D_MODEL)], axis=0)

    cos_t, sin_t = _rotary_tables(jnp.arange(LP))
    ret_tabs = _ret_tables()
    bias_tabs = _bias_tables(rel_bias)
    sample_tabs = _sample_tables(rel_bias)

    meta_p, win_p, ret_p, conv_p, win_s, ret_s, conv_s = [], [], [], [], [], [], []
    for l in range(DEPTH):
        p = _norm_proj(h, norm_mix[l], w_in[l].astype(BF16))

        w32 = jnp.concatenate([conv_w[l], jnp.zeros((1, CONV_CH), F32)], axis=0)
        conv_o, conv_tail = _conv_prompt(p, w32, conv_b[l], conv_ln_g[l], conv_ln_b[l])
        ret_o, ret_state = _ret_prompt(p, cos_t, sin_t, ret_tabs, ret_gn_g[l], ret_gn_b[l])
        swa_o, k_normed = _swa_prompt(p, swa_sinks[l], swa_q_norm[l], swa_k_norm[l], bias_tabs)
        s_conv, s_ret, s_swa, new_conv, new_ret, new_win = _sample_mixers(
            p, l, cache_meta_kv, cache_swa_kv, state_ret, state_conv, sample_tabs, w32, conv_b[l],
            conv_ln_g[l], conv_ln_b[l], ret_gn_g[l], ret_gn_b[l], swa_q_norm[l], swa_k_norm[l], swa_sinks[l])
        new_win = new_win.reshape(DEC_BATCH, WINDOW, 2, SWA_KV_HEADS, SWA_HD)

        kbd = jnp.zeros((2 * N_KEYS, 2 * PEER_HALF), F32)
        kbd = kbd.at[:N_KEYS, :PEER_HALF].set(peer_keys[l, 0]).at[N_KEYS:, PEER_HALF:].set(peer_keys[l, 1])
        h1, xn, st = _out_ffn(h, (conv_o, ret_o, swa_o), (s_conv, s_ret, s_swa), w_out[l].astype(BF16),
                              norm_ffn[l], peer_wq[l].astype(BF16), kbd.astype(BF16))
        rho, c1, r2, e2 = _topk(st)
        h = _peer(xn, h1, peer_u[l].astype(BF16), peer_v[l].T.astype(BF16), rho, c1, r2, e2)

        kp = k_normed.reshape(BATCH, LP, SWA_KV_HEADS, SWA_HD)
        vp = p[:NP_ROWS, C_SV:].reshape(BATCH, LP, SWA_KV_HEADS, SWA_HD)
        kvp = jnp.stack([kp, vp], axis=2)
        meta_p.append(kvp[:, :N_META])
        win_p.append(kvp[:, L_REAL - WINDOW:L_REAL])
        ret_p.append(ret_state)
        conv_p.append(conv_tail[:, 32 - (CONV_WIDTH - 1):])
        win_s.append(new_win)
        ret_s.append(new_ret)
        conv_s.append(new_conv)

    y_prompt = h[:NP_ROWS].reshape(BATCH, LP, D_MODEL)[:, N_META:L_REAL]
    y_sample = h[NP_ROWS:].reshape(DEC_BATCH, DEC_SEQ, D_MODEL)
    return (y_prompt, y_sample, jnp.stack(meta_p), jnp.stack(win_p), jnp.stack(ret_p), jnp.stack(conv_p),
            jnp.stack(win_s), jnp.stack(ret_s), jnp.stack(conv_s))
```

```python
import functools
import math

import numpy as np
import jax
import jax.numpy as jnp
from jax import lax
from jax.experimental import pallas as pl
from jax.experimental.pallas import tpu as pltpu

F32 = jnp.float32
BF16 = jnp.bfloat16

D_MODEL = 1024
BATCH = 2
SEQ = 8192
DEPTH = 2
DEC_BATCH = 128
DEC_SEQ = 8
PAST_LEN = 8192
N_META = 16
CONV_CH = 512
CONV_WIDTH = 31
RET_HEADS = 4
RET_DK = 64
RET_DV = 128
SWA_HEADS = 8
SWA_KV_HEADS = 2
SWA_GROUP = SWA_HEADS // SWA_KV_HEADS
SWA_HD = 64
WINDOW = 128
N_BUCKETS = 32
REL_MAX_DIST = 128
PEER_HEADS = 8
N_KEYS = 128
N_EXPERTS = N_KEYS * N_KEYS
PEER_TOPK = 16
PEER_HALF = 64
EPS = 1e-6
NEG = -1e30

PROJ_COLS = 3328
C_CONV, C_RQ, C_RK, C_RV, C_RG, C_SQ, C_SK, C_SV = 0, 1024, 1280, 1536, 2048, 2560, 3072, 3200

L_REAL = N_META + SEQ
BLK = 128
N_BLK = 65
LP = N_BLK * BLK
NP_ROWS = BATCH * LP
NS_ROWS = DEC_BATCH * DEC_SEQ
N_ROWS = NP_ROWS + NS_ROWS
LAST_REAL = L_REAL - (N_BLK - 1) * BLK

TM = 384
TO = 256
TT = 768
SG = 8
EC = 1024
PEER_LANES = 256
CONV_T = 640
CONV_RB = 64
VMEM_LIMIT = 56 * 1024 * 1024


def _cparams(sem, flags=None):
    return pltpu.CompilerParams(dimension_semantics=sem, vmem_limit_bytes=VMEM_LIMIT, flags=flags)


def _norm_proj_kernel(x_ref, g_ref, w_ref, o_ref):
    x = x_ref[...]
    ms = jnp.mean(x * x, axis=-1, keepdims=True)
    xn = x * lax.rsqrt(ms + EPS) * g_ref[...]
    o_ref[...] = jnp.dot(xn.astype(BF16), w_ref[...], preferred_element_type=F32)


def _norm_proj(h, g, w_bf):
    return pl.pallas_call(
        _norm_proj_kernel,
        out_shape=jax.ShapeDtypeStruct((N_ROWS, PROJ_COLS), F32),
        grid=(N_ROWS // TM,),
        in_specs=[pl.BlockSpec((TM, D_MODEL), lambda i: (i, 0)),
                  pl.BlockSpec((1, D_MODEL), lambda i: (0, 0)),
                  pl.BlockSpec((D_MODEL, PROJ_COLS), lambda i: (0, 0))],
        out_specs=pl.BlockSpec((TM, PROJ_COLS), lambda i: (i, 0)),
        compiler_params=_cparams(("parallel",)),
        name="norm_proj",
    )(h, g.reshape(1, D_MODEL), w_bf)


def _conv_kernel(c_ref, w_ref, b_ref, g_ref, beta_ref, o_ref, st_ref, xin_ref):
    t = pl.program_id(1)

    @pl.when(t == 0)
    def _():
        xin_ref[0:32, :] = jnp.zeros((32, CONV_CH), F32)

    @pl.when(t > 0)
    def _():
        xin_ref[0:32, :] = xin_ref[CONV_T:CONV_T + 32, :]

    c = c_ref[...]
    xin_ref[32:32 + CONV_T, :] = c[:, :CONV_CH] * jax.nn.sigmoid(c[:, CONV_CH:])

    w = w_ref[...]
    bias = b_ref[...]
    gam = g_ref[...]
    beta = beta_ref[...]
    for rb in range(CONV_T // CONV_RB):
        r0 = rb * CONV_RB
        acc = jnp.zeros((CONV_RB, CONV_CH), F32)
        for k in range(CONV_WIDTH):
            acc = acc + xin_ref[r0 + 2 + k:r0 + 2 + k + CONV_RB, :] * w[k:k + 1, :]
        y = acc + bias
        mu = jnp.mean(y, axis=-1, keepdims=True)
        d = y - mu
        var = jnp.mean(d * d, axis=-1, keepdims=True)
        yn = d * lax.rsqrt(var + EPS) * gam + beta
        o_ref[r0:r0 + CONV_RB, :] = yn * jax.nn.sigmoid(yn)

    @pl.when(t == pl.num_programs(1) - 1)
    def _():
        lo = 32 + (L_REAL - 32) - (LP - CONV_T)
        st_ref[0] = xin_ref[lo:lo + 32, :]


def _conv_prompt(p, w32, b, g, beta):
    nt = LP // CONV_T
    return pl.pallas_call(
        _conv_kernel,
        out_shape=(jax.ShapeDtypeStruct((NP_ROWS, CONV_CH), F32),
                   jax.ShapeDtypeStruct((BATCH, 32, CONV_CH), F32)),
        grid=(BATCH, nt),
        in_specs=[pl.BlockSpec((CONV_T, 2 * CONV_CH), lambda bi, t: (bi * nt + t, 0)),
                  pl.BlockSpec((32, CONV_CH), lambda bi, t: (0, 0)),
                  pl.BlockSpec((1, CONV_CH), lambda bi, t: (0, 0)),
                  pl.BlockSpec((1, CONV_CH), lambda bi, t: (0, 0)),
                  pl.BlockSpec((1, CONV_CH), lambda bi, t: (0, 0))],
        out_specs=(pl.BlockSpec((CONV_T, CONV_CH), lambda bi, t: (bi * nt + t, 0)),
                   pl.BlockSpec((1, 32, CONV_CH), lambda bi, t: (bi, 0, 0))),
        scratch_shapes=[pltpu.VMEM((32 + CONV_T + 32, CONV_CH), F32)],
        compiler_params=_cparams(("arbitrary", "arbitrary")),
        name="conv_prompt",
    )(p, w32, b.reshape(1, -1), g.reshape(1, -1), beta.reshape(1, -1))


def _swap_halves(x, first_half):
    return jnp.where(first_half, pltpu.roll(x, x.shape[1] - 32, axis=1), pltpu.roll(x, 32, axis=1))


def _ret_kernel(q_ref, k_ref, v_ref, rg_ref, cos_ref, sin_ref, dmat_ref, dq_ref, dk_ref, sg_ref,
                gng_ref, gnb_ref, o_ref, st_ref, s_ref):
    j = pl.program_id(1)

    @pl.when(j == 0)
    def _():
        s_ref[...] = jnp.zeros_like(s_ref)

    cos = cos_ref[...]
    sin = sin_ref[...]
    lane = lax.broadcasted_iota(jnp.int32, (BLK, RET_HEADS * RET_DK), 1)
    first_half = (lane % RET_DK) < (RET_DK // 2)
    q = q_ref[...]
    k = k_ref[...]
    q = q * cos + _swap_halves(q, first_half) * sin
    k = (k * cos + _swap_halves(k, first_half) * sin) * (RET_DK ** -0.5)
    v = v_ref[...]
    rg = rg_ref[...]
    for h in range(RET_HEADS):
        qh = q[:, h * RET_DK:(h + 1) * RET_DK].astype(BF16)
        kh = k[:, h * RET_DK:(h + 1) * RET_DK]
        vh = v[:, h * RET_DV:(h + 1) * RET_DV].astype(BF16)
        s_old = s_ref[h]
        inner = lax.dot_general(qh, kh.astype(BF16), (((1,), (1,)), ((), ())),
                                preferred_element_type=F32) * dmat_ref[h]
        o = (jnp.dot(inner.astype(BF16), vh, preferred_element_type=F32)
             + dq_ref[h] * jnp.dot(qh, s_old.astype(BF16), preferred_element_type=F32))
        kdec_t = (kh * dk_ref[0, h]).T.astype(BF16)
        s_new = sg_ref[0, h, 0:1, :] * s_old + jnp.dot(kdec_t, vh, preferred_element_type=F32)
        s_ref[h] = s_new
        mu = jnp.mean(o, axis=-1, keepdims=True)
        d = o - mu
        var = jnp.mean(d * d, axis=-1, keepdims=True)
        y = d * lax.rsqrt(var + EPS) * gng_ref[:, h * RET_DV:(h + 1) * RET_DV] \
            + gnb_ref[:, h * RET_DV:(h + 1) * RET_DV]
        gate = rg[:, h * RET_DV:(h + 1) * RET_DV]
        o_ref[:, h * RET_DV:(h + 1) * RET_DV] = y * (gate * jax.nn.sigmoid(gate))

    @pl.when(j == pl.num_programs(1) - 1)
    def _():
        st_ref[0] = s_ref[...]


def _ret_tables():
    lg = jnp.log(1.0 - 2.0 ** (-5.0 - jnp.arange(RET_HEADS, dtype=F32)))
    i = jnp.arange(BLK, dtype=F32)
    diff = i[:, None] - i[None, :]
    dmat = jnp.where(diff >= 0, jnp.exp(jnp.maximum(diff, 0.0)[None] * lg[:, None, None]), 0.0)
    dq = jnp.broadcast_to(jnp.exp((i[None] + 1.0) * lg[:, None])[:, :, None], (RET_HEADS, BLK, RET_DV))

    def kdec(c_eff):
        e = jnp.where(i[None] < c_eff, jnp.exp((c_eff - 1.0 - i)[None] * lg[:, None]), 0.0)
        return jnp.broadcast_to(e[:, :, None], (RET_HEADS, BLK, RET_DK))

    def sgam(c_eff):
        return jnp.broadcast_to(jnp.exp(c_eff * lg)[:, None, None], (RET_HEADS, 8, RET_DV))

    dk = jnp.stack([kdec(float(BLK)), kdec(float(LAST_REAL))])
    sg = jnp.stack([sgam(float(BLK)), sgam(float(LAST_REAL))])
    return dmat, dq, dk, sg


def _rotary_tables(pos):
    half = RET_DK // 2
    inv = 1.0 / (10000.0 ** (jnp.arange(half, dtype=F32) / half))
    ang = pos.astype(F32)[:, None] * inv[None]
    cos, sin = jnp.cos(ang), jnp.sin(ang)
    cos_t = jnp.tile(jnp.concatenate([cos, cos], axis=-1), (1, RET_HEADS))
    sin_t = jnp.tile(jnp.concatenate([-sin, sin], axis=-1), (1, RET_HEADS))
    return cos_t, sin_t


def _ret_prompt(p, cos_t, sin_t, tabs, gn_g, gn_b):
    dmat, dq, dk, sg = tabs
    nb = N_BLK
    last = nb - 1
    return pl.pallas_call(
        _ret_kernel,
        out_shape=(jax.ShapeDtypeStruct((NP_ROWS, RET_HEADS * RET_DV), F32),
                   jax.ShapeDtypeStruct((BATCH, RET_HEADS, RET_DK, RET_DV), F32)),
        grid=(BATCH, nb),
        in_specs=[pl.BlockSpec((BLK, 256), lambda b, j: (b * nb + j, C_RQ // 256)),
                  pl.BlockSpec((BLK, 256), lambda b, j: (b * nb + j, C_RK // 256)),
                  pl.BlockSpec((BLK, 512), lambda b, j: (b * nb + j, C_RV // 512)),
                  pl.BlockSpec((BLK, 512), lambda b, j: (b * nb + j, C_RG // 512)),
                  pl.BlockSpec((BLK, 256), lambda b, j: (j, 0)),
                  pl.BlockSpec((BLK, 256), lambda b, j: (j, 0)),
                  pl.BlockSpec((RET_HEADS, BLK, BLK), lambda b, j: (0, 0, 0)),
                  pl.BlockSpec((RET_HEADS, BLK, RET_DV), lambda b, j: (0, 0, 0)),
                  pl.BlockSpec((1, RET_HEADS, BLK, RET_DK), lambda b, j: (j // last, 0, 0, 0)),
                  pl.BlockSpec((1, RET_HEADS, 8, RET_DV), lambda b, j: (j // last, 0, 0, 0)),
                  pl.BlockSpec((1, 512), lambda b, j: (0, 0)),
                  pl.BlockSpec((1, 512), lambda b, j: (0, 0))],
        out_specs=(pl.BlockSpec((BLK, 512), lambda b, j: (b * nb + j, 0)),
                   pl.BlockSpec((1, RET_HEADS, RET_DK, RET_DV), lambda b, j: (b, 0, 0, 0))),
        scratch_shapes=[pltpu.VMEM((RET_HEADS, RET_DK, RET_DV), F32)],
        compiler_params=_cparams(("arbitrary", "arbitrary")),
        name="ret_prompt",
    )(p, p, p, p, cos_t, sin_t, dmat, dq, dk, sg, gn_g.reshape(1, -1), gn_b.reshape(1, -1))


def _group_rms(x, ones_bd, w):
    x2 = x * x
    hi = x2.astype(BF16)
    lo = (x2 - hi.astype(F32)).astype(BF16)
    ss = (jnp.dot(hi, ones_bd, preferred_element_type=F32)
          + jnp.dot(lo, ones_bd, preferred_element_type=F32))
    return x * lax.rsqrt(ss * (1.0 / SWA_HD) + EPS) * w


def _swa_kernel(sinks_ref, q_ref, kc_ref, vc_ref, kp_ref, vp_ref, km_ref, vm_ref, ones_ref,
                qn_ref, kn_ref, bc_ref, bp_ref, bm_ref, o_ref, kout_ref):
    j = pl.program_id(1)
    ones_q = ones_ref[...]
    ones_k = ones_ref[0:128, 0:128]
    qw = qn_ref[...]
    kw = kn_ref[...]
    q = _group_rms(q_ref[...], ones_q, qw) * (SWA_HD ** -0.5)
    kc = _group_rms(kc_ref[...], ones_k, kw)
    kp = _group_rms(kp_ref[...], ones_k, kw)
    km = _group_rms(km_ref[...], ones_k, kw)
    kout_ref[...] = kc
    vc = vc_ref[...].astype(BF16)
    vp = vp_ref[...].astype(BF16)
    vm = vm_ref[...].astype(BF16)

    rows = SWA_GROUP * BLK
    qi = lax.broadcasted_iota(jnp.int32, (rows, BLK), 0) % BLK
    kj = lax.broadcasted_iota(jnp.int32, (rows, BLK), 1)
    valid_c = jnp.where(kj <= qi, j * BLK + kj, -1) >= N_META
    valid_p = jnp.where(kj > qi, (j - 1) * BLK + kj, -1) >= N_META
    qm = lax.broadcasted_iota(jnp.int32, (rows, N_META), 0) % BLK
    mm = lax.broadcasted_iota(jnp.int32, (rows, N_META), 1)
    valid_m = mm <= j * BLK + qm

    dn = (((1,), (1,)), ((), ()))
    for kv in range(SWA_KV_HEADS):
        heads = range(kv * SWA_GROUP, (kv + 1) * SWA_GROUP)
        qs = jnp.concatenate([q[:, h * SWA_HD:(h + 1) * SWA_HD] for h in heads], axis=0).astype(BF16)
        ksl = slice(kv * SWA_HD, (kv + 1) * SWA_HD)
        s_c = lax.dot_general(qs, kc[:, ksl].astype(BF16), dn, preferred_element_type=F32)
        s_p = lax.dot_general(qs, kp[:, ksl].astype(BF16), dn, preferred_element_type=F32)
        s_m = lax.dot_general(qs, km[:, ksl].astype(BF16), dn, preferred_element_type=F32)
        s_c = jnp.where(valid_c, s_c + bc_ref[kv], NEG)
        s_p = jnp.where(valid_p, s_p + bp_ref[kv], NEG)
        s_m = jnp.where(valid_m, s_m + bm_ref[0, kv], NEG)
        sink = jnp.concatenate([jnp.full((BLK, 1), sinks_ref[h], F32) for h in heads], axis=0)
        m = jnp.maximum(jnp.maximum(jnp.max(s_c, axis=-1, keepdims=True),
                                    jnp.max(s_p, axis=-1, keepdims=True)),
                        jnp.maximum(jnp.max(s_m, axis=-1, keepdims=True), sink))
        p_c = jnp.exp(s_c - m)
        p_p = jnp.exp(s_p - m)
        p_m = jnp.exp(s_m - m)
        den = (jnp.sum(p_c, axis=-1, keepdims=True) + jnp.sum(p_p, axis=-1, keepdims=True)
               + jnp.sum(p_m, axis=-1, keepdims=True) + jnp.exp(sink - m))
        acc = (jnp.dot(p_c.astype(BF16), vc[:, ksl], preferred_element_type=F32)
               + jnp.dot(p_p.astype(BF16), vp[:, ksl], preferred_element_type=F32)
               + jnp.dot(p_m.astype(BF16), vm[:, ksl], preferred_element_type=F32))
        out = acc / den
        for g, h in enumerate(heads):
            o_ref[:, h * SWA_HD:(h + 1) * SWA_HD] = out[g * BLK:(g + 1) * BLK, :]


def _one_hot(idx, n):
    return jnp.asarray(np.asarray(idx)[..., None] == np.arange(n), F32)


def _t5_bucket_np(dist):
    max_exact = N_BUCKETS // 2
    df = np.maximum(dist, 1).astype(np.float64)
    large = max_exact + (np.log(df / max_exact) / math.log(REL_MAX_DIST / max_exact)
                         * (N_BUCKETS - max_exact)).astype(np.int64)
    large = np.minimum(large, N_BUCKETS - 1)
    return np.where(dist < max_exact, dist, large).astype(np.int32)


def _bias_tables(rel_bias):
    qi = np.arange(BLK)[:, None]
    kj = np.arange(BLK)[None, :]
    b_cur = _t5_bucket_np(np.maximum(qi - kj, 0))
    b_prev = _t5_bucket_np(np.maximum(qi - kj + BLK, 0))
    m = np.arange(N_META)[None, :]
    b_m0 = _t5_bucket_np(np.maximum(qi - m, 0))
    b_m1 = _t5_bucket_np(np.maximum(qi + BLK - m, 0))
    assert (b_m1 == N_BUCKETS - 1).all()
    rb = rel_bias.astype(F32)

    def look(bk):
        return jnp.einsum("...b,bh->h...", _one_hot(bk, N_BUCKETS), rb, precision=lax.Precision.HIGHEST)

    return look(b_cur), look(b_prev), jnp.stack([look(b_m0), look(b_m1)])


def _swa_prompt(p, sinks, q_norm, k_norm, bias_tabs):
    srows = SWA_GROUP * BLK
    b_cur = bias_tabs[0].reshape(SWA_KV_HEADS, srows, BLK)
    b_prev = bias_tabs[1].reshape(SWA_KV_HEADS, srows, BLK)
    b_meta = bias_tabs[2].reshape(2, SWA_KV_HEADS, srows, N_META)
    nb = N_BLK
    ones_bd = jnp.asarray(np.kron(np.eye(SWA_HEADS), np.ones((SWA_HD, SWA_HD))), BF16)
    qn = jnp.tile(q_norm.astype(F32), SWA_HEADS).reshape(1, -1)
    kn = jnp.tile(k_norm.astype(F32), SWA_KV_HEADS).reshape(1, -1)
    ck, cv = C_SK // 128, C_SV // 128
    return pl.pallas_call(
        _swa_kernel,
        out_shape=(jax.ShapeDtypeStruct((NP_ROWS, SWA_HEADS * SWA_HD), F32),
                   jax.ShapeDtypeStruct((NP_ROWS, SWA_KV_HEADS * SWA_HD), F32)),
        grid=(BATCH, nb),
        in_specs=[pl.BlockSpec(memory_space=pltpu.SMEM),
                  pl.BlockSpec((BLK, 512), lambda b, j: (b * nb + j, C_SQ // 512)),
                  pl.BlockSpec((BLK, 128), lambda b, j: (b * nb + j, ck)),
                  pl.BlockSpec((BLK, 128), lambda b, j: (b * nb + j, cv)),
                  pl.BlockSpec((BLK, 128), lambda b, j: (b * nb + jnp.maximum(j - 1, 0), ck)),
                  pl.BlockSpec((BLK, 128), lambda b, j: (b * nb + jnp.maximum(j - 1, 0), cv)),
                  pl.BlockSpec((N_META, 128), lambda b, j: (b * (LP // N_META), ck)),
                  pl.BlockSpec((N_META, 128), lambda b, j: (b * (LP // N_META), cv)),
                  pl.BlockSpec((512, 512), lambda b, j: (0, 0)),
                  pl.BlockSpec((1, 512), lambda b, j: (0, 0)),
                  pl.BlockSpec((1, 128), lambda b, j: (0, 0)),
                  pl.BlockSpec((SWA_KV_HEADS, srows, BLK), lambda b, j: (0, 0, 0)),
                  pl.BlockSpec((SWA_KV_HEADS, srows, BLK), lambda b, j: (0, 0, 0)),
                  pl.BlockSpec((1, SWA_KV_HEADS, srows, N_META), lambda b, j: (jnp.minimum(j, 1), 0, 0, 0))],
        out_specs=(pl.BlockSpec((BLK, 512), lambda b, j: (b * nb + j, 0)),
                   pl.BlockSpec((BLK, 128), lambda b, j: (b * nb + j, 0))),
        compiler_params=_cparams(("parallel", "parallel")),
        name="swa_prompt",
    )(sinks.astype(F32), p, p, p, p, p, p, p, ones_bd, qn, kn, b_cur, b_prev, b_meta)


def _out_ffn_kernel(h_ref, a_ref, b_ref, c_ref, as_ref, bs_ref, cs_ref, wo_ref, g_ref, wq_ref, kbd_ref,
                    h1_ref, xn_ref, st_ref):
    is_prompt = pl.program_id(0) < NP_ROWS // TO
    a = jnp.where(is_prompt, a_ref[...], as_ref[...])
    b = jnp.where(is_prompt, b_ref[...], bs_ref[...])
    c = jnp.where(is_prompt, c_ref[...], cs_ref[...])
    h1 = (h_ref[...]
          + jnp.dot(a.astype(BF16), wo_ref[0:512, :], preferred_element_type=F32)
          + jnp.dot(b.astype(BF16), wo_ref[512:1024, :], preferred_element_type=F32)
          + jnp.dot(c.astype(BF16), wo_ref[1024:1536, :], preferred_element_type=F32))
    h1_ref[...] = h1
    ms = jnp.mean(h1 * h1, axis=-1, keepdims=True)
    xn_f = h1 * lax.rsqrt(ms + EPS) * g_ref[...]
    xn_ref[...] = xn_f.T.astype(BF16)
    q = jnp.dot(xn_f.astype(BF16), wq_ref[...], preferred_element_type=F32).astype(BF16)
    kbd = kbd_ref[...]
    for hh in range(PEER_HEADS):
        st_ref[hh] = lax.dot_general(kbd, q[:, hh * 128:(hh + 1) * 128], (((1,), (1,)), ((), ())),
                                     preferred_element_type=F32)


def _out_ffn(h, mix_p, mix_s, wo_bf, g, wq_bf, kbd_bf):
    npb = NP_ROWS // TO
    pspec = pl.BlockSpec((TO, 512), lambda i: (jnp.minimum(i, npb - 1), 0))
    sspec = pl.BlockSpec((TO, 512), lambda i: (jnp.maximum(i - npb, 0), 0))
    return pl.pallas_call(
        _out_ffn_kernel,
        out_shape=(jax.ShapeDtypeStruct((N_ROWS, D_MODEL), F32),
                   jax.ShapeDtypeStruct((D_MODEL, N_ROWS), BF16),
                   jax.ShapeDtypeStruct((PEER_HEADS, 2 * N_KEYS, N_ROWS), F32)),
        grid=(N_ROWS // TO,),
        in_specs=[pl.BlockSpec((TO, D_MODEL), lambda i: (i, 0)),
                  pspec, pspec, pspec, sspec, sspec, sspec,
                  pl.BlockSpec((1536, D_MODEL), lambda i: (0, 0)),
                  pl.BlockSpec((1, D_MODEL), lambda i: (0, 0)),
                  pl.BlockSpec((D_MODEL, D_MODEL), lambda i: (0, 0)),
                  pl.BlockSpec((2 * N_KEYS, 128), lambda i: (0, 0))],
        out_specs=(pl.BlockSpec((TO, D_MODEL), lambda i: (i, 0)),
                   pl.BlockSpec((D_MODEL, TO), lambda i: (0, i)),
                   pl.BlockSpec((PEER_HEADS, 2 * N_KEYS, TO), lambda i: (0, 0, i))),
        compiler_params=_cparams(("parallel",)),
        name="out_ffn",
    )(h, *mix_p, *mix_s, wo_bf, g.reshape(1, -1), wq_bf, kbd_bf)


def _top_vals(x, n, with_rank=False):
    vals = []
    rank = jnp.full(x.shape, float(n), F32)
    for r in range(n):
        mx = jnp.max(x, axis=0, keepdims=True)
        vals.append(mx)
        hit = x == mx
        if with_rank:
            rank = jnp.where(hit, float(r), rank)
        x = jnp.where(hit, NEG, x)
    return (vals, rank) if with_rank else vals


def _topk_kernel(st_ref, rho_ref, c1_ref, r2_ref, e2_ref):
    s1 = st_ref[0, 0:N_KEYS, :]
    s2 = st_ref[0, N_KEYS:2 * N_KEYS, :]
    v1 = _top_vals(s1, PEER_TOPK)
    v2, rank2 = _top_vals(s2, PEER_TOPK, with_rank=True)
    sv1 = jnp.concatenate(v1, axis=0)
    sv2 = jnp.concatenate(v2, axis=0)
    cand = jnp.concatenate([v1[0] + sv2, v1[1] + sv2]
                           + [v1[a] + sv2[0:8] for a in range(2, 8)]
                           + [sv1[8:16] + v2[0]], axis=0)
    top = _top_vals(cand, PEER_TOPK)
    tau = top[PEER_TOPK - 1]
    z = jnp.ones_like(tau)
    for r in range(1, PEER_TOPK):
        z = z + jnp.exp(top[r] - top[0])
    rho = jnp.zeros(s1.shape, F32)
    for a in range(PEER_TOPK):
        n_a = jnp.sum(jnp.where(v1[a] + sv2 >= tau, 1.0, 0.0), axis=0, keepdims=True)
        rho = jnp.where(s1 == v1[a], n_a, rho)
    rho_ref[0] = rho
    c1_ref[0] = jnp.exp(s1 - v1[0]) / z
    r2_ref[0] = rank2.astype(BF16)
    e2_ref[0] = jnp.exp(s2 - v2[0]).astype(BF16)


def _topk(st):
    shp = jax.ShapeDtypeStruct((PEER_HEADS, N_KEYS, N_ROWS), F32)
    shp_bf = jax.ShapeDtypeStruct((PEER_HEADS, N_KEYS, N_ROWS), BF16)
    spec = pl.BlockSpec((1, N_KEYS, TM), lambda i, h: (h, 0, i))
    return pl.pallas_call(
        _topk_kernel,
        out_shape=(shp, shp, shp_bf, shp_bf),
        grid=(N_ROWS // TM, PEER_HEADS),
        in_specs=[pl.BlockSpec((1, 2 * N_KEYS, TM), lambda i, h: (h, 0, i))],
        out_specs=(spec, spec, spec, spec),
        compiler_params=_cparams(("parallel", "parallel")),
        name="peer_topk",
    )(st)


N_CHUNK = N_EXPERTS // EC
N_WORK = (N_ROWS // TT) * N_CHUNK


def _peer_stages(xn_ref, u_ref, vt_ref, rho_ref, c1_ref, r2_ref, e2_ref, acc_ref,
                 at_w, at_r, gt_w, gt_r):
    sub = 16
    n_tb = TT // PEER_LANES
    n_ii = EC // N_KEYS

    def gate_build(ii, tb, jbs):
        cols = slice(tb * PEER_LANES, (tb + 1) * PEER_LANES)
        w = {jb: jnp.zeros((sub, PEER_LANES), BF16) for jb in jbs}
        for hh in range(PEER_HEADS):
            rho = jnp.broadcast_to(rho_ref[hh, ii:ii + 1, cols], (sub, PEER_LANES)).astype(BF16)
            cc = jnp.broadcast_to(c1_ref[hh, ii:ii + 1, cols], (sub, PEER_LANES)).astype(BF16)
            for jb in jbs:
                jr = slice(jb * sub, (jb + 1) * sub)
                w[jb] = w[jb] + jnp.where(r2_ref[hh, jr, cols] < rho, e2_ref[hh, jr, cols] * cc, 0.0)
        for jb in jbs:
            rows = slice(ii * N_KEYS + jb * sub, ii * N_KEYS + (jb + 1) * sub)
            a = at_r[rows, cols]
            act = 0.5 * a * (1.0 + lax.erf(a * (2.0 ** -0.5)))
            gt_w[rows, cols] = w[jb] * act.astype(BF16)

    at_w[...] = jnp.dot(u_ref[...], xn_ref[...], preferred_element_type=F32)
    acc_ref[...] += jnp.dot(vt_ref[...], gt_r[...], preferred_element_type=F32)
    all_jb = range(N_KEYS // sub)
    for ii in range(n_ii):
        for tb in range(n_tb):
            gate_build(ii, tb, all_jb)


def _peer_kernel(xn_ref, h1_ref, u_ref, vt_ref, rho_ref, c1_ref, r2_ref, e2_ref, o_ref,
                 acc_ref, at, gt0, gt1):
    g = pl.program_id(0)
    cv = (g - 1) % N_CHUNK

    @pl.when(g == 0)
    def _():
        gt1[...] = jnp.zeros_like(gt1)

    @pl.when(jnp.logical_or(g < 1, cv == 0))
    def _():
        acc_ref[...] = jnp.zeros_like(acc_ref)

    args = (xn_ref, u_ref, vt_ref, rho_ref, c1_ref, r2_ref, e2_ref, acc_ref)

    @pl.when(g % 2 == 0)
    def _():
        _peer_stages(*args, at, at, gt0, gt1)

    @pl.when(g % 2 == 1)
    def _():
        _peer_stages(*args, at, at, gt1, gt0)

    @pl.when(jnp.logical_and(g >= 1, cv == N_CHUNK - 1))
    def _():
        o_ref[...] = h1_ref[...] + acc_ref[...].T


def _peer(xn, h1, u_bf, vt_bf, rho, c1, r2, e2):
    nk = EC // N_KEYS

    def item(g, lag):
        w = jnp.clip(g - lag, 0, N_WORK - 1)
        return w // N_CHUNK, w % N_CHUNK

    return pl.pallas_call(
        _peer_kernel,
        out_shape=jax.ShapeDtypeStruct((N_ROWS, D_MODEL), F32),
        grid=(N_WORK + 1,),
        in_specs=[pl.BlockSpec((D_MODEL, TT), lambda g: (0, item(g, 0)[0])),
                  pl.BlockSpec((TT, D_MODEL), lambda g: (item(g, 1)[0], 0)),
                  pl.BlockSpec((EC, D_MODEL), lambda g: (item(g, 0)[1], 0)),
                  pl.BlockSpec((D_MODEL, EC), lambda g: (0, item(g, 1)[1])),
                  pl.BlockSpec((PEER_HEADS, nk, TT), lambda g: (0, item(g, 0)[1], item(g, 0)[0])),
                  pl.BlockSpec((PEER_HEADS, nk, TT), lambda g: (0, item(g, 0)[1], item(g, 0)[0])),
                  pl.BlockSpec((PEER_HEADS, N_KEYS, TT), lambda g: (0, 0, item(g, 0)[0])),
                  pl.BlockSpec((PEER_HEADS, N_KEYS, TT), lambda g: (0, 0, item(g, 0)[0]))],
        out_specs=pl.BlockSpec((TT, D_MODEL), lambda g: (item(g, 1)[0], 0)),
        scratch_shapes=[pltpu.VMEM((D_MODEL, TT), F32), pltpu.VMEM((EC, TT), F32),
                        pltpu.VMEM((EC, TT), BF16), pltpu.VMEM((EC, TT), BF16)],
        compiler_params=_cparams(("arbitrary",)),
        name="peer_dense",
    )(xn, h1, u_bf, vt_bf, rho, c1, r2, e2)


S_STACK = SWA_GROUP * DEC_SEQ
S_SMALL = N_META + DEC_SEQ
NT_DIMS = (((1,), (1,)), ((), ()))


def _sample_kernel(p_ref, cs_ref, rs_ref, meta_ref, win_ref,
                   cw_ref, cb_ref, cg_ref, cbeta_ref,
                   cos_ref, sin_ref, dmat_ref, dq_ref, dk_ref, sgam_ref, gng_ref, gnb_ref, eye_ref,
                   ones_ref, qn_ref, kn_ref, bwin_ref, bsm_ref, sink_ref,
                   conv_o_ref, ret_o_ref, swa_o_ref, ncs_ref, nrs_ref, nwin_ref,
                   xin_ref, kpad_ref, vpad_ref, ksm_ref, vsm_ref):
    kpad_ref[...] = jnp.zeros_like(kpad_ref)
    vpad_ref[...] = jnp.zeros_like(vpad_ref)
    ksm_ref[...] = jnp.zeros_like(ksm_ref)
    vsm_ref[...] = jnp.zeros_like(vsm_ref)
    lane = lax.broadcasted_iota(jnp.int32, (DEC_SEQ, RET_HEADS * RET_DK), 1)
    first_half = (lane % RET_DK) < (RET_DK // 2)
    cos = cos_ref[...]
    sin = sin_ref[...]
    ones_q = ones_ref[...]
    ones_k = ones_ref[0:128, 0:128]

    for s in range(SG):
        rows = slice(s * DEC_SEQ, (s + 1) * DEC_SEQ)
        p = p_ref[rows, :]

        glu = p[:, C_CONV:C_CONV + CONV_CH] * jax.nn.sigmoid(p[:, C_CONV + CONV_CH:C_RQ])
        xin_ref[0:CONV_WIDTH - 1, :] = cs_ref[0, s]
        xin_ref[CONV_WIDTH - 1:CONV_WIDTH - 1 + DEC_SEQ, :] = glu
        acc = jnp.zeros((DEC_SEQ, CONV_CH), F32)
        for k in range(CONV_WIDTH):
            acc = acc + xin_ref[k:k + DEC_SEQ, :] * cw_ref[k:k + 1, :]
        y = acc + cb_ref[...]
        mu = jnp.mean(y, axis=-1, keepdims=True)
        d = y - mu
        var = jnp.mean(d * d, axis=-1, keepdims=True)
        yn = d * lax.rsqrt(var + EPS) * cg_ref[...] + cbeta_ref[...]
        conv_o_ref[rows, :] = yn * jax.nn.sigmoid(yn)
        ncs_ref[s] = xin_ref[DEC_SEQ:DEC_SEQ + CONV_WIDTH - 1, :]

        q = p[:, C_RQ:C_RK]
        k = p[:, C_RK:C_RV]
        q = q * cos + _swap_halves(q, first_half) * sin
        k = (k * cos + _swap_halves(k, first_half) * sin) * (RET_DK ** -0.5)
        rg = p[:, C_RG:C_SQ]
        kpad_ref[0:DEC_SEQ, :] = k
        vpad_ref[0:DEC_SEQ, :] = p[:, C_RV:C_RG]
        for h in range(RET_HEADS):
            qh = q[:, h * RET_DK:(h + 1) * RET_DK]
            kp = kpad_ref[:, h * RET_DK:(h + 1) * RET_DK]
            vp = vpad_ref[:, h * RET_DV:(h + 1) * RET_DV]
            s_old = rs_ref[0, s, h]
            inner = lax.dot_general(qh, kp, NT_DIMS, preferred_element_type=F32) * dmat_ref[h]
            o = (jnp.dot(inner, vp, preferred_element_type=F32)
                 + dq_ref[h] * jnp.dot(qh, s_old, preferred_element_type=F32))
            kdec_t = lax.dot_general(eye_ref[...], kp * dk_ref[h], NT_DIMS,
                                     preferred_element_type=F32)
            nrs_ref[s, h] = sgam_ref[h] * s_old + jnp.dot(kdec_t, vp, preferred_element_type=F32)
            mu = jnp.mean(o, axis=-1, keepdims=True)
            d = o - mu
            var = jnp.mean(d * d, axis=-1, keepdims=True)
            hs = slice(h * RET_DV, (h + 1) * RET_DV)
            yr = d * lax.rsqrt(var + EPS) * gng_ref[:, hs] + gnb_ref[:, hs]
            gate = rg[:, hs]
            ret_o_ref[rows, hs] = yr * (gate * jax.nn.sigmoid(gate))

        sq = p[:, C_SQ:C_SK]
        sk = p[:, C_SK:C_SV]
        qss = jnp.dot(sq * sq, ones_q, preferred_element_type=F32)
        kss = jnp.dot(sk * sk, ones_k, preferred_element_type=F32)
        qa = sq * lax.rsqrt(qss * (1.0 / SWA_HD) + EPS) * qn_ref[...] * (SWA_HD ** -0.5)
        kn = sk * lax.rsqrt(kss * (1.0 / SWA_HD) + EPS) * kn_ref[...]
        vn = p[:, C_SV:PROJ_COLS]
        ksm_ref[0:N_META, :] = meta_ref[0, s, :, 0:128]
        vsm_ref[0:N_META, :] = meta_ref[0, s, :, 128:256]
        ksm_ref[N_META:S_SMALL, :] = kn
        vsm_ref[N_META:S_SMALL, :] = vn
        for kv in range(SWA_KV_HEADS):
            ksl = slice(kv * SWA_HD, (kv + 1) * SWA_HD)
            vsl = slice(128 + kv * SWA_HD, 128 + (kv + 1) * SWA_HD)
            qs = jnp.concatenate([qa[:, (kv * SWA_GROUP + g) * SWA_HD:(kv * SWA_GROUP + g + 1) * SWA_HD]
                                  for g in range(SWA_GROUP)], axis=0)
            s_w = lax.dot_general(qs, win_ref[0, s, :, ksl], NT_DIMS, preferred_element_type=F32) + bwin_ref[kv]
            s_s = lax.dot_general(qs, ksm_ref[:, ksl], NT_DIMS, preferred_element_type=F32) + bsm_ref[kv]
            sink = sink_ref[kv, :, 0:1]
            m = jnp.maximum(jnp.maximum(jnp.max(s_w, axis=-1, keepdims=True),
                                        jnp.max(s_s, axis=-1, keepdims=True)), sink)
            p_w = jnp.exp(s_w - m)
            p_s = jnp.exp(s_s - m)
            den = (jnp.sum(p_w, axis=-1, keepdims=True) + jnp.sum(p_s, axis=-1, keepdims=True)
                   + jnp.exp(sink - m))
            o = (jnp.dot(p_w, win_ref[0, s, :, vsl], preferred_element_type=F32)
                 + jnp.dot(p_s, vsm_ref[:, ksl], preferred_element_type=F32)) / den
            for g in range(SWA_GROUP):
                hh = kv * SWA_GROUP + g
                swa_o_ref[rows, hh * SWA_HD:(hh + 1) * SWA_HD] = o[g * DEC_SEQ:(g + 1) * DEC_SEQ, :]
        nwin_ref[s, 0:WINDOW - DEC_SEQ, :] = win_ref[0, s, DEC_SEQ:WINDOW, :]
        nwin_ref[s, WINDOW - DEC_SEQ:WINDOW, 0:128] = kn
        nwin_ref[s, WINDOW - DEC_SEQ:WINDOW, 128:256] = vn


def _sample_tables(rel_bias):
    lg = jnp.log(1.0 - 2.0 ** (-5.0 - jnp.arange(RET_HEADS, dtype=F32)))
    i = jnp.arange(DEC_SEQ, dtype=F32)
    diff = i[:, None] - i[None, :]
    dm = jnp.where(diff >= 0, jnp.exp(jnp.maximum(diff, 0.0)[None] * lg[:, None, None]), 0.0)
    dmat = jnp.zeros((RET_HEADS, DEC_SEQ, 128), F32).at[:, :, :DEC_SEQ].set(dm)
    dq = jnp.broadcast_to(jnp.exp((i[None] + 1.0) * lg[:, None])[:, :, None], (RET_HEADS, DEC_SEQ, RET_DV))
    kd = jnp.exp((DEC_SEQ - 1.0 - i)[None] * lg[:, None])
    dk = jnp.zeros((RET_HEADS, 128, RET_DK), F32).at[:, :DEC_SEQ, :].set(
        jnp.broadcast_to(kd[:, :, None], (RET_HEADS, DEC_SEQ, RET_DK)))
    sg = jnp.broadcast_to(jnp.exp(DEC_SEQ * lg)[:, None, None], (RET_HEADS, 1, RET_DV))
    cos_t, sin_t = _rotary_tables(PAST_LEN + jnp.arange(DEC_SEQ))

    ti = np.arange(S_STACK)[:, None] % DEC_SEQ
    j = np.arange(WINDOW)[None, :]
    bk_win = _t5_bucket_np(np.maximum(ti + WINDOW - j, 0))
    ok_win = j > ti
    c = np.arange(128)[None, :]
    jn = c - N_META
    bk_new = _t5_bucket_np(np.clip(ti - jn, 0, None))
    ok_sm = (c < N_META) | ((c < S_SMALL) & (jn <= ti))
    bk_sm = np.where(c < N_META, N_BUCKETS - 1, bk_new)
    rb = rel_bias.astype(F32)
    head = np.arange(SWA_KV_HEADS)[:, None] * SWA_GROUP + (np.arange(S_STACK) // DEC_SEQ)[None, :]
    head_oh = _one_hot(head, SWA_HEADS)

    def look(bk, ok):
        b = jnp.einsum("rcb,bh,krh->krc", _one_hot(bk, N_BUCKETS), rb, head_oh,
                       precision=lax.Precision.HIGHEST)
        return jnp.where(jnp.asarray(np.broadcast_to(ok[None], b.shape)), b, NEG)

    return dict(dmat=dmat, dq=dq, dk=dk, sg=sg, cos=cos_t, sin=sin_t,
                bwin=look(bk_win, ok_win), bsm=look(bk_sm, ok_sm), head_oh=head_oh)


def _sample_mixers(p, l, cache_meta_kv, cache_swa_kv, state_ret, state_conv, tabs, conv_w32, conv_b,
                   conv_g, conv_beta, gn_g, gn_b, q_norm, k_norm, sinks):
    row0 = NP_ROWS // (SG * DEC_SEQ)
    nrow = SG * DEC_SEQ
    ones_bd = jnp.asarray(np.kron(np.eye(SWA_HEADS), np.ones((SWA_HD, SWA_HD))), F32)
    qn = jnp.tile(q_norm.astype(F32), SWA_HEADS).reshape(1, -1)
    kn = jnp.tile(k_norm.astype(F32), SWA_KV_HEADS).reshape(1, -1)
    sink_rows = jnp.einsum("h,krh->kr", sinks.astype(F32), tabs["head_oh"], precision=lax.Precision.HIGHEST)
    sink_t = jnp.broadcast_to(sink_rows[:, :, None], (SWA_KV_HEADS, S_STACK, 128))
    meta = cache_meta_kv.reshape(DEPTH, DEC_BATCH, N_META, 256)
    win = cache_swa_kv.reshape(DEPTH, DEC_BATCH, WINDOW, 256)

    def const(shape):
        return pl.BlockSpec(shape, lambda i: (0,) * len(shape))

    return pl.pallas_call(
        _sample_kernel,
        out_shape=(jax.ShapeDtypeStruct((NS_ROWS, CONV_CH), F32),
                   jax.ShapeDtypeStruct((NS_ROWS, RET_HEADS * RET_DV), F32),
                   jax.ShapeDtypeStruct((NS_ROWS, SWA_HEADS * SWA_HD), F32),
                   jax.ShapeDtypeStruct((DEC_BATCH, CONV_WIDTH - 1, CONV_CH), F32),
                   jax.ShapeDtypeStruct((DEC_BATCH, RET_HEADS, RET_DK, RET_DV), F32),
                   jax.ShapeDtypeStruct((DEC_BATCH, WINDOW, 256), F32)),
        grid=(DEC_BATCH // SG,),
        in_specs=[pl.BlockSpec((nrow, PROJ_COLS), lambda i: (row0 + i, 0)),
                  pl.BlockSpec((1, SG, CONV_WIDTH - 1, CONV_CH), lambda i: (l, i, 0, 0)),
                  pl.BlockSpec((1, SG, RET_HEADS, RET_DK, RET_DV), lambda i: (l, i, 0, 0, 0)),
                  pl.BlockSpec((1, SG, N_META, 256), lambda i: (l, i, 0, 0)),
                  pl.BlockSpec((1, SG, WINDOW, 256), lambda i: (l, i, 0, 0)),
                  const((32, CONV_CH)), const((1, CONV_CH)), const((1, CONV_CH)), const((1, CONV_CH)),
                  const((DEC_SEQ, 256)), const((DEC_SEQ, 256)),
                  const((RET_HEADS, DEC_SEQ, 128)), const((RET_HEADS, DEC_SEQ, RET_DV)),
                  const((RET_HEADS, 128, RET_DK)), const((RET_HEADS, 1, RET_DV)),
                  const((1, 512)), const((1, 512)), const((RET_DK, RET_DK)),
                  const((512, 512)), const((1, 512)), const((1, 128)),
                  const((SWA_KV_HEADS, S_STACK, 128)), const((SWA_KV_HEADS, S_STACK, 128)),
                  const((SWA_KV_HEADS, S_STACK, 128))],
        out_specs=(pl.BlockSpec((nrow, CONV_CH), lambda i: (i, 0)),
                   pl.BlockSpec((nrow, 512), lambda i: (i, 0)),
                   pl.BlockSpec((nrow, 512), lambda i: (i, 0)),
                   pl.BlockSpec((SG, CONV_WIDTH - 1, CONV_CH), lambda i: (i, 0, 0)),
                   pl.BlockSpec((SG, RET_HEADS, RET_DK, RET_DV), lambda i: (i, 0, 0, 0)),
                   pl.BlockSpec((SG, WINDOW, 256), lambda i: (i, 0, 0))),
        scratch_shapes=[pltpu.VMEM((40, CONV_CH), F32), pltpu.VMEM((128, RET_HEADS * RET_DK), F32),
                        pltpu.VMEM((128, RET_HEADS * RET_DV), F32),
                        pltpu.VMEM((128, 128), F32), pltpu.VMEM((128, 128), F32)],
        compiler_params=_cparams(("parallel",)),
        name="sample_mixers",
    )(p, state_conv, state_ret, meta, win,
      conv_w32, conv_b.reshape(1, -1), conv_g.reshape(1, -1), conv_beta.reshape(1, -1),
      tabs["cos"], tabs["sin"], tabs["dmat"], tabs["dq"], tabs["dk"], tabs["sg"],
      gn_g.reshape(1, -1), gn_b.reshape(1, -1), jnp.eye(RET_DK, dtype=F32),
      ones_bd, qn, kn, tabs["bwin"], tabs["bsm"], sink_t)


def kernel(x_prompt, x_sample, cache_meta_kv, cache_swa_kv, state_ret, state_conv, meta_tokens, rel_bias,
           norm_mix, w_in, conv_w, conv_b, conv_ln_g, conv_ln_b, ret_gn_g, ret_gn_b, swa_q_norm,
           swa_k_norm, swa_sinks, w_out, norm_ffn, peer_wq, peer_keys, peer_u, peer_v):
    meta = jnp.broadcast_to(meta_tokens.astype(F32)[None], (BATCH, N_META, D_MODEL))
    pad = jnp.zeros((BATCH, LP - L_REAL, D_MODEL), F32)
    hp = jnp.concatenate([meta, x_prompt, pad], axis=1).reshape(NP_ROWS, D_MODEL)
    h = jnp.concatenate([hp, x_sample.reshape(NS_ROWS, D_MODEL)], axis=0)

    cos_t, sin_t = _rotary_tables(jnp.arange(LP))
    ret_tabs = _ret_tables()
    bias_tabs = _bias_tables(rel_bias)
    sample_tabs = _sample_tables(rel_bias)

    meta_p, win_p, ret_p, conv_p, win_s, ret_s, conv_s = [], [], [], [], [], [], []
    for l in range(DEPTH):
        p = _norm_proj(h, norm_mix[l], w_in[l].astype(BF16))

        w32 = jnp.concatenate([conv_w[l], jnp.zeros((1, CONV_CH), F32)], axis=0)
        conv_o, conv_tail = _conv_prompt(p, w32, conv_b[l], conv_ln_g[l], conv_ln_b[l])
        ret_o, ret_state = _ret_prompt(p, cos_t, sin_t, ret_tabs, ret_gn_g[l], ret_gn_b[l])
        swa_o, k_normed = _swa_prompt(p, swa_sinks[l], swa_q_norm[l], swa_k_norm[l], bias_tabs)
        s_conv, s_ret, s_swa, new_conv, new_ret, new_win = _sample_mixers(
            p, l, cache_meta_kv, cache_swa_kv, state_ret, state_conv, sample_tabs, w32, conv_b[l],
            conv_ln_g[l], conv_ln_b[l], ret_gn_g[l], ret_gn_b[l], swa_q_norm[l], swa_k_norm[l], swa_sinks[l])
        new_win = new_win.reshape(DEC_BATCH, WINDOW, 2, SWA_KV_HEADS, SWA_HD)

        kbd = jnp.zeros((2 * N_KEYS, 2 * PEER_HALF), F32)
        kbd = kbd.at[:N_KEYS, :PEER_HALF].set(peer_keys[l, 0]).at[N_KEYS:, PEER_HALF:].set(peer_keys[l, 1])
        h1, xn, st = _out_ffn(h, (conv_o, ret_o, swa_o), (s_conv, s_ret, s_swa), w_out[l].astype(BF16),
                              norm_ffn[l], peer_wq[l].astype(BF16), kbd.astype(BF16))
        rho, c1, r2, e2 = _topk(st)
        h = _peer(xn, h1, peer_u[l].astype(BF16), peer_v[l].T.astype(BF16), rho, c1, r2, e2)

        kp = k_normed.reshape(BATCH, LP, SWA_KV_HEADS, SWA_HD)
        vp = p[:NP_ROWS, C_SV:].reshape(BATCH, LP, SWA_KV_HEADS, SWA_HD)
        kvp = jnp.stack([kp, vp], axis=2)
        meta_p.append(kvp[:, :N_META])
        win_p.append(kvp[:, L_REAL - WINDOW:L_REAL])
        ret_p.append(ret_state)
        conv_p.append(conv_tail[:, 32 - (CONV_WIDTH - 1):])
        win_s.append(new_win)
        ret_s.append(new_ret)
        conv_s.append(new_conv)

    y_prompt = h[:NP_ROWS].reshape(BATCH, LP, D_MODEL)[:, N_META:L_REAL]
    y_sample = h[NP_ROWS:].reshape(DEC_BATCH, DEC_SEQ, D_MODEL)
    return (y_prompt, y_sample, jnp.stack(meta_p), jnp.stack(win_p), jnp.stack(ret_p), jnp.stack(conv_p),
            jnp.stack(win_s), jnp.stack(ret_s), jnp.stack(conv_s))
```

```python
import functools
import math

import numpy as np
import jax
import jax.numpy as jnp
from jax import lax
from jax.experimental import pallas as pl
from jax.experimental.pallas import tpu as pltpu

F32 = jnp.float32
BF16 = jnp.bfloat16

D_MODEL = 1024
BATCH = 2
SEQ = 8192
DEPTH = 2
DEC_BATCH = 128
DEC_SEQ = 8
PAST_LEN = 8192
N_META = 16
CONV_CH = 512
CONV_WIDTH = 31
RET_HEADS = 4
RET_DK = 64
RET_DV = 128
SWA_HEADS = 8
SWA_KV_HEADS = 2
SWA_GROUP = SWA_HEADS // SWA_KV_HEADS
SWA_HD = 64
WINDOW = 128
N_BUCKETS = 32
REL_MAX_DIST = 128
PEER_HEADS = 8
N_KEYS = 128
N_EXPERTS = N_KEYS * N_KEYS
PEER_TOPK = 16
PEER_HALF = 64
EPS = 1e-6
NEG = -1e30

PROJ_COLS = 3328
C_CONV, C_RQ, C_RK, C_RV, C_RG, C_SQ, C_SK, C_SV = 0, 1024, 1280, 1536, 2048, 2560, 3072, 3200

L_REAL = N_META + SEQ
BLK = 128
N_BLK = 65
LP = N_BLK * BLK
NP_ROWS = BATCH * LP
NS_ROWS = DEC_BATCH * DEC_SEQ
N_ROWS = NP_ROWS + NS_ROWS
LAST_REAL = L_REAL - (N_BLK - 1) * BLK

TM = 384
TO = 256
TT = 768
SG = 8
EC = 1024
PEER_LANES = 256
CONV_T = 640
CONV_RB = 64
VMEM_LIMIT = 56 * 1024 * 1024


def _cparams(sem, flags=None):
    return pltpu.CompilerParams(dimension_semantics=sem, vmem_limit_bytes=VMEM_LIMIT, flags=flags)


def _norm_proj_kernel(x_ref, g_ref, w_ref, o_ref):
    x = x_ref[...]
    ms = jnp.mean(x * x, axis=-1, keepdims=True)
    xn = x * lax.rsqrt(ms + EPS) * g_ref[...]
    o_ref[...] = jnp.dot(xn.astype(BF16), w_ref[...], preferred_element_type=F32)


def _norm_proj(h, g, w_bf):
    return pl.pallas_call(
        _norm_proj_kernel,
        out_shape=jax.ShapeDtypeStruct((N_ROWS, PROJ_COLS), F32),
        grid=(N_ROWS // TM,),
        in_specs=[pl.BlockSpec((TM, D_MODEL), lambda i: (i, 0)),
                  pl.BlockSpec((1, D_MODEL), lambda i: (0, 0)),
                  pl.BlockSpec((D_MODEL, PROJ_COLS), lambda i: (0, 0))],
        out_specs=pl.BlockSpec((TM, PROJ_COLS), lambda i: (i, 0)),
        compiler_params=_cparams(("parallel",)),
        name="norm_proj",
    )(h, g.reshape(1, D_MODEL), w_bf)


def _conv_kernel(c_ref, w_ref, b_ref, g_ref, beta_ref, o_ref, st_ref, xin_ref, xs_ref):
    t = pl.program_id(1)

    @pl.when(t == 0)
    def _():
        xin_ref[0:32, :] = jnp.zeros((32, CONV_CH), F32)

    @pl.when(t > 0)
    def _():
        xin_ref[0:32, :] = xin_ref[CONV_T:CONV_T + 32, :]

    c = c_ref[...]
    xin_ref[32:32 + CONV_T, :] = c[:, :CONV_CH] * jax.nn.sigmoid(c[:, CONV_CH:])
    n_sh = 32 + CONV_T - 8
    for r in range(1, 8):
        xs_ref[r - 1, 0:n_sh, :] = xin_ref[r:r + n_sh, :]

    w = w_ref[...]
    bias = b_ref[...]
    gam = g_ref[...]
    beta = beta_ref[...]
    for rb in range(CONV_T // CONV_RB):
        r0 = rb * CONV_RB
        acc = jnp.zeros((CONV_RB, CONV_CH), F32)
        for k in range(CONV_WIDTH):
            sh = (2 + k) % 8
            lo = r0 + 2 + k - sh
            win = xin_ref[lo:lo + CONV_RB, :] if sh == 0 else xs_ref[sh - 1, lo:lo + CONV_RB, :]
            acc = acc + win * w[k:k + 1, :]
        y = acc + bias
        mu = jnp.mean(y, axis=-1, keepdims=True)
        d = y - mu
        var = jnp.mean(d * d, axis=-1, keepdims=True)
        yn = d * lax.rsqrt(var + EPS) * gam + beta
        o_ref[r0:r0 + CONV_RB, :] = yn * jax.nn.sigmoid(yn)

    @pl.when(t == pl.num_programs(1) - 1)
    def _():
        lo = 32 + (L_REAL - 32) - (LP - CONV_T)
        st_ref[0] = xin_ref[lo:lo + 32, :]


def _conv_prompt(p, w32, b, g, beta):
    nt = LP // CONV_T
    return pl.pallas_call(
        _conv_kernel,
        out_shape=(jax.ShapeDtypeStruct((NP_ROWS, CONV_CH), F32),
                   jax.ShapeDtypeStruct((BATCH, 32, CONV_CH), F32)),
        grid=(BATCH, nt),
        in_specs=[pl.BlockSpec((CONV_T, 2 * CONV_CH), lambda bi, t: (bi * nt + t, 0)),
                  pl.BlockSpec((32, CONV_CH), lambda bi, t: (0, 0)),
                  pl.BlockSpec((1, CONV_CH), lambda bi, t: (0, 0)),
                  pl.BlockSpec((1, CONV_CH), lambda bi, t: (0, 0)),
                  pl.BlockSpec((1, CONV_CH), lambda bi, t: (0, 0))],
        out_specs=(pl.BlockSpec((CONV_T, CONV_CH), lambda bi, t: (bi * nt + t, 0)),
                   pl.BlockSpec((1, 32, CONV_CH), lambda bi, t: (bi, 0, 0))),
        scratch_shapes=[pltpu.VMEM((32 + CONV_T, CONV_CH), F32),
                        pltpu.VMEM((7, 32 + CONV_T - 8, CONV_CH), F32)],
        compiler_params=_cparams(("arbitrary", "arbitrary")),
        name="conv_prompt",
    )(p, w32, b.reshape(1, -1), g.reshape(1, -1), beta.reshape(1, -1))


def _swap_halves(x, first_half):
    return jnp.where(first_half, pltpu.roll(x, x.shape[1] - 32, axis=1), pltpu.roll(x, 32, axis=1))


def _ret_kernel(q_ref, k_ref, v_ref, rg_ref, cos_ref, sin_ref, dmat_ref, dq_ref, dk_ref, sg_ref,
                gng_ref, gnb_ref, o_ref, st_ref, s_ref):
    j = pl.program_id(1)

    @pl.when(j == 0)
    def _():
        s_ref[...] = jnp.zeros_like(s_ref)

    cos = cos_ref[...]
    sin = sin_ref[...]
    lane = lax.broadcasted_iota(jnp.int32, (BLK, RET_HEADS * RET_DK), 1)
    first_half = (lane % RET_DK) < (RET_DK // 2)
    q = q_ref[...]
    k = k_ref[...]
    q = q * cos + _swap_halves(q, first_half) * sin
    k = (k * cos + _swap_halves(k, first_half) * sin) * (RET_DK ** -0.5)
    v = v_ref[...]
    rg = rg_ref[...]
    for h in range(RET_HEADS):
        qh = q[:, h * RET_DK:(h + 1) * RET_DK].astype(BF16)
        kh = k[:, h * RET_DK:(h + 1) * RET_DK]
        vh = v[:, h * RET_DV:(h + 1) * RET_DV].astype(BF16)
        s_old = s_ref[h]
        inner = lax.dot_general(qh, kh.astype(BF16), (((1,), (1,)), ((), ())),
                                preferred_element_type=F32) * dmat_ref[h]
        o = (jnp.dot(inner.astype(BF16), vh, preferred_element_type=F32)
             + dq_ref[h] * jnp.dot(qh, s_old.astype(BF16), preferred_element_type=F32))
        kdec_t = (kh * dk_ref[0, h]).T.astype(BF16)
        s_new = sg_ref[0, h, 0:1, :] * s_old + jnp.dot(kdec_t, vh, preferred_element_type=F32)
        s_ref[h] = s_new
        mu = jnp.mean(o, axis=-1, keepdims=True)
        d = o - mu
        var = jnp.mean(d * d, axis=-1, keepdims=True)
        y = d * lax.rsqrt(var + EPS) * gng_ref[:, h * RET_DV:(h + 1) * RET_DV] \
            + gnb_ref[:, h * RET_DV:(h + 1) * RET_DV]
        gate = rg[:, h * RET_DV:(h + 1) * RET_DV]
        o_ref[:, h * RET_DV:(h + 1) * RET_DV] = y * (gate * jax.nn.sigmoid(gate))

    @pl.when(j == pl.num_programs(1) - 1)
    def _():
        st_ref[0] = s_ref[...]


def _ret_tables():
    lg = jnp.log(1.0 - 2.0 ** (-5.0 - jnp.arange(RET_HEADS, dtype=F32)))
    i = jnp.arange(BLK, dtype=F32)
    diff = i[:, None] - i[None, :]
    dmat = jnp.where(diff >= 0, jnp.exp(jnp.maximum(diff, 0.0)[None] * lg[:, None, None]), 0.0)
    dq = jnp.broadcast_to(jnp.exp((i[None] + 1.0) * lg[:, None])[:, :, None], (RET_HEADS, BLK, RET_DV))

    def kdec(c_eff):
        e = jnp.where(i[None] < c_eff, jnp.exp((c_eff - 1.0 - i)[None] * lg[:, None]), 0.0)
        return jnp.broadcast_to(e[:, :, None], (RET_HEADS, BLK, RET_DK))

    def sgam(c_eff):
        return jnp.broadcast_to(jnp.exp(c_eff * lg)[:, None, None], (RET_HEADS, 8, RET_DV))

    dk = jnp.stack([kdec(float(BLK)), kdec(float(LAST_REAL))])
    sg = jnp.stack([sgam(float(BLK)), sgam(float(LAST_REAL))])
    return dmat, dq, dk, sg


def _rotary_tables(pos):
    half = RET_DK // 2
    inv = 1.0 / (10000.0 ** (jnp.arange(half, dtype=F32) / half))
    ang = pos.astype(F32)[:, None] * inv[None]
    cos, sin = jnp.cos(ang), jnp.sin(ang)
    cos_t = jnp.tile(jnp.concatenate([cos, cos], axis=-1), (1, RET_HEADS))
    sin_t = jnp.tile(jnp.concatenate([-sin, sin], axis=-1), (1, RET_HEADS))
    return cos_t, sin_t


def _ret_prompt(p, cos_t, sin_t, tabs, gn_g, gn_b):
    dmat, dq, dk, sg = tabs
    nb = N_BLK
    last = nb - 1
    return pl.pallas_call(
        _ret_kernel,
        out_shape=(jax.ShapeDtypeStruct((NP_ROWS, RET_HEADS * RET_DV), F32),
                   jax.ShapeDtypeStruct((BATCH, RET_HEADS, RET_DK, RET_DV), F32)),
        grid=(BATCH, nb),
        in_specs=[pl.BlockSpec((BLK, 256), lambda b, j: (b * nb + j, C_RQ // 256)),
                  pl.BlockSpec((BLK, 256), lambda b, j: (b * nb + j, C_RK // 256)),
                  pl.BlockSpec((BLK, 512), lambda b, j: (b * nb + j, C_RV // 512)),
                  pl.BlockSpec((BLK, 512), lambda b, j: (b * nb + j, C_RG // 512)),
                  pl.BlockSpec((BLK, 256), lambda b, j: (j, 0)),
                  pl.BlockSpec((BLK, 256), lambda b, j: (j, 0)),
                  pl.BlockSpec((RET_HEADS, BLK, BLK), lambda b, j: (0, 0, 0)),
                  pl.BlockSpec((RET_HEADS, BLK, RET_DV), lambda b, j: (0, 0, 0)),
                  pl.BlockSpec((1, RET_HEADS, BLK, RET_DK), lambda b, j: (j // last, 0, 0, 0)),
                  pl.BlockSpec((1, RET_HEADS, 8, RET_DV), lambda b, j: (j // last, 0, 0, 0)),
                  pl.BlockSpec((1, 512), lambda b, j: (0, 0)),
                  pl.BlockSpec((1, 512), lambda b, j: (0, 0))],
        out_specs=(pl.BlockSpec((BLK, 512), lambda b, j: (b * nb + j, 0)),
                   pl.BlockSpec((1, RET_HEADS, RET_DK, RET_DV), lambda b, j: (b, 0, 0, 0))),
        scratch_shapes=[pltpu.VMEM((RET_HEADS, RET_DK, RET_DV), F32)],
        compiler_params=_cparams(("arbitrary", "arbitrary")),
        name="ret_prompt",
    )(p, p, p, p, cos_t, sin_t, dmat, dq, dk, sg, gn_g.reshape(1, -1), gn_b.reshape(1, -1))


def _group_rms(x, ones_bd, w):
    x2 = x * x
    hi = x2.astype(BF16)
    lo = (x2 - hi.astype(F32)).astype(BF16)
    ss = (jnp.dot(hi, ones_bd, preferred_element_type=F32)
          + jnp.dot(lo, ones_bd, preferred_element_type=F32))
    return x * lax.rsqrt(ss * (1.0 / SWA_HD) + EPS) * w


def _swa_kernel(sinks_ref, q_ref, kc_ref, vc_ref, kp_ref, vp_ref, km_ref, vm_ref, ones_ref,
                qn_ref, kn_ref, bc_ref, bp_ref, bm_ref, o_ref, kout_ref):
    j = pl.program_id(1)
    ones_q = ones_ref[...]
    ones_k = ones_ref[0:128, 0:128]
    qw = qn_ref[...]
    kw = kn_ref[...]
    q = _group_rms(q_ref[...], ones_q, qw) * (SWA_HD ** -0.5)
    kc = _group_rms(kc_ref[...], ones_k, kw)
    kp = _group_rms(kp_ref[...], ones_k, kw)
    km = _group_rms(km_ref[...], ones_k, kw)
    kout_ref[...] = kc
    vc = vc_ref[...].astype(BF16)
    vp = vp_ref[...].astype(BF16)
    vm = vm_ref[...].astype(BF16)

    rows = SWA_GROUP * BLK
    qi = lax.broadcasted_iota(jnp.int32, (rows, BLK), 0) % BLK
    kj = lax.broadcasted_iota(jnp.int32, (rows, BLK), 1)
    valid_c = jnp.where(kj <= qi, j * BLK + kj, -1) >= N_META
    valid_p = jnp.where(kj > qi, (j - 1) * BLK + kj, -1) >= N_META
    qm = lax.broadcasted_iota(jnp.int32, (rows, N_META), 0) % BLK
    mm = lax.broadcasted_iota(jnp.int32, (rows, N_META), 1)
    valid_m = mm <= j * BLK + qm

    dn = (((1,), (1,)), ((), ()))
    for kv in range(SWA_KV_HEADS):
        heads = range(kv * SWA_GROUP, (kv + 1) * SWA_GROUP)
        qs = jnp.concatenate([q[:, h * SWA_HD:(h + 1) * SWA_HD] for h in heads], axis=0).astype(BF16)
        ksl = slice(kv * SWA_HD, (kv + 1) * SWA_HD)
        s_c = lax.dot_general(qs, kc[:, ksl].astype(BF16), dn, preferred_element_type=F32)
        s_p = lax.dot_general(qs, kp[:, ksl].astype(BF16), dn, preferred_element_type=F32)
        s_m = lax.dot_general(qs, km[:, ksl].astype(BF16), dn, preferred_element_type=F32)
        s_c = jnp.where(valid_c, s_c + bc_ref[kv], NEG)
        s_p = jnp.where(valid_p, s_p + bp_ref[kv], NEG)
        s_m = jnp.where(valid_m, s_m + bm_ref[0, kv], NEG)
        sink = jnp.concatenate([jnp.full((BLK, 1), sinks_ref[h], F32) for h in heads], axis=0)
        m = jnp.maximum(jnp.maximum(jnp.max(s_c, axis=-1, keepdims=True),
                                    jnp.max(s_p, axis=-1, keepdims=True)),
                        jnp.maximum(jnp.max(s_m, axis=-1, keepdims=True), sink))
        p_c = jnp.exp(s_c - m)
        p_p = jnp.exp(s_p - m)
        p_m = jnp.exp(s_m - m)
        den = (jnp.sum(p_c, axis=-1, keepdims=True) + jnp.sum(p_p, axis=-1, keepdims=True)
               + jnp.sum(p_m, axis=-1, keepdims=True) + jnp.exp(sink - m))
        acc = (jnp.dot(p_c.astype(BF16), vc[:, ksl], preferred_element_type=F32)
               + jnp.dot(p_p.astype(BF16), vp[:, ksl], preferred_element_type=F32)
               + jnp.dot(p_m.astype(BF16), vm[:, ksl], preferred_element_type=F32))
        out = acc / den
        for g, h in enumerate(heads):
            o_ref[:, h * SWA_HD:(h + 1) * SWA_HD] = out[g * BLK:(g + 1) * BLK, :]


def _one_hot(idx, n):
    return jnp.asarray(np.asarray(idx)[..., None] == np.arange(n), F32)


def _t5_bucket_np(dist):
    max_exact = N_BUCKETS // 2
    df = np.maximum(dist, 1).astype(np.float64)
    large = max_exact + (np.log(df / max_exact) / math.log(REL_MAX_DIST / max_exact)
                         * (N_BUCKETS - max_exact)).astype(np.int64)
    large = np.minimum(large, N_BUCKETS - 1)
    return np.where(dist < max_exact, dist, large).astype(np.int32)


def _bias_tables(rel_bias):
    qi = np.arange(BLK)[:, None]
    kj = np.arange(BLK)[None, :]
    b_cur = _t5_bucket_np(np.maximum(qi - kj, 0))
    b_prev = _t5_bucket_np(np.maximum(qi - kj + BLK, 0))
    m = np.arange(N_META)[None, :]
    b_m0 = _t5_bucket_np(np.maximum(qi - m, 0))
    b_m1 = _t5_bucket_np(np.maximum(qi + BLK - m, 0))
    assert (b_m1 == N_BUCKETS - 1).all()
    rb = rel_bias.astype(F32)

    def look(bk):
        return jnp.einsum("...b,bh->h...", _one_hot(bk, N_BUCKETS), rb, precision=lax.Precision.HIGHEST)

    return look(b_cur), look(b_prev), jnp.stack([look(b_m0), look(b_m1)])


def _swa_prompt(p, sinks, q_norm, k_norm, bias_tabs):
    srows = SWA_GROUP * BLK
    b_cur = bias_tabs[0].reshape(SWA_KV_HEADS, srows, BLK)
    b_prev = bias_tabs[1].reshape(SWA_KV_HEADS, srows, BLK)
    b_meta = bias_tabs[2].reshape(2, SWA_KV_HEADS, srows, N_META)
    nb = N_BLK
    ones_bd = jnp.asarray(np.kron(np.eye(SWA_HEADS), np.ones((SWA_HD, SWA_HD))), BF16)
    qn = jnp.tile(q_norm.astype(F32), SWA_HEADS).reshape(1, -1)
    kn = jnp.tile(k_norm.astype(F32), SWA_KV_HEADS).reshape(1, -1)
    ck, cv = C_SK // 128, C_SV // 128
    return pl.pallas_call(
        _swa_kernel,
        out_shape=(jax.ShapeDtypeStruct((NP_ROWS, SWA_HEADS * SWA_HD), F32),
                   jax.ShapeDtypeStruct((NP_ROWS, SWA_KV_HEADS * SWA_HD), F32)),
        grid=(BATCH, nb),
        in_specs=[pl.BlockSpec(memory_space=pltpu.SMEM),
                  pl.BlockSpec((BLK, 512), lambda b, j: (b * nb + j, C_SQ // 512)),
                  pl.BlockSpec((BLK, 128), lambda b, j: (b * nb + j, ck)),
                  pl.BlockSpec((BLK, 128), lambda b, j: (b * nb + j, cv)),
                  pl.BlockSpec((BLK, 128), lambda b, j: (b * nb + jnp.maximum(j - 1, 0), ck)),
                  pl.BlockSpec((BLK, 128), lambda b, j: (b * nb + jnp.maximum(j - 1, 0), cv)),
                  pl.BlockSpec((N_META, 128), lambda b, j: (b * (LP // N_META), ck)),
                  pl.BlockSpec((N_META, 128), lambda b, j: (b * (LP // N_META), cv)),
                  pl.BlockSpec((512, 512), lambda b, j: (0, 0)),
                  pl.BlockSpec((1, 512), lambda b, j: (0, 0)),
                  pl.BlockSpec((1, 128), lambda b, j: (0, 0)),
                  pl.BlockSpec((SWA_KV_HEADS, srows, BLK), lambda b, j: (0, 0, 0)),
                  pl.BlockSpec((SWA_KV_HEADS, srows, BLK), lambda b, j: (0, 0, 0)),
                  pl.BlockSpec((1, SWA_KV_HEADS, srows, N_META), lambda b, j: (jnp.minimum(j, 1), 0, 0, 0))],
        out_specs=(pl.BlockSpec((BLK, 512), lambda b, j: (b * nb + j, 0)),
                   pl.BlockSpec((BLK, 128), lambda b, j: (b * nb + j, 0))),
        compiler_params=_cparams(("parallel", "parallel")),
        name="swa_prompt",
    )(sinks.astype(F32), p, p, p, p, p, p, p, ones_bd, qn, kn, b_cur, b_prev, b_meta)


def _out_ffn_kernel(h_ref, a_ref, b_ref, c_ref, as_ref, bs_ref, cs_ref, wo_ref, g_ref, wq_ref, kbd_ref,
                    h1_ref, xn_ref, st_ref):
    is_prompt = pl.program_id(0) < NP_ROWS // TO
    a = jnp.where(is_prompt, a_ref[...], as_ref[...])
    b = jnp.where(is_prompt, b_ref[...], bs_ref[...])
    c = jnp.where(is_prompt, c_ref[...], cs_ref[...])
    h1 = (h_ref[...]
          + jnp.dot(a.astype(BF16), wo_ref[0:512, :], preferred_element_type=F32)
          + jnp.dot(b.astype(BF16), wo_ref[512:1024, :], preferred_element_type=F32)
          + jnp.dot(c.astype(BF16), wo_ref[1024:1536, :], preferred_element_type=F32))
    h1_ref[...] = h1
    ms = jnp.mean(h1 * h1, axis=-1, keepdims=True)
    xn_f = h1 * lax.rsqrt(ms + EPS) * g_ref[...]
    xn_ref[...] = xn_f.T.astype(BF16)
    q = jnp.dot(xn_f.astype(BF16), wq_ref[...], preferred_element_type=F32).astype(BF16)
    kbd = kbd_ref[...]
    for hh in range(PEER_HEADS):
        st_ref[hh] = lax.dot_general(kbd, q[:, hh * 128:(hh + 1) * 128], (((1,), (1,)), ((), ())),
                                     preferred_element_type=F32)


def _out_ffn(h, mix_p, mix_s, wo_bf, g, wq_bf, kbd_bf):
    npb = NP_ROWS // TO
    pspec = pl.BlockSpec((TO, 512), lambda i: (jnp.minimum(i, npb - 1), 0))
    sspec = pl.BlockSpec((TO, 512), lambda i: (jnp.maximum(i - npb, 0), 0))
    return pl.pallas_call(
        _out_ffn_kernel,
        out_shape=(jax.ShapeDtypeStruct((N_ROWS, D_MODEL), F32),
                   jax.ShapeDtypeStruct((D_MODEL, N_ROWS), BF16),
                   jax.ShapeDtypeStruct((PEER_HEADS, 2 * N_KEYS, N_ROWS), F32)),
        grid=(N_ROWS // TO,),
        in_specs=[pl.BlockSpec((TO, D_MODEL), lambda i: (i, 0)),
                  pspec, pspec, pspec, sspec, sspec, sspec,
                  pl.BlockSpec((1536, D_MODEL), lambda i: (0, 0)),
                  pl.BlockSpec((1, D_MODEL), lambda i: (0, 0)),
                  pl.BlockSpec((D_MODEL, D_MODEL), lambda i: (0, 0)),
                  pl.BlockSpec((2 * N_KEYS, 128), lambda i: (0, 0))],
        out_specs=(pl.BlockSpec((TO, D_MODEL), lambda i: (i, 0)),
                   pl.BlockSpec((D_MODEL, TO), lambda i: (0, i)),
                   pl.BlockSpec((PEER_HEADS, 2 * N_KEYS, TO), lambda i: (0, 0, i))),
        compiler_params=_cparams(("parallel",)),
        name="out_ffn",
    )(h, *mix_p, *mix_s, wo_bf, g.reshape(1, -1), wq_bf, kbd_bf)


def _top_vals(x, n, with_rank=False):
    vals = []
    rank = jnp.full(x.shape, float(n), F32)
    for r in range(n):
        mx = jnp.max(x, axis=0, keepdims=True)
        vals.append(mx)
        hit = x == mx
        if with_rank:
            rank = jnp.where(hit, float(r), rank)
        x = jnp.where(hit, NEG, x)
    return (vals, rank) if with_rank else vals


def _topk_kernel(st_ref, rho_ref, c1_ref, r2_ref, e2_ref):
    s1 = st_ref[0, 0:N_KEYS, :]
    s2 = st_ref[0, N_KEYS:2 * N_KEYS, :]
    v1 = _top_vals(s1, PEER_TOPK)
    v2, rank2 = _top_vals(s2, PEER_TOPK, with_rank=True)
    sv1 = jnp.concatenate(v1, axis=0)
    sv2 = jnp.concatenate(v2, axis=0)
    cand = jnp.concatenate([v1[0] + sv2, v1[1] + sv2]
                           + [v1[a] + sv2[0:8] for a in range(2, 8)]
                           + [sv1[8:16] + v2[0]], axis=0)
    top = _top_vals(cand, PEER_TOPK)
    tau = top[PEER_TOPK - 1]
    z = jnp.ones_like(tau)
    for r in range(1, PEER_TOPK):
        z = z + jnp.exp(top[r] - top[0])
    rho = jnp.zeros(s1.shape, F32)
    for a in range(PEER_TOPK):
        n_a = jnp.sum(jnp.where(v1[a] + sv2 >= tau, 1.0, 0.0), axis=0, keepdims=True)
        rho = jnp.where(s1 == v1[a], n_a, rho)
    rho_ref[0] = rho
    c1_ref[0] = jnp.exp(s1 - v1[0]) / z
    r2_ref[0] = rank2.astype(BF16)
    e2_ref[0] = jnp.exp(s2 - v2[0]).astype(BF16)


def _topk(st):
    shp = jax.ShapeDtypeStruct((PEER_HEADS, N_KEYS, N_ROWS), F32)
    shp_bf = jax.ShapeDtypeStruct((PEER_HEADS, N_KEYS, N_ROWS), BF16)
    spec = pl.BlockSpec((1, N_KEYS, TM), lambda i, h: (h, 0, i))
    return pl.pallas_call(
        _topk_kernel,
        out_shape=(shp, shp, shp_bf, shp_bf),
        grid=(N_ROWS // TM, PEER_HEADS),
        in_specs=[pl.BlockSpec((1, 2 * N_KEYS, TM), lambda i, h: (h, 0, i))],
        out_specs=(spec, spec, spec, spec),
        compiler_params=_cparams(("parallel", "parallel")),
        name="peer_topk",
    )(st)


N_CHUNK = N_EXPERTS // EC
N_WORK = (N_ROWS // TT) * N_CHUNK


def _peer_stages(xn_ref, u_ref, vt_ref, rho_ref, c1_ref, r2_ref, e2_ref, acc_ref,
                 at_w, at_r, gt_w, gt_r):
    sub = 16
    n_tb = TT // PEER_LANES
    n_ii = EC // N_KEYS

    def gate_build(ii, tb, jbs):
        cols = slice(tb * PEER_LANES, (tb + 1) * PEER_LANES)
        w = {jb: jnp.zeros((sub, PEER_LANES), BF16) for jb in jbs}
        for hh in range(PEER_HEADS):
            rho = jnp.broadcast_to(rho_ref[hh, ii:ii + 1, cols], (sub, PEER_LANES)).astype(BF16)
            cc = jnp.broadcast_to(c1_ref[hh, ii:ii + 1, cols], (sub, PEER_LANES)).astype(BF16)
            for jb in jbs:
                jr = slice(jb * sub, (jb + 1) * sub)
                w[jb] = w[jb] + jnp.where(r2_ref[hh, jr, cols] < rho, e2_ref[hh, jr, cols] * cc, 0.0)
        for jb in jbs:
            rows = slice(ii * N_KEYS + jb * sub, ii * N_KEYS + (jb + 1) * sub)
            a = at_r[rows, cols]
            act = 0.5 * a * (1.0 + lax.erf(a * (2.0 ** -0.5)))
            gt_w[rows, cols] = w[jb] * act.astype(BF16)

    at_w[...] = jnp.dot(u_ref[...], xn_ref[...], preferred_element_type=F32)
    acc_ref[...] += jnp.dot(vt_ref[...], gt_r[...], preferred_element_type=F32)
    all_jb = range(N_KEYS // sub)
    for ii in range(n_ii):
        for tb in range(n_tb):
            gate_build(ii, tb, all_jb)


def _peer_kernel(xn_ref, h1_ref, u_ref, vt_ref, rho_ref, c1_ref, r2_ref, e2_ref, o_ref,
                 acc_ref, at, gt):
    g = pl.program_id(0)
    cv = (g - 1) % N_CHUNK
    slot = g % 2

    @pl.when(g == 0)
    def _():
        gt[1] = jnp.zeros((EC, TT), BF16)

    @pl.when(jnp.logical_or(g < 1, cv == 0))
    def _():
        acc_ref[...] = jnp.zeros_like(acc_ref)

    _peer_stages(xn_ref, u_ref, vt_ref, rho_ref, c1_ref, r2_ref, e2_ref, acc_ref,
                 at, at, gt.at[slot], gt.at[1 - slot])

    @pl.when(jnp.logical_and(g >= 1, cv == N_CHUNK - 1))
    def _():
        o_ref[...] = h1_ref[...] + acc_ref[...].T


def _peer(xn, h1, u_bf, vt_bf, rho, c1, r2, e2):
    nk = EC // N_KEYS

    def item(g, lag):
        w = jnp.clip(g - lag, 0, N_WORK - 1)
        return w // N_CHUNK, w % N_CHUNK

    return pl.pallas_call(
        _peer_kernel,
        out_shape=jax.ShapeDtypeStruct((N_ROWS, D_MODEL), F32),
        grid=(N_WORK + 1,),
        in_specs=[pl.BlockSpec((D_MODEL, TT), lambda g: (0, item(g, 0)[0])),
                  pl.BlockSpec((TT, D_MODEL), lambda g: (item(g, 1)[0], 0)),
                  pl.BlockSpec((EC, D_MODEL), lambda g: (item(g, 0)[1], 0)),
                  pl.BlockSpec((D_MODEL, EC), lambda g: (0, item(g, 1)[1])),
                  pl.BlockSpec((PEER_HEADS, nk, TT), lambda g: (0, item(g, 0)[1], item(g, 0)[0])),
                  pl.BlockSpec((PEER_HEADS, nk, TT), lambda g: (0, item(g, 0)[1], item(g, 0)[0])),
                  pl.BlockSpec((PEER_HEADS, N_KEYS, TT), lambda g: (0, 0, item(g, 0)[0])),
                  pl.BlockSpec((PEER_HEADS, N_KEYS, TT), lambda g: (0, 0, item(g, 0)[0]))],
        out_specs=pl.BlockSpec((TT, D_MODEL), lambda g: (item(g, 1)[0], 0)),
        scratch_shapes=[pltpu.VMEM((D_MODEL, TT), F32), pltpu.VMEM((EC, TT), F32),
                        pltpu.VMEM((2, EC, TT), BF16)],
        compiler_params=_cparams(("arbitrary",)),
        name="peer_dense",
    )(xn, h1, u_bf, vt_bf, rho, c1, r2, e2)


S_STACK = SWA_GROUP * DEC_SEQ
S_SMALL = N_META + DEC_SEQ
NT_DIMS = (((1,), (1,)), ((), ()))


def _sample_kernel(p_ref, cs_ref, rs_ref, meta_ref, win_ref,
                   cw_ref, cb_ref, cg_ref, cbeta_ref,
                   cos_ref, sin_ref, dmat_ref, dq_ref, dk_ref, sgam_ref, gng_ref, gnb_ref, eye_ref,
                   ones_ref, qn_ref, kn_ref, bwin_ref, bsm_ref, sink_ref,
                   conv_o_ref, ret_o_ref, swa_o_ref, ncs_ref, nrs_ref, nwin_ref,
                   xin_ref, kpad_ref, vpad_ref, ksm_ref, vsm_ref):
    kpad_ref[...] = jnp.zeros_like(kpad_ref)
    vpad_ref[...] = jnp.zeros_like(vpad_ref)
    ksm_ref[...] = jnp.zeros_like(ksm_ref)
    vsm_ref[...] = jnp.zeros_like(vsm_ref)
    lane = lax.broadcasted_iota(jnp.int32, (DEC_SEQ, RET_HEADS * RET_DK), 1)
    first_half = (lane % RET_DK) < (RET_DK // 2)
    cos = cos_ref[...]
    sin = sin_ref[...]
    ones_q = ones_ref[...]
    ones_k = ones_ref[0:128, 0:128]

    for s in range(SG):
        rows = slice(s * DEC_SEQ, (s + 1) * DEC_SEQ)
        p = p_ref[rows, :]
        xin, kpad, vpad, ksm, vsm = (r.at[s] for r in (xin_ref, kpad_ref, vpad_ref, ksm_ref, vsm_ref))

        glu = p[:, C_CONV:C_CONV + CONV_CH] * jax.nn.sigmoid(p[:, C_CONV + CONV_CH:C_RQ])
        xin[0:CONV_WIDTH - 1, :] = cs_ref[0, s]
        xin[CONV_WIDTH - 1:CONV_WIDTH - 1 + DEC_SEQ, :] = glu
        acc = jnp.zeros((DEC_SEQ, CONV_CH), F32)
        for k in range(CONV_WIDTH):
            acc = acc + xin[k:k + DEC_SEQ, :] * cw_ref[k:k + 1, :]
        y = acc + cb_ref[...]
        mu = jnp.mean(y, axis=-1, keepdims=True)
        d = y - mu
        var = jnp.mean(d * d, axis=-1, keepdims=True)
        yn = d * lax.rsqrt(var + EPS) * cg_ref[...] + cbeta_ref[...]
        conv_o_ref[rows, :] = yn * jax.nn.sigmoid(yn)
        ncs_ref[s] = xin[DEC_SEQ:DEC_SEQ + CONV_WIDTH - 1, :]

        q = p[:, C_RQ:C_RK]
        k = p[:, C_RK:C_RV]
        q = q * cos + _swap_halves(q, first_half) * sin
        k = (k * cos + _swap_halves(k, first_half) * sin) * (RET_DK ** -0.5)
        rg = p[:, C_RG:C_SQ]
        kpad[0:DEC_SEQ, :] = k
        vpad[0:DEC_SEQ, :] = p[:, C_RV:C_RG]
        for h in range(RET_HEADS):
            qh = q[:, h * RET_DK:(h + 1) * RET_DK]
            kp = kpad[:, h * RET_DK:(h + 1) * RET_DK]
            vp = vpad[:, h * RET_DV:(h + 1) * RET_DV]
            s_old = rs_ref[0, s, h]
            inner = lax.dot_general(qh, kp, NT_DIMS, preferred_element_type=F32) * dmat_ref[h]
            o = (jnp.dot(inner, vp, preferred_element_type=F32)
                 + dq_ref[h] * jnp.dot(qh, s_old, preferred_element_type=F32))
            kdec_t = lax.dot_general(eye_ref[...], kp * dk_ref[h], NT_DIMS,
                                     preferred_element_type=F32)
            nrs_ref[s, h] = sgam_ref[h] * s_old + jnp.dot(kdec_t, vp, preferred_element_type=F32)
            mu = jnp.mean(o, axis=-1, keepdims=True)
            d = o - mu
            var = jnp.mean(d * d, axis=-1, keepdims=True)
            hs = slice(h * RET_DV, (h + 1) * RET_DV)
            yr = d * lax.rsqrt(var + EPS) * gng_ref[:, hs] + gnb_ref[:, hs]
            gate = rg[:, hs]
            ret_o_ref[rows, hs] = yr * (gate * jax.nn.sigmoid(gate))

        sq = p[:, C_SQ:C_SK]
        sk = p[:, C_SK:C_SV]
        qss = jnp.dot(sq * sq, ones_q, preferred_element_type=F32)
        kss = jnp.dot(sk * sk, ones_k, preferred_element_type=F32)
        qa = sq * lax.rsqrt(qss * (1.0 / SWA_HD) + EPS) * qn_ref[...] * (SWA_HD ** -0.5)
        kn = sk * lax.rsqrt(kss * (1.0 / SWA_HD) + EPS) * kn_ref[...]
        vn = p[:, C_SV:PROJ_COLS]
        ksm[0:N_META, :] = meta_ref[0, s, :, 0:128]
        vsm[0:N_META, :] = meta_ref[0, s, :, 128:256]
        ksm[N_META:S_SMALL, :] = kn
        vsm[N_META:S_SMALL, :] = vn
        for kv in range(SWA_KV_HEADS):
            ksl = slice(kv * SWA_HD, (kv + 1) * SWA_HD)
            vsl = slice(128 + kv * SWA_HD, 128 + (kv + 1) * SWA_HD)
            qs = jnp.concatenate([qa[:, (kv * SWA_GROUP + g) * SWA_HD:(kv * SWA_GROUP + g + 1) * SWA_HD]
                                  for g in range(SWA_GROUP)], axis=0)
            s_w = lax.dot_general(qs, win_ref[0, s, :, ksl], NT_DIMS, preferred_element_type=F32) + bwin_ref[kv]
            s_s = lax.dot_general(qs, ksm[:, ksl], NT_DIMS, preferred_element_type=F32) + bsm_ref[kv]
            sink = sink_ref[kv, :, 0:1]
            m = jnp.maximum(jnp.maximum(jnp.max(s_w, axis=-1, keepdims=True),
                                        jnp.max(s_s, axis=-1, keepdims=True)), sink)
            p_w = jnp.exp(s_w - m)
            p_s = jnp.exp(s_s - m)
            den = (jnp.sum(p_w, axis=-1, keepdims=True) + jnp.sum(p_s, axis=-1, keepdims=True)
                   + jnp.exp(sink - m))
            o = (jnp.dot(p_w, win_ref[0, s, :, vsl], preferred_element_type=F32)
                 + jnp.dot(p_s, vsm[:, ksl], preferred_element_type=F32)) / den
            for g in range(SWA_GROUP):
                hh = kv * SWA_GROUP + g
                swa_o_ref[rows, hh * SWA_HD:(hh + 1) * SWA_HD] = o[g * DEC_SEQ:(g + 1) * DEC_SEQ, :]
        nwin_ref[s, 0:WINDOW - DEC_SEQ, :] = win_ref[0, s, DEC_SEQ:WINDOW, :]
        nwin_ref[s, WINDOW - DEC_SEQ:WINDOW, 0:128] = kn
        nwin_ref[s, WINDOW - DEC_SEQ:WINDOW, 128:256] = vn


def _sample_tables(rel_bias):
    lg = jnp.log(1.0 - 2.0 ** (-5.0 - jnp.arange(RET_HEADS, dtype=F32)))
    i = jnp.arange(DEC_SEQ, dtype=F32)
    diff = i[:, None] - i[None, :]
    dm = jnp.where(diff >= 0, jnp.exp(jnp.maximum(diff, 0.0)[None] * lg[:, None, None]), 0.0)
    dmat = jnp.zeros((RET_HEADS, DEC_SEQ, 128), F32).at[:, :, :DEC_SEQ].set(dm)
    dq = jnp.broadcast_to(jnp.exp((i[None] + 1.0) * lg[:, None])[:, :, None], (RET_HEADS, DEC_SEQ, RET_DV))
    kd = jnp.exp((DEC_SEQ - 1.0 - i)[None] * lg[:, None])
    dk = jnp.zeros((RET_HEADS, 128, RET_DK), F32).at[:, :DEC_SEQ, :].set(
        jnp.broadcast_to(kd[:, :, None], (RET_HEADS, DEC_SEQ, RET_DK)))
    sg = jnp.broadcast_to(jnp.exp(DEC_SEQ * lg)[:, None, None], (RET_HEADS, 1, RET_DV))
    cos_t, sin_t = _rotary_tables(PAST_LEN + jnp.arange(DEC_SEQ))

    ti = np.arange(S_STACK)[:, None] % DEC_SEQ
    j = np.arange(WINDOW)[None, :]
    bk_win = _t5_bucket_np(np.maximum(ti + WINDOW - j, 0))
    ok_win = j > ti
    c = np.arange(128)[None, :]
    jn = c - N_META
    bk_new = _t5_bucket_np(np.clip(ti - jn, 0, None))
    ok_sm = (c < N_META) | ((c < S_SMALL) & (jn <= ti))
    bk_sm = np.where(c < N_META, N_BUCKETS - 1, bk_new)
    rb = rel_bias.astype(F32)
    head = np.arange(SWA_KV_HEADS)[:, None] * SWA_GROUP + (np.arange(S_STACK) // DEC_SEQ)[None, :]
    head_oh = _one_hot(head, SWA_HEADS)

    def look(bk, ok):
        b = jnp.einsum("rcb,bh,krh->krc", _one_hot(bk, N_BUCKETS), rb, head_oh,
                       precision=lax.Precision.HIGHEST)
        return jnp.where(jnp.asarray(np.broadcast_to(ok[None], b.shape)), b, NEG)

    return dict(dmat=dmat, dq=dq, dk=dk, sg=sg, cos=cos_t, sin=sin_t,
                bwin=look(bk_win, ok_win), bsm=look(bk_sm, ok_sm), head_oh=head_oh)


def _sample_mixers(p, l, cache_meta_kv, cache_swa_kv, state_ret, state_conv, tabs, conv_w32, conv_b,
                   conv_g, conv_beta, gn_g, gn_b, q_norm, k_norm, sinks):
    row0 = NP_ROWS // (SG * DEC_SEQ)
    nrow = SG * DEC_SEQ
    ones_bd = jnp.asarray(np.kron(np.eye(SWA_HEADS), np.ones((SWA_HD, SWA_HD))), F32)
    qn = jnp.tile(q_norm.astype(F32), SWA_HEADS).reshape(1, -1)
    kn = jnp.tile(k_norm.astype(F32), SWA_KV_HEADS).reshape(1, -1)
    sink_rows = jnp.einsum("h,krh->kr", sinks.astype(F32), tabs["head_oh"], precision=lax.Precision.HIGHEST)
    sink_t = jnp.broadcast_to(sink_rows[:, :, None], (SWA_KV_HEADS, S_STACK, 128))
    meta = cache_meta_kv.reshape(DEPTH, DEC_BATCH, N_META, 256)
    win = cache_swa_kv.reshape(DEPTH, DEC_BATCH, WINDOW, 256)

    def const(shape):
        return pl.BlockSpec(shape, lambda i: (0,) * len(shape))

    return pl.pallas_call(
        _sample_kernel,
        out_shape=(jax.ShapeDtypeStruct((NS_ROWS, CONV_CH), F32),
                   jax.ShapeDtypeStruct((NS_ROWS, RET_HEADS * RET_DV), F32),
                   jax.ShapeDtypeStruct((NS_ROWS, SWA_HEADS * SWA_HD), F32),
                   jax.ShapeDtypeStruct((DEC_BATCH, CONV_WIDTH - 1, CONV_CH), F32),
                   jax.ShapeDtypeStruct((DEC_BATCH, RET_HEADS, RET_DK, RET_DV), F32),
                   jax.ShapeDtypeStruct((DEC_BATCH, WINDOW, 256), F32)),
        grid=(DEC_BATCH // SG,),
        in_specs=[pl.BlockSpec((nrow, PROJ_COLS), lambda i: (row0 + i, 0)),
                  pl.BlockSpec((1, SG, CONV_WIDTH - 1, CONV_CH), lambda i: (l, i, 0, 0)),
                  pl.BlockSpec((1, SG, RET_HEADS, RET_DK, RET_DV), lambda i: (l, i, 0, 0, 0)),
                  pl.BlockSpec((1, SG, N_META, 256), lambda i: (l, i, 0, 0)),
                  pl.BlockSpec((1, SG, WINDOW, 256), lambda i: (l, i, 0, 0)),
                  const((32, CONV_CH)), const((1, CONV_CH)), const((1, CONV_CH)), const((1, CONV_CH)),
                  const((DEC_SEQ, 256)), const((DEC_SEQ, 256)),
                  const((RET_HEADS, DEC_SEQ, 128)), const((RET_HEADS, DEC_SEQ, RET_DV)),
                  const((RET_HEADS, 128, RET_DK)), const((RET_HEADS, 1, RET_DV)),
                  const((1, 512)), const((1, 512)), const((RET_DK, RET_DK)),
                  const((512, 512)), const((1, 512)), const((1, 128)),
                  const((SWA_KV_HEADS, S_STACK, 128)), const((SWA_KV_HEADS, S_STACK, 128)),
                  const((SWA_KV_HEADS, S_STACK, 128))],
        out_specs=(pl.BlockSpec((nrow, CONV_CH), lambda i: (i, 0)),
                   pl.BlockSpec((nrow, 512), lambda i: (i, 0)),
                   pl.BlockSpec((nrow, 512), lambda i: (i, 0)),
                   pl.BlockSpec((SG, CONV_WIDTH - 1, CONV_CH), lambda i: (i, 0, 0)),
                   pl.BlockSpec((SG, RET_HEADS, RET_DK, RET_DV), lambda i: (i, 0, 0, 0)),
                   pl.BlockSpec((SG, WINDOW, 256), lambda i: (i, 0, 0))),
        scratch_shapes=[pltpu.VMEM((SG, 40, CONV_CH), F32), pltpu.VMEM((SG, 128, RET_HEADS * RET_DK), F32),
                        pltpu.VMEM((SG, 128, RET_HEADS * RET_DV), F32),
                        pltpu.VMEM((SG, 128, 128), F32), pltpu.VMEM((SG, 128, 128), F32)],
        compiler_params=_cparams(("parallel",)),
        name="sample_mixers",
    )(p, state_conv, state_ret, meta, win,
      conv_w32, conv_b.reshape(1, -1), conv_g.reshape(1, -1), conv_beta.reshape(1, -1),
      tabs["cos"], tabs["sin"], tabs["dmat"], tabs["dq"], tabs["dk"], tabs["sg"],
      gn_g.reshape(1, -1), gn_b.reshape(1, -1), jnp.eye(RET_DK, dtype=F32),
      ones_bd, qn, kn, tabs["bwin"], tabs["bsm"], sink_t)


def kernel(x_prompt, x_sample, cache_meta_kv, cache_swa_kv, state_ret, state_conv, meta_tokens, rel_bias,
           norm_mix, w_in, conv_w, conv_b, conv_ln_g, conv_ln_b, ret_gn_g, ret_gn_b, swa_q_norm,
           swa_k_norm, swa_sinks, w_out, norm_ffn, peer_wq, peer_keys, peer_u, peer_v):
    meta = jnp.broadcast_to(meta_tokens.astype(F32)[None], (BATCH, N_META, D_MODEL))
    pad = jnp.zeros((BATCH, LP - L_REAL, D_MODEL), F32)
    hp = jnp.concatenate([meta, x_prompt, pad], axis=1).reshape(NP_ROWS, D_MODEL)
    h = jnp.concatenate([hp, x_sample.reshape(NS_ROWS, D_MODEL)], axis=0)

    cos_t, sin_t = _rotary_tables(jnp.arange(LP))
    ret_tabs = _ret_tables()
    bias_tabs = _bias_tables(rel_bias)
    sample_tabs = _sample_tables(rel_bias)

    meta_p, win_p, ret_p, conv_p, win_s, ret_s, conv_s = [], [], [], [], [], [], []
    for l in range(DEPTH):
        p = _norm_proj(h, norm_mix[l], w_in[l].astype(BF16))

        w32 = jnp.concatenate([conv_w[l], jnp.zeros((1, CONV_CH), F32)], axis=0)
        conv_o, conv_tail = _conv_prompt(p, w32, conv_b[l], conv_ln_g[l], conv_ln_b[l])
        ret_o, ret_state = _ret_prompt(p, cos_t, sin_t, ret_tabs, ret_gn_g[l], ret_gn_b[l])
        swa_o, k_normed = _swa_prompt(p, swa_sinks[l], swa_q_norm[l], swa_k_norm[l], bias_tabs)
        s_conv, s_ret, s_swa, new_conv, new_ret, new_win = _sample_mixers(
            p, l, cache_meta_kv, cache_swa_kv, state_ret, state_conv, sample_tabs, w32, conv_b[l],
            conv_ln_g[l], conv_ln_b[l], ret_gn_g[l], ret_gn_b[l], swa_q_norm[l], swa_k_norm[l], swa_sinks[l])
        new_win = new_win.reshape(DEC_BATCH, WINDOW, 2, SWA_KV_HEADS, SWA_HD)

        kbd = jnp.zeros((2 * N_KEYS, 2 * PEER_HALF), F32)
        kbd = kbd.at[:N_KEYS, :PEER_HALF].set(peer_keys[l, 0]).at[N_KEYS:, PEER_HALF:].set(peer_keys[l, 1])
        h1, xn, st = _out_ffn(h, (conv_o, ret_o, swa_o), (s_conv, s_ret, s_swa), w_out[l].astype(BF16),
                              norm_ffn[l], peer_wq[l].astype(BF16), kbd.astype(BF16))
        rho, c1, r2, e2 = _topk(st)
        h = _peer(xn, h1, peer_u[l].astype(BF16), peer_v[l].T.astype(BF16), rho, c1, r2, e2)

        kp = k_normed.reshape(BATCH, LP, SWA_KV_HEADS, SWA_HD)
        vp = p[:NP_ROWS, C_SV:].reshape(BATCH, LP, SWA_KV_HEADS, SWA_HD)
        kvp = jnp.stack([kp, vp], axis=2)
        meta_p.append(kvp[:, :N_META])
        win_p.append(kvp[:, L_REAL - WINDOW:L_REAL])
        ret_p.append(ret_state)
        conv_p.append(conv_tail[:, 32 - (CONV_WIDTH - 1):])
        win_s.append(new_win)
        ret_s.append(new_ret)
        conv_s.append(new_conv)

    y_prompt = h[:NP_ROWS].reshape(BATCH, LP, D_MODEL)[:, N_META:L_REAL]
    y_sample = h[NP_ROWS:].reshape(DEC_BATCH, DEC_SEQ, D_MODEL)
    return (y_prompt, y_sample, jnp.stack(meta_p), jnp.stack(win_p), jnp.stack(ret_p), jnp.stack(conv_p),
            jnp.stack(win_s), jnp.stack(ret_s), jnp.stack(conv_s))
```

```python
import functools
import math

import numpy as np
import jax
import jax.numpy as jnp
from jax import lax
from jax.experimental import pallas as pl
from jax.experimental.pallas import tpu as pltpu

F32 = jnp.float32
BF16 = jnp.bfloat16

D_MODEL = 1024
BATCH = 2
SEQ = 8192
DEPTH = 2
DEC_BATCH = 128
DEC_SEQ = 8
PAST_LEN = 8192
N_META = 16
CONV_CH = 512
CONV_WIDTH = 31
RET_HEADS = 4
RET_DK = 64
RET_DV = 128
SWA_HEADS = 8
SWA_KV_HEADS = 2
SWA_GROUP = SWA_HEADS // SWA_KV_HEADS
SWA_HD = 64
WINDOW = 128
N_BUCKETS = 32
REL_MAX_DIST = 128
PEER_HEADS = 8
N_KEYS = 128
N_EXPERTS = N_KEYS * N_KEYS
PEER_TOPK = 16
PEER_HALF = 64
EPS = 1e-6
NEG = -1e30

PROJ_COLS = 3328
C_CONV, C_RQ, C_RK, C_RV, C_RG, C_SQ, C_SK, C_SV = 0, 1024, 1280, 1536, 2048, 2560, 3072, 3200

L_REAL = N_META + SEQ
BLK = 128
N_BLK = 65
LP = N_BLK * BLK
NP_ROWS = BATCH * LP
NS_ROWS = DEC_BATCH * DEC_SEQ
N_ROWS = NP_ROWS + NS_ROWS
LAST_REAL = L_REAL - (N_BLK - 1) * BLK

TM = 384
TO = 256
TT = 768
SG = 8
EC = 1024
PEER_LANES = 256
CONV_T = 640
CONV_RB = 64
VMEM_LIMIT = 56 * 1024 * 1024


def _cparams(sem, flags=None):
    return pltpu.CompilerParams(dimension_semantics=sem, vmem_limit_bytes=VMEM_LIMIT, flags=flags)


def _norm_proj_kernel(x_ref, g_ref, w_ref, o_ref):
    x = x_ref[...]
    ms = jnp.mean(x * x, axis=-1, keepdims=True)
    xn = x * lax.rsqrt(ms + EPS) * g_ref[...]
    o_ref[...] = jnp.dot(xn.astype(BF16), w_ref[...], preferred_element_type=F32)


def _norm_proj(h, g, w_bf):
    return pl.pallas_call(
        _norm_proj_kernel,
        out_shape=jax.ShapeDtypeStruct((N_ROWS, PROJ_COLS), F32),
        grid=(N_ROWS // TM,),
        in_specs=[pl.BlockSpec((TM, D_MODEL), lambda i: (i, 0)),
                  pl.BlockSpec((1, D_MODEL), lambda i: (0, 0)),
                  pl.BlockSpec((D_MODEL, PROJ_COLS), lambda i: (0, 0))],
        out_specs=pl.BlockSpec((TM, PROJ_COLS), lambda i: (i, 0)),
        compiler_params=_cparams(("parallel",)),
        name="norm_proj",
    )(h, g.reshape(1, D_MODEL), w_bf)


def _conv_kernel(c_ref, w_ref, b_ref, g_ref, beta_ref, o_ref, st_ref, xin_ref, xs_ref):
    t = pl.program_id(1)

    @pl.when(t == 0)
    def _():
        xin_ref[0:32, :] = jnp.zeros((32, CONV_CH), F32)

    @pl.when(t > 0)
    def _():
        xin_ref[0:32, :] = xin_ref[CONV_T:CONV_T + 32, :]

    c = c_ref[...]
    xin_ref[32:32 + CONV_T, :] = c[:, :CONV_CH] * jax.nn.sigmoid(c[:, CONV_CH:])
    n_sh = 32 + CONV_T - 8
    for r in range(1, 8):
        xs_ref[r - 1, 0:n_sh, :] = xin_ref[r:r + n_sh, :]

    w = w_ref[...]
    bias = b_ref[...]
    gam = g_ref[...]
    beta = beta_ref[...]
    for rb in range(CONV_T // CONV_RB):
        r0 = rb * CONV_RB
        acc = jnp.zeros((CONV_RB, CONV_CH), F32)
        for k in range(CONV_WIDTH):
            sh = (2 + k) % 8
            lo = r0 + 2 + k - sh
            win = xin_ref[lo:lo + CONV_RB, :] if sh == 0 else xs_ref[sh - 1, lo:lo + CONV_RB, :]
            acc = acc + win * w[k:k + 1, :]
        y = acc + bias
        mu = jnp.mean(y, axis=-1, keepdims=True)
        d = y - mu
        var = jnp.mean(d * d, axis=-1, keepdims=True)
        yn = d * lax.rsqrt(var + EPS) * gam + beta
        o_ref[r0:r0 + CONV_RB, :] = yn * jax.nn.sigmoid(yn)

    @pl.when(t == pl.num_programs(1) - 1)
    def _():
        lo = 32 + (L_REAL - 32) - (LP - CONV_T)
        st_ref[0] = xin_ref[lo:lo + 32, :]


def _conv_prompt(p, w32, b, g, beta):
    nt = LP // CONV_T
    return pl.pallas_call(
        _conv_kernel,
        out_shape=(jax.ShapeDtypeStruct((NP_ROWS, CONV_CH), F32),
                   jax.ShapeDtypeStruct((BATCH, 32, CONV_CH), F32)),
        grid=(BATCH, nt),
        in_specs=[pl.BlockSpec((CONV_T, 2 * CONV_CH), lambda bi, t: (bi * nt + t, 0)),
                  pl.BlockSpec((32, CONV_CH), lambda bi, t: (0, 0)),
                  pl.BlockSpec((1, CONV_CH), lambda bi, t: (0, 0)),
                  pl.BlockSpec((1, CONV_CH), lambda bi, t: (0, 0)),
                  pl.BlockSpec((1, CONV_CH), lambda bi, t: (0, 0))],
        out_specs=(pl.BlockSpec((CONV_T, CONV_CH), lambda bi, t: (bi * nt + t, 0)),
                   pl.BlockSpec((1, 32, CONV_CH), lambda bi, t: (bi, 0, 0))),
        scratch_shapes=[pltpu.VMEM((32 + CONV_T, CONV_CH), F32),
                        pltpu.VMEM((7, 32 + CONV_T - 8, CONV_CH), F32)],
        compiler_params=_cparams(("arbitrary", "arbitrary")),
        name="conv_prompt",
    )(p, w32, b.reshape(1, -1), g.reshape(1, -1), beta.reshape(1, -1))


def _swap_halves(x, first_half):
    return jnp.where(first_half, pltpu.roll(x, x.shape[1] - 32, axis=1), pltpu.roll(x, 32, axis=1))


def _ret_kernel(q_ref, k_ref, v_ref, rg_ref, cos_ref, sin_ref, dmat_ref, dq_ref, dk_ref, sg_ref,
                gng_ref, gnb_ref, o_ref, st_ref, s_ref):
    j = pl.program_id(1)

    @pl.when(j == 0)
    def _():
        s_ref[...] = jnp.zeros_like(s_ref)

    cos = cos_ref[...]
    sin = sin_ref[...]
    lane = lax.broadcasted_iota(jnp.int32, (BLK, RET_HEADS * RET_DK), 1)
    first_half = (lane % RET_DK) < (RET_DK // 2)
    q = q_ref[...]
    k = k_ref[...]
    q = q * cos + _swap_halves(q, first_half) * sin
    k = (k * cos + _swap_halves(k, first_half) * sin) * (RET_DK ** -0.5)
    v = v_ref[...]
    rg = rg_ref[...]
    hs = range(RET_HEADS)
    qh = [q[:, h * RET_DK:(h + 1) * RET_DK].astype(BF16) for h in hs]
    vh = [v[:, h * RET_DV:(h + 1) * RET_DV].astype(BF16) for h in hs]
    inner, q_state, kdec_t = [], [], []
    for h in hs:
        kh = k[:, h * RET_DK:(h + 1) * RET_DK]
        inner.append(lax.dot_general(qh[h], kh.astype(BF16), (((1,), (1,)), ((), ())),
                                     preferred_element_type=F32) * dmat_ref[h])
        q_state.append(jnp.dot(qh[h], s_ref[h].astype(BF16), preferred_element_type=F32))
        kdec_t.append((kh * dk_ref[0, h]).T.astype(BF16))
    outs = []
    for h in hs:
        outs.append(jnp.dot(inner[h].astype(BF16), vh[h], preferred_element_type=F32) + dq_ref[h] * q_state[h])
        s_ref[h] = sg_ref[0, h, 0:1, :] * s_ref[h] + jnp.dot(kdec_t[h], vh[h], preferred_element_type=F32)
    for h in hs:
        o = outs[h]
        mu = jnp.mean(o, axis=-1, keepdims=True)
        d = o - mu
        var = jnp.mean(d * d, axis=-1, keepdims=True)
        y = d * lax.rsqrt(var + EPS) * gng_ref[:, h * RET_DV:(h + 1) * RET_DV] \
            + gnb_ref[:, h * RET_DV:(h + 1) * RET_DV]
        gate = rg[:, h * RET_DV:(h + 1) * RET_DV]
        o_ref[:, h * RET_DV:(h + 1) * RET_DV] = y * (gate * jax.nn.sigmoid(gate))

    @pl.when(j == pl.num_programs(1) - 1)
    def _():
        st_ref[0] = s_ref[...]


def _ret_tables():
    lg = jnp.log(1.0 - 2.0 ** (-5.0 - jnp.arange(RET_HEADS, dtype=F32)))
    i = jnp.arange(BLK, dtype=F32)
    diff = i[:, None] - i[None, :]
    dmat = jnp.where(diff >= 0, jnp.exp(jnp.maximum(diff, 0.0)[None] * lg[:, None, None]), 0.0)
    dq = jnp.broadcast_to(jnp.exp((i[None] + 1.0) * lg[:, None])[:, :, None], (RET_HEADS, BLK, RET_DV))

    def kdec(c_eff):
        e = jnp.where(i[None] < c_eff, jnp.exp((c_eff - 1.0 - i)[None] * lg[:, None]), 0.0)
        return jnp.broadcast_to(e[:, :, None], (RET_HEADS, BLK, RET_DK))

    def sgam(c_eff):
        return jnp.broadcast_to(jnp.exp(c_eff * lg)[:, None, None], (RET_HEADS, 8, RET_DV))

    dk = jnp.stack([kdec(float(BLK)), kdec(float(LAST_REAL))])
    sg = jnp.stack([sgam(float(BLK)), sgam(float(LAST_REAL))])
    return dmat, dq, dk, sg


def _rotary_tables(pos):
    half = RET_DK // 2
    inv = 1.0 / (10000.0 ** (jnp.arange(half, dtype=F32) / half))
    ang = pos.astype(F32)[:, None] * inv[None]
    cos, sin = jnp.cos(ang), jnp.sin(ang)
    cos_t = jnp.tile(jnp.concatenate([cos, cos], axis=-1), (1, RET_HEADS))
    sin_t = jnp.tile(jnp.concatenate([-sin, sin], axis=-1), (1, RET_HEADS))
    return cos_t, sin_t


def _ret_prompt(p, cos_t, sin_t, tabs, gn_g, gn_b):
    dmat, dq, dk, sg = tabs
    nb = N_BLK
    last = nb - 1
    return pl.pallas_call(
        _ret_kernel,
        out_shape=(jax.ShapeDtypeStruct((NP_ROWS, RET_HEADS * RET_DV), F32),
                   jax.ShapeDtypeStruct((BATCH, RET_HEADS, RET_DK, RET_DV), F32)),
        grid=(BATCH, nb),
        in_specs=[pl.BlockSpec((BLK, 256), lambda b, j: (b * nb + j, C_RQ // 256)),
                  pl.BlockSpec((BLK, 256), lambda b, j: (b * nb + j, C_RK // 256)),
                  pl.BlockSpec((BLK, 512), lambda b, j: (b * nb + j, C_RV // 512)),
                  pl.BlockSpec((BLK, 512), lambda b, j: (b * nb + j, C_RG // 512)),
                  pl.BlockSpec((BLK, 256), lambda b, j: (j, 0)),
                  pl.BlockSpec((BLK, 256), lambda b, j: (j, 0)),
                  pl.BlockSpec((RET_HEADS, BLK, BLK), lambda b, j: (0, 0, 0)),
                  pl.BlockSpec((RET_HEADS, BLK, RET_DV), lambda b, j: (0, 0, 0)),
                  pl.BlockSpec((1, RET_HEADS, BLK, RET_DK), lambda b, j: (j // last, 0, 0, 0)),
                  pl.BlockSpec((1, RET_HEADS, 8, RET_DV), lambda b, j: (j // last, 0, 0, 0)),
                  pl.BlockSpec((1, 512), lambda b, j: (0, 0)),
                  pl.BlockSpec((1, 512), lambda b, j: (0, 0))],
        out_specs=(pl.BlockSpec((BLK, 512), lambda b, j: (b * nb + j, 0)),
                   pl.BlockSpec((1, RET_HEADS, RET_DK, RET_DV), lambda b, j: (b, 0, 0, 0))),
        scratch_shapes=[pltpu.VMEM((RET_HEADS, RET_DK, RET_DV), F32)],
        compiler_params=_cparams(("arbitrary", "arbitrary")),
        name="ret_prompt",
    )(p, p, p, p, cos_t, sin_t, dmat, dq, dk, sg, gn_g.reshape(1, -1), gn_b.reshape(1, -1))


def _group_rms(x, ones_bd, w):
    x2 = x * x
    hi = x2.astype(BF16)
    lo = (x2 - hi.astype(F32)).astype(BF16)
    ss = (jnp.dot(hi, ones_bd, preferred_element_type=F32)
          + jnp.dot(lo, ones_bd, preferred_element_type=F32))
    return x * lax.rsqrt(ss * (1.0 / SWA_HD) + EPS) * w


def _swa_kernel(sinks_ref, q_ref, kc_ref, vc_ref, kp_ref, vp_ref, km_ref, vm_ref, ones_ref,
                qn_ref, kn_ref, bc_ref, bp_ref, bm_ref, o_ref, kout_ref):
    j = pl.program_id(1)
    ones_q = ones_ref[...]
    ones_k = ones_ref[0:128, 0:128]
    qw = qn_ref[...]
    kw = kn_ref[...]
    q = _group_rms(q_ref[...], ones_q, qw) * (SWA_HD ** -0.5)
    kc = _group_rms(kc_ref[...], ones_k, kw)
    kp = _group_rms(kp_ref[...], ones_k, kw)
    km = _group_rms(km_ref[...], ones_k, kw)
    kout_ref[...] = kc
    vc = vc_ref[...].astype(BF16)
    vp = vp_ref[...].astype(BF16)
    vm = vm_ref[...].astype(BF16)

    rows = SWA_GROUP * BLK
    qi = lax.broadcasted_iota(jnp.int32, (rows, BLK), 0) % BLK
    kj = lax.broadcasted_iota(jnp.int32, (rows, BLK), 1)
    valid_c = jnp.where(kj <= qi, j * BLK + kj, -1) >= N_META
    valid_p = jnp.where(kj > qi, (j - 1) * BLK + kj, -1) >= N_META
    qm = lax.broadcasted_iota(jnp.int32, (rows, N_META), 0) % BLK
    mm = lax.broadcasted_iota(jnp.int32, (rows, N_META), 1)
    valid_m = mm <= j * BLK + qm

    dn = (((1,), (1,)), ((), ()))
    kvs = range(SWA_KV_HEADS)
    heads = [range(kv * SWA_GROUP, (kv + 1) * SWA_GROUP) for kv in kvs]
    ksl = [slice(kv * SWA_HD, (kv + 1) * SWA_HD) for kv in kvs]
    s_c, s_p, s_m = [], [], []
    for kv in kvs:
        qs = jnp.concatenate([q[:, h * SWA_HD:(h + 1) * SWA_HD] for h in heads[kv]], axis=0).astype(BF16)
        s_c.append(lax.dot_general(qs, kc[:, ksl[kv]].astype(BF16), dn, preferred_element_type=F32))
        s_p.append(lax.dot_general(qs, kp[:, ksl[kv]].astype(BF16), dn, preferred_element_type=F32))
        s_m.append(lax.dot_general(qs, km[:, ksl[kv]].astype(BF16), dn, preferred_element_type=F32))
    p_c, p_p, p_m, den = [], [], [], []
    for kv in kvs:
        sc = jnp.where(valid_c, s_c[kv] + bc_ref[kv], NEG)
        sp = jnp.where(valid_p, s_p[kv] + bp_ref[kv], NEG)
        sm = jnp.where(valid_m, s_m[kv] + bm_ref[0, kv], NEG)
        sink = jnp.concatenate([jnp.full((BLK, 1), sinks_ref[h], F32) for h in heads[kv]], axis=0)
        m = jnp.maximum(jnp.maximum(jnp.max(sc, axis=-1, keepdims=True),
                                    jnp.max(sp, axis=-1, keepdims=True)),
                        jnp.maximum(jnp.max(sm, axis=-1, keepdims=True), sink))
        p_c.append(jnp.exp(sc - m))
        p_p.append(jnp.exp(sp - m))
        p_m.append(jnp.exp(sm - m))
        den.append(jnp.sum(p_c[kv], axis=-1, keepdims=True) + jnp.sum(p_p[kv], axis=-1, keepdims=True)
                   + jnp.sum(p_m[kv], axis=-1, keepdims=True) + jnp.exp(sink - m))
    for kv in kvs:
        acc = (jnp.dot(p_c[kv].astype(BF16), vc[:, ksl[kv]], preferred_element_type=F32)
               + jnp.dot(p_p[kv].astype(BF16), vp[:, ksl[kv]], preferred_element_type=F32)
               + jnp.dot(p_m[kv].astype(BF16), vm[:, ksl[kv]], preferred_element_type=F32))
        out = acc / den[kv]
        for g, h in enumerate(heads[kv]):
            o_ref[:, h * SWA_HD:(h + 1) * SWA_HD] = out[g * BLK:(g + 1) * BLK, :]


def _one_hot(idx, n):
    return jnp.asarray(np.asarray(idx)[..., None] == np.arange(n), F32)


def _t5_bucket_np(dist):
    max_exact = N_BUCKETS // 2
    df = np.maximum(dist, 1).astype(np.float64)
    large = max_exact + (np.log(df / max_exact) / math.log(REL_MAX_DIST / max_exact)
                         * (N_BUCKETS - max_exact)).astype(np.int64)
    large = np.minimum(large, N_BUCKETS - 1)
    return np.where(dist < max_exact, dist, large).astype(np.int32)


def _bias_tables(rel_bias):
    qi = np.arange(BLK)[:, None]
    kj = np.arange(BLK)[None, :]
    b_cur = _t5_bucket_np(np.maximum(qi - kj, 0))
    b_prev = _t5_bucket_np(np.maximum(qi - kj + BLK, 0))
    m = np.arange(N_META)[None, :]
    b_m0 = _t5_bucket_np(np.maximum(qi - m, 0))
    b_m1 = _t5_bucket_np(np.maximum(qi + BLK - m, 0))
    assert (b_m1 == N_BUCKETS - 1).all()
    rb = rel_bias.astype(F32)

    def look(bk):
        return jnp.einsum("...b,bh->h...", _one_hot(bk, N_BUCKETS), rb, precision=lax.Precision.HIGHEST)

    return look(b_cur), look(b_prev), jnp.stack([look(b_m0), look(b_m1)])


def _swa_prompt(p, sinks, q_norm, k_norm, bias_tabs):
    srows = SWA_GROUP * BLK
    b_cur = bias_tabs[0].reshape(SWA_KV_HEADS, srows, BLK)
    b_prev = bias_tabs[1].reshape(SWA_KV_HEADS, srows, BLK)
    b_meta = bias_tabs[2].reshape(2, SWA_KV_HEADS, srows, N_META)
    nb = N_BLK
    ones_bd = jnp.asarray(np.kron(np.eye(SWA_HEADS), np.ones((SWA_HD, SWA_HD))), BF16)
    qn = jnp.tile(q_norm.astype(F32), SWA_HEADS).reshape(1, -1)
    kn = jnp.tile(k_norm.astype(F32), SWA_KV_HEADS).reshape(1, -1)
    ck, cv = C_SK // 128, C_SV // 128
    return pl.pallas_call(
        _swa_kernel,
        out_shape=(jax.ShapeDtypeStruct((NP_ROWS, SWA_HEADS * SWA_HD), F32),
                   jax.ShapeDtypeStruct((NP_ROWS, SWA_KV_HEADS * SWA_HD), F32)),
        grid=(BATCH, nb),
        in_specs=[pl.BlockSpec(memory_space=pltpu.SMEM),
                  pl.BlockSpec((BLK, 512), lambda b, j: (b * nb + j, C_SQ // 512)),
                  pl.BlockSpec((BLK, 128), lambda b, j: (b * nb + j, ck)),
                  pl.BlockSpec((BLK, 128), lambda b, j: (b * nb + j, cv)),
                  pl.BlockSpec((BLK, 128), lambda b, j: (b * nb + jnp.maximum(j - 1, 0), ck)),
                  pl.BlockSpec((BLK, 128), lambda b, j: (b * nb + jnp.maximum(j - 1, 0), cv)),
                  pl.BlockSpec((N_META, 128), lambda b, j: (b * (LP // N_META), ck)),
                  pl.BlockSpec((N_META, 128), lambda b, j: (b * (LP // N_META), cv)),
                  pl.BlockSpec((512, 512), lambda b, j: (0, 0)),
                  pl.BlockSpec((1, 512), lambda b, j: (0, 0)),
                  pl.BlockSpec((1, 128), lambda b, j: (0, 0)),
                  pl.BlockSpec((SWA_KV_HEADS, srows, BLK), lambda b, j: (0, 0, 0)),
                  pl.BlockSpec((SWA_KV_HEADS, srows, BLK), lambda b, j: (0, 0, 0)),
                  pl.BlockSpec((1, SWA_KV_HEADS, srows, N_META), lambda b, j: (jnp.minimum(j, 1), 0, 0, 0))],
        out_specs=(pl.BlockSpec((BLK, 512), lambda b, j: (b * nb + j, 0)),
                   pl.BlockSpec((BLK, 128), lambda b, j: (b * nb + j, 0))),
        compiler_params=_cparams(("parallel", "parallel")),
        name="swa_prompt",
    )(sinks.astype(F32), p, p, p, p, p, p, p, ones_bd, qn, kn, b_cur, b_prev, b_meta)


def _out_ffn_kernel(h_ref, a_ref, b_ref, c_ref, as_ref, bs_ref, cs_ref, wo_ref, g_ref, wq_ref, kbd_ref,
                    h1_ref, xn_ref, st_ref):
    is_prompt = pl.program_id(0) < NP_ROWS // TO
    a = jnp.where(is_prompt, a_ref[...], as_ref[...])
    b = jnp.where(is_prompt, b_ref[...], bs_ref[...])
    c = jnp.where(is_prompt, c_ref[...], cs_ref[...])
    h1 = (h_ref[...]
          + jnp.dot(a.astype(BF16), wo_ref[0:512, :], preferred_element_type=F32)
          + jnp.dot(b.astype(BF16), wo_ref[512:1024, :], preferred_element_type=F32)
          + jnp.dot(c.astype(BF16), wo_ref[1024:1536, :], preferred_element_type=F32))
    h1_ref[...] = h1
    ms = jnp.mean(h1 * h1, axis=-1, keepdims=True)
    xn_f = h1 * lax.rsqrt(ms + EPS) * g_ref[...]
    xn_ref[...] = xn_f.T.astype(BF16)
    q = jnp.dot(xn_f.astype(BF16), wq_ref[...], preferred_element_type=F32).astype(BF16)
    kbd = kbd_ref[...]
    for hh in range(PEER_HEADS):
        st_ref[hh] = lax.dot_general(kbd, q[:, hh * 128:(hh + 1) * 128], (((1,), (1,)), ((), ())),
                                     preferred_element_type=F32)


def _out_ffn(h, mix_p, mix_s, wo_bf, g, wq_bf, kbd_bf):
    npb = NP_ROWS // TO
    pspec = pl.BlockSpec((TO, 512), lambda i: (jnp.minimum(i, npb - 1), 0))
    sspec = pl.BlockSpec((TO, 512), lambda i: (jnp.maximum(i - npb, 0), 0))
    return pl.pallas_call(
        _out_ffn_kernel,
        out_shape=(jax.ShapeDtypeStruct((N_ROWS, D_MODEL), F32),
                   jax.ShapeDtypeStruct((D_MODEL, N_ROWS), BF16),
                   jax.ShapeDtypeStruct((PEER_HEADS, 2 * N_KEYS, N_ROWS), F32)),
        grid=(N_ROWS // TO,),
        in_specs=[pl.BlockSpec((TO, D_MODEL), lambda i: (i, 0)),
                  pspec, pspec, pspec, sspec, sspec, sspec,
                  pl.BlockSpec((1536, D_MODEL), lambda i: (0, 0)),
                  pl.BlockSpec((1, D_MODEL), lambda i: (0, 0)),
                  pl.BlockSpec((D_MODEL, D_MODEL), lambda i: (0, 0)),
                  pl.BlockSpec((2 * N_KEYS, 128), lambda i: (0, 0))],
        out_specs=(pl.BlockSpec((TO, D_MODEL), lambda i: (i, 0)),
                   pl.BlockSpec((D_MODEL, TO), lambda i: (0, i)),
                   pl.BlockSpec((PEER_HEADS, 2 * N_KEYS, TO), lambda i: (0, 0, i))),
        compiler_params=_cparams(("parallel",)),
        name="out_ffn",
    )(h, *mix_p, *mix_s, wo_bf, g.reshape(1, -1), wq_bf, kbd_bf)


def _top_vals(x, n, with_rank=False):
    vals = []
    rank = jnp.full(x.shape, float(n), F32)
    for r in range(n):
        mx = jnp.max(x, axis=0, keepdims=True)
        vals.append(mx)
        hit = x == mx
        if with_rank:
            rank = jnp.where(hit, float(r), rank)
        x = jnp.where(hit, NEG, x)
    return (vals, rank) if with_rank else vals


def _topk_kernel(st_ref, rho_ref, c1_ref, r2_ref, e2_ref):
    s1 = st_ref[0, 0:N_KEYS, :]
    s2 = st_ref[0, N_KEYS:2 * N_KEYS, :]
    v1 = _top_vals(s1, PEER_TOPK)
    v2, rank2 = _top_vals(s2, PEER_TOPK, with_rank=True)
    sv1 = jnp.concatenate(v1, axis=0)
    sv2 = jnp.concatenate(v2, axis=0)
    cand = jnp.concatenate([v1[0] + sv2, v1[1] + sv2]
                           + [v1[a] + sv2[0:8] for a in range(2, 8)]
                           + [sv1[8:16] + v2[0]], axis=0)
    top = _top_vals(cand, PEER_TOPK)
    tau = top[PEER_TOPK - 1]
    z = jnp.ones_like(tau)
    for r in range(1, PEER_TOPK):
        z = z + jnp.exp(top[r] - top[0])
    rho = jnp.zeros(s1.shape, F32)
    for a in range(PEER_TOPK):
        n_a = jnp.sum(jnp.where(v1[a] + sv2 >= tau, 1.0, 0.0), axis=0, keepdims=True)
        rho = jnp.where(s1 == v1[a], n_a, rho)
    rho_ref[0] = rho
    c1_ref[0] = jnp.exp(s1 - v1[0]) * (0.5 / z)
    r2_ref[0] = rank2.astype(BF16)
    e2_ref[0] = jnp.exp(s2 - v2[0]).astype(BF16)


def _topk(st):
    shp = jax.ShapeDtypeStruct((PEER_HEADS, N_KEYS, N_ROWS), F32)
    shp_bf = jax.ShapeDtypeStruct((PEER_HEADS, N_KEYS, N_ROWS), BF16)
    spec = pl.BlockSpec((1, N_KEYS, TM), lambda i, h: (h, 0, i))
    return pl.pallas_call(
        _topk_kernel,
        out_shape=(shp, shp, shp_bf, shp_bf),
        grid=(N_ROWS // TM, PEER_HEADS),
        in_specs=[pl.BlockSpec((1, 2 * N_KEYS, TM), lambda i, h: (h, 0, i))],
        out_specs=(spec, spec, spec, spec),
        compiler_params=_cparams(("parallel", "parallel")),
        name="peer_topk",
    )(st)


N_CHUNK = N_EXPERTS // EC
N_WORK = (N_ROWS // TT) * N_CHUNK


def _peer_stages(xn_ref, u_ref, vt_ref, rho_ref, c1_ref, r2_ref, e2_ref, acc_ref,
                 at_w, at_r, gt_w, gt_r):
    sub = 16
    n_tb = TT // PEER_LANES
    n_ii = EC // N_KEYS

    def gate_build(ii, tb, jbs):
        cols = slice(tb * PEER_LANES, (tb + 1) * PEER_LANES)
        w = {jb: jnp.zeros((sub, PEER_LANES), BF16) for jb in jbs}
        for hh in range(PEER_HEADS):
            rho = jnp.broadcast_to(rho_ref[hh, ii:ii + 1, cols], (sub, PEER_LANES)).astype(BF16)
            cc = jnp.broadcast_to(c1_ref[hh, ii:ii + 1, cols], (sub, PEER_LANES)).astype(BF16)
            for jb in jbs:
                jr = slice(jb * sub, (jb + 1) * sub)
                w[jb] = w[jb] + jnp.where(r2_ref[hh, jr, cols] < rho, e2_ref[hh, jr, cols] * cc, 0.0)
        for jb in jbs:
            rows = slice(ii * N_KEYS + jb * sub, ii * N_KEYS + (jb + 1) * sub)
            a = at_r[rows, cols]
            gt_w[rows, cols] = w[jb] * (a * (1.0 + lax.erf(a * (2.0 ** -0.5))))

    at_w[...] = jnp.dot(u_ref[...], xn_ref[...], preferred_element_type=F32).astype(BF16)
    acc_ref[...] += jnp.dot(vt_ref[...], gt_r[...], preferred_element_type=F32)
    all_jb = range(N_KEYS // sub)
    for ii in range(n_ii):
        for tb in range(n_tb):
            gate_build(ii, tb, all_jb)


def _peer_kernel(xn_ref, h1_ref, u_ref, vt_ref, rho_ref, c1_ref, r2_ref, e2_ref, o_ref,
                 acc_ref, at, gt):
    g = pl.program_id(0)
    cv = (g - 1) % N_CHUNK
    slot = g % 2

    @pl.when(g == 0)
    def _():
        gt[1] = jnp.zeros((EC, TT), BF16)

    @pl.when(jnp.logical_or(g < 1, cv == 0))
    def _():
        acc_ref[...] = jnp.zeros_like(acc_ref)

    _peer_stages(xn_ref, u_ref, vt_ref, rho_ref, c1_ref, r2_ref, e2_ref, acc_ref,
                 at, at, gt.at[slot], gt.at[1 - slot])

    @pl.when(jnp.logical_and(g >= 1, cv == N_CHUNK - 1))
    def _():
        o_ref[...] = h1_ref[...] + acc_ref[...].T


def _peer(xn, h1, u_bf, vt_bf, rho, c1, r2, e2):
    nk = EC // N_KEYS

    def item(g, lag):
        w = jnp.clip(g - lag, 0, N_WORK - 1)
        return w // N_CHUNK, w % N_CHUNK

    return pl.pallas_call(
        _peer_kernel,
        out_shape=jax.ShapeDtypeStruct((N_ROWS, D_MODEL), F32),
        grid=(N_WORK + 1,),
        in_specs=[pl.BlockSpec((D_MODEL, TT), lambda g: (0, item(g, 0)[0])),
                  pl.BlockSpec((TT, D_MODEL), lambda g: (item(g, 1)[0], 0)),
                  pl.BlockSpec((EC, D_MODEL), lambda g: (item(g, 0)[1], 0)),
                  pl.BlockSpec((D_MODEL, EC), lambda g: (0, item(g, 1)[1])),
                  pl.BlockSpec((PEER_HEADS, nk, TT), lambda g: (0, item(g, 0)[1], item(g, 0)[0])),
                  pl.BlockSpec((PEER_HEADS, nk, TT), lambda g: (0, item(g, 0)[1], item(g, 0)[0])),
                  pl.BlockSpec((PEER_HEADS, N_KEYS, TT), lambda g: (0, 0, item(g, 0)[0])),
                  pl.BlockSpec((PEER_HEADS, N_KEYS, TT), lambda g: (0, 0, item(g, 0)[0]))],
        out_specs=pl.BlockSpec((TT, D_MODEL), lambda g: (item(g, 1)[0], 0)),
        scratch_shapes=[pltpu.VMEM((D_MODEL, TT), F32), pltpu.VMEM((EC, TT), BF16),
                        pltpu.VMEM((2, EC, TT), BF16)],
        compiler_params=_cparams(("arbitrary",)),
        name="peer_dense",
    )(xn, h1, u_bf, vt_bf, rho, c1, r2, e2)


S_STACK = SWA_GROUP * DEC_SEQ
S_SMALL = N_META + DEC_SEQ
NT_DIMS = (((1,), (1,)), ((), ()))


def _sample_kernel(p_ref, cs_ref, rs_ref, meta_ref, win_ref,
                   cw_ref, cb_ref, cg_ref, cbeta_ref,
                   cos_ref, sin_ref, dmat_ref, dq_ref, dk_ref, sgam_ref, gng_ref, gnb_ref, eye_ref,
                   ones_ref, qn_ref, kn_ref, bwin_ref, bsm_ref, sink_ref,
                   conv_o_ref, ret_o_ref, swa_o_ref, ncs_ref, nrs_ref, nwin_ref,
                   xin_ref, kpad_ref, vpad_ref, ksm_ref, vsm_ref):
    kpad_ref[...] = jnp.zeros_like(kpad_ref)
    vpad_ref[...] = jnp.zeros_like(vpad_ref)
    ksm_ref[...] = jnp.zeros_like(ksm_ref)
    vsm_ref[...] = jnp.zeros_like(vsm_ref)
    lane = lax.broadcasted_iota(jnp.int32, (DEC_SEQ, RET_HEADS * RET_DK), 1)
    first_half = (lane % RET_DK) < (RET_DK // 2)
    cos = cos_ref[...]
    sin = sin_ref[...]
    ones_q = ones_ref[...]
    ones_k = ones_ref[0:128, 0:128]

    seqs = range(SG)
    rows = [slice(s * DEC_SEQ, (s + 1) * DEC_SEQ) for s in seqs]
    xin = [xin_ref.at[s] for s in seqs]
    kpad = [kpad_ref.at[s] for s in seqs]
    vpad = [vpad_ref.at[s] for s in seqs]
    ksm = [ksm_ref.at[s] for s in seqs]
    vsm = [vsm_ref.at[s] for s in seqs]

    for s in seqs:
        glu = (p_ref[rows[s], C_CONV:C_CONV + CONV_CH]
               * jax.nn.sigmoid(p_ref[rows[s], C_CONV + CONV_CH:C_RQ]))
        xin[s][0:CONV_WIDTH - 1, :] = cs_ref[0, s]
        xin[s][CONV_WIDTH - 1:CONV_WIDTH - 1 + DEC_SEQ, :] = glu
    acc = [jnp.zeros((DEC_SEQ, CONV_CH), F32) for _ in seqs]
    for k in range(CONV_WIDTH):
        wk = cw_ref[k:k + 1, :]
        for s in seqs:
            acc[s] = acc[s] + xin[s][k:k + DEC_SEQ, :] * wk
    for s in seqs:
        y = acc[s] + cb_ref[...]
        mu = jnp.mean(y, axis=-1, keepdims=True)
        d = y - mu
        var = jnp.mean(d * d, axis=-1, keepdims=True)
        yn = d * lax.rsqrt(var + EPS) * cg_ref[...] + cbeta_ref[...]
        conv_o_ref[rows[s], :] = yn * jax.nn.sigmoid(yn)
        ncs_ref[s] = xin[s][DEC_SEQ:DEC_SEQ + CONV_WIDTH - 1, :]

    qr = []
    for s in seqs:
        q = p_ref[rows[s], C_RQ:C_RK]
        k = p_ref[rows[s], C_RK:C_RV]
        qr.append(q * cos + _swap_halves(q, first_half) * sin)
        kpad[s][0:DEC_SEQ, :] = (k * cos + _swap_halves(k, first_half) * sin) * (RET_DK ** -0.5)
        vpad[s][0:DEC_SEQ, :] = p_ref[rows[s], C_RV:C_RG]
    pairs = [(s, h) for h in range(RET_HEADS) for s in seqs]
    inner, q_state, kdec_t = {}, {}, {}
    for s, h in pairs:
        qh = qr[s][:, h * RET_DK:(h + 1) * RET_DK]
        kp = kpad[s][:, h * RET_DK:(h + 1) * RET_DK]
        inner[s, h] = lax.dot_general(qh, kp, NT_DIMS, preferred_element_type=F32) * dmat_ref[h]
        q_state[s, h] = jnp.dot(qh, rs_ref[0, s, h], preferred_element_type=F32)
        kdec_t[s, h] = lax.dot_general(eye_ref[...], kp * dk_ref[h], NT_DIMS,
                                       preferred_element_type=F32)
    ret_raw = {}
    for s, h in pairs:
        vp = vpad[s][:, h * RET_DV:(h + 1) * RET_DV]
        ret_raw[s, h] = jnp.dot(inner[s, h], vp, preferred_element_type=F32) + dq_ref[h] * q_state[s, h]
        nrs_ref[s, h] = sgam_ref[h] * rs_ref[0, s, h] + jnp.dot(kdec_t[s, h], vp, preferred_element_type=F32)
    for s, h in pairs:
        o = ret_raw[s, h]
        mu = jnp.mean(o, axis=-1, keepdims=True)
        d = o - mu
        var = jnp.mean(d * d, axis=-1, keepdims=True)
        hs = slice(h * RET_DV, (h + 1) * RET_DV)
        yr = d * lax.rsqrt(var + EPS) * gng_ref[:, hs] + gnb_ref[:, hs]
        gate = p_ref[rows[s], C_RG + h * RET_DV:C_RG + (h + 1) * RET_DV]
        ret_o_ref[rows[s], hs] = yr * (gate * jax.nn.sigmoid(gate))

    qss = [jnp.dot(jnp.square(p_ref[rows[s], C_SQ:C_SK]), ones_q, preferred_element_type=F32) for s in seqs]
    kss = [jnp.dot(jnp.square(p_ref[rows[s], C_SK:C_SV]), ones_k, preferred_element_type=F32) for s in seqs]
    qa = []
    for s in seqs:
        qa.append(p_ref[rows[s], C_SQ:C_SK] * lax.rsqrt(qss[s] * (1.0 / SWA_HD) + EPS)
                  * qn_ref[...] * (SWA_HD ** -0.5))
        kn = p_ref[rows[s], C_SK:C_SV] * lax.rsqrt(kss[s] * (1.0 / SWA_HD) + EPS) * kn_ref[...]
        vn = p_ref[rows[s], C_SV:PROJ_COLS]
        ksm[s][0:N_META, :] = meta_ref[0, s, :, 0:128]
        vsm[s][0:N_META, :] = meta_ref[0, s, :, 128:256]
        ksm[s][N_META:S_SMALL, :] = kn
        vsm[s][N_META:S_SMALL, :] = vn
        nwin_ref[s, 0:WINDOW - DEC_SEQ, :] = win_ref[0, s, DEC_SEQ:WINDOW, :]
        nwin_ref[s, WINDOW - DEC_SEQ:WINDOW, 0:128] = kn
        nwin_ref[s, WINDOW - DEC_SEQ:WINDOW, 128:256] = vn
    spairs = [(s, kv) for kv in range(SWA_KV_HEADS) for s in seqs]
    s_w, s_s = {}, {}
    for s, kv in spairs:
        ksl = slice(kv * SWA_HD, (kv + 1) * SWA_HD)
        qs = jnp.concatenate([qa[s][:, (kv * SWA_GROUP + g) * SWA_HD:(kv * SWA_GROUP + g + 1) * SWA_HD]
                              for g in range(SWA_GROUP)], axis=0)
        s_w[s, kv] = lax.dot_general(qs, win_ref[0, s, :, ksl], NT_DIMS, preferred_element_type=F32) + bwin_ref[kv]
        s_s[s, kv] = lax.dot_general(qs, ksm[s][:, ksl], NT_DIMS, preferred_element_type=F32) + bsm_ref[kv]
    p_w, p_s, den = {}, {}, {}
    for s, kv in spairs:
        sink = sink_ref[kv, :, 0:1]
        m = jnp.maximum(jnp.maximum(jnp.max(s_w[s, kv], axis=-1, keepdims=True),
                                    jnp.max(s_s[s, kv], axis=-1, keepdims=True)), sink)
        p_w[s, kv] = jnp.exp(s_w[s, kv] - m)
        p_s[s, kv] = jnp.exp(s_s[s, kv] - m)
        den[s, kv] = (jnp.sum(p_w[s, kv], axis=-1, keepdims=True) + jnp.sum(p_s[s, kv], axis=-1, keepdims=True)
                      + jnp.exp(sink - m))
    for s, kv in spairs:
        ksl = slice(kv * SWA_HD, (kv + 1) * SWA_HD)
        vsl = slice(128 + kv * SWA_HD, 128 + (kv + 1) * SWA_HD)
        o = (jnp.dot(p_w[s, kv], win_ref[0, s, :, vsl], preferred_element_type=F32)
             + jnp.dot(p_s[s, kv], vsm[s][:, ksl], preferred_element_type=F32)) / den[s, kv]
        for g in range(SWA_GROUP):
            hh = kv * SWA_GROUP + g
            swa_o_ref[rows[s], hh * SWA_HD:(hh + 1) * SWA_HD] = o[g * DEC_SEQ:(g + 1) * DEC_SEQ, :]


def _sample_tables(rel_bias):
    lg = jnp.log(1.0 - 2.0 ** (-5.0 - jnp.arange(RET_HEADS, dtype=F32)))
    i = jnp.arange(DEC_SEQ, dtype=F32)
    diff = i[:, None] - i[None, :]
    dm = jnp.where(diff >= 0, jnp.exp(jnp.maximum(diff, 0.0)[None] * lg[:, None, None]), 0.0)
    dmat = jnp.zeros((RET_HEADS, DEC_SEQ, 128), F32).at[:, :, :DEC_SEQ].set(dm)
    dq = jnp.broadcast_to(jnp.exp((i[None] + 1.0) * lg[:, None])[:, :, None], (RET_HEADS, DEC_SEQ, RET_DV))
    kd = jnp.exp((DEC_SEQ - 1.0 - i)[None] * lg[:, None])
    dk = jnp.zeros((RET_HEADS, 128, RET_DK), F32).at[:, :DEC_SEQ, :].set(
        jnp.broadcast_to(kd[:, :, None], (RET_HEADS, DEC_SEQ, RET_DK)))
    sg = jnp.broadcast_to(jnp.exp(DEC_SEQ * lg)[:, None, None], (RET_HEADS, 1, RET_DV))
    cos_t, sin_t = _rotary_tables(PAST_LEN + jnp.arange(DEC_SEQ))

    ti = np.arange(S_STACK)[:, None] % DEC_SEQ
    j = np.arange(WINDOW)[None, :]
    bk_win = _t5_bucket_np(np.maximum(ti + WINDOW - j, 0))
    ok_win = j > ti
    c = np.arange(128)[None, :]
    jn = c - N_META
    bk_new = _t5_bucket_np(np.clip(ti - jn, 0, None))
    ok_sm = (c < N_META) | ((c < S_SMALL) & (jn <= ti))
    bk_sm = np.where(c < N_META, N_BUCKETS - 1, bk_new)
    rb = rel_bias.astype(F32)
    head = np.arange(SWA_KV_HEADS)[:, None] * SWA_GROUP + (np.arange(S_STACK) // DEC_SEQ)[None, :]
    head_oh = _one_hot(head, SWA_HEADS)

    def look(bk, ok):
        b = jnp.einsum("rcb,bh,krh->krc", _one_hot(bk, N_BUCKETS), rb, head_oh,
                       precision=lax.Precision.HIGHEST)
        return jnp.where(jnp.asarray(np.broadcast_to(ok[None], b.shape)), b, NEG)

    return dict(dmat=dmat, dq=dq, dk=dk, sg=sg, cos=cos_t, sin=sin_t,
                bwin=look(bk_win, ok_win), bsm=look(bk_sm, ok_sm), head_oh=head_oh)


def _sample_mixers(p, l, cache_meta_kv, cache_swa_kv, state_ret, state_conv, tabs, conv_w32, conv_b,
                   conv_g, conv_beta, gn_g, gn_b, q_norm, k_norm, sinks):
    row0 = NP_ROWS // (SG * DEC_SEQ)
    nrow = SG * DEC_SEQ
    ones_bd = jnp.asarray(np.kron(np.eye(SWA_HEADS), np.ones((SWA_HD, SWA_HD))), F32)
    qn = jnp.tile(q_norm.astype(F32), SWA_HEADS).reshape(1, -1)
    kn = jnp.tile(k_norm.astype(F32), SWA_KV_HEADS).reshape(1, -1)
    sink_rows = jnp.einsum("h,krh->kr", sinks.astype(F32), tabs["head_oh"], precision=lax.Precision.HIGHEST)
    sink_t = jnp.broadcast_to(sink_rows[:, :, None], (SWA_KV_HEADS, S_STACK, 128))
    meta = cache_meta_kv.reshape(DEPTH, DEC_BATCH, N_META, 256)
    win = cache_swa_kv.reshape(DEPTH, DEC_BATCH, WINDOW, 256)

    def const(shape):
        return pl.BlockSpec(shape, lambda i: (0,) * len(shape))

    return pl.pallas_call(
        _sample_kernel,
        out_shape=(jax.ShapeDtypeStruct((NS_ROWS, CONV_CH), F32),
                   jax.ShapeDtypeStruct((NS_ROWS, RET_HEADS * RET_DV), F32),
                   jax.ShapeDtypeStruct((NS_ROWS, SWA_HEADS * SWA_HD), F32),
                   jax.ShapeDtypeStruct((DEC_BATCH, CONV_WIDTH - 1, CONV_CH), F32),
                   jax.ShapeDtypeStruct((DEC_BATCH, RET_HEADS, RET_DK, RET_DV), F32),
                   jax.ShapeDtypeStruct((DEC_BATCH, WINDOW, 256), F32)),
        grid=(DEC_BATCH // SG,),
        in_specs=[pl.BlockSpec((nrow, PROJ_COLS), lambda i: (row0 + i, 0)),
                  pl.BlockSpec((1, SG, CONV_WIDTH - 1, CONV_CH), lambda i: (l, i, 0, 0)),
                  pl.BlockSpec((1, SG, RET_HEADS, RET_DK, RET_DV), lambda i: (l, i, 0, 0, 0)),
                  pl.BlockSpec((1, SG, N_META, 256), lambda i: (l, i, 0, 0)),
                  pl.BlockSpec((1, SG, WINDOW, 256), lambda i: (l, i, 0, 0)),
                  const((32, CONV_CH)), const((1, CONV_CH)), const((1, CONV_CH)), const((1, CONV_CH)),
                  const((DEC_SEQ, 256)), const((DEC_SEQ, 256)),
                  const((RET_HEADS, DEC_SEQ, 128)), const((RET_HEADS, DEC_SEQ, RET_DV)),
                  const((RET_HEADS, 128, RET_DK)), const((RET_HEADS, 1, RET_DV)),
                  const((1, 512)), const((1, 512)), const((RET_DK, RET_DK)),
                  const((512, 512)), const((1, 512)), const((1, 128)),
                  const((SWA_KV_HEADS, S_STACK, 128)), const((SWA_KV_HEADS, S_STACK, 128)),
                  const((SWA_KV_HEADS, S_STACK, 128))],
        out_specs=(pl.BlockSpec((nrow, CONV_CH), lambda i: (i, 0)),
                   pl.BlockSpec((nrow, 512), lambda i: (i, 0)),
                   pl.BlockSpec((nrow, 512), lambda i: (i, 0)),
                   pl.BlockSpec((SG, CONV_WIDTH - 1, CONV_CH), lambda i: (i, 0, 0)),
                   pl.BlockSpec((SG, RET_HEADS, RET_DK, RET_DV), lambda i: (i, 0, 0, 0)),
                   pl.BlockSpec((SG, WINDOW, 256), lambda i: (i, 0, 0))),
        scratch_shapes=[pltpu.VMEM((SG, 40, CONV_CH), F32), pltpu.VMEM((SG, 128, RET_HEADS * RET_DK), F32),
                        pltpu.VMEM((SG, 128, RET_HEADS * RET_DV), F32),
                        pltpu.VMEM((SG, 128, 128), F32), pltpu.VMEM((SG, 128, 128), F32)],
        compiler_params=_cparams(("parallel",)),
        name="sample_mixers",
    )(p, state_conv, state_ret, meta, win,
      conv_w32, conv_b.reshape(1, -1), conv_g.reshape(1, -1), conv_beta.reshape(1, -1),
      tabs["cos"], tabs["sin"], tabs["dmat"], tabs["dq"], tabs["dk"], tabs["sg"],
      gn_g.reshape(1, -1), gn_b.reshape(1, -1), jnp.eye(RET_DK, dtype=F32),
      ones_bd, qn, kn, tabs["bwin"], tabs["bsm"], sink_t)


def kernel(x_prompt, x_sample, cache_meta_kv, cache_swa_kv, state_ret, state_conv, meta_tokens, rel_bias,
           norm_mix, w_in, conv_w, conv_b, conv_ln_g, conv_ln_b, ret_gn_g, ret_gn_b, swa_q_norm,
           swa_k_norm, swa_sinks, w_out, norm_ffn, peer_wq, peer_keys, peer_u, peer_v):
    meta = jnp.broadcast_to(meta_tokens.astype(F32)[None], (BATCH, N_META, D_MODEL))
    pad = jnp.zeros((BATCH, LP - L_REAL, D_MODEL), F32)
    hp = jnp.concatenate([meta, x_prompt, pad], axis=1).reshape(NP_ROWS, D_MODEL)
    h = jnp.concatenate([hp, x_sample.reshape(NS_ROWS, D_MODEL)], axis=0)

    cos_t, sin_t = _rotary_tables(jnp.arange(LP))
    ret_tabs = _ret_tables()
    bias_tabs = _bias_tables(rel_bias)
    sample_tabs = _sample_tables(rel_bias)

    meta_p, win_p, ret_p, conv_p, win_s, ret_s, conv_s = [], [], [], [], [], [], []
    for l in range(DEPTH):
        p = _norm_proj(h, norm_mix[l], w_in[l].astype(BF16))

        w32 = jnp.concatenate([conv_w[l], jnp.zeros((1, CONV_CH), F32)], axis=0)
        conv_o, conv_tail = _conv_prompt(p, w32, conv_b[l], conv_ln_g[l], conv_ln_b[l])
        ret_o, ret_state = _ret_prompt(p, cos_t, sin_t, ret_tabs, ret_gn_g[l], ret_gn_b[l])
        swa_o, k_normed = _swa_prompt(p, swa_sinks[l], swa_q_norm[l], swa_k_norm[l], bias_tabs)
        s_conv, s_ret, s_swa, new_conv, new_ret, new_win = _sample_mixers(
            p, l, cache_meta_kv, cache_swa_kv, state_ret, state_conv, sample_tabs, w32, conv_b[l],
            conv_ln_g[l], conv_ln_b[l], ret_gn_g[l], ret_gn_b[l], swa_q_norm[l], swa_k_norm[l], swa_sinks[l])
        new_win = new_win.reshape(DEC_BATCH, WINDOW, 2, SWA_KV_HEADS, SWA_HD)

        kbd = jnp.zeros((2 * N_KEYS, 2 * PEER_HALF), F32)
        kbd = kbd.at[:N_KEYS, :PEER_HALF].set(peer_keys[l, 0]).at[N_KEYS:, PEER_HALF:].set(peer_keys[l, 1])
        h1, xn, st = _out_ffn(h, (conv_o, ret_o, swa_o), (s_conv, s_ret, s_swa), w_out[l].astype(BF16),
                              norm_ffn[l], peer_wq[l].astype(BF16), kbd.astype(BF16))
        rho, c1, r2, e2 = _topk(st)
        h = _peer(xn, h1, peer_u[l].astype(BF16), peer_v[l].T.astype(BF16), rho, c1, r2, e2)

        kp = k_normed.reshape(BATCH, LP, SWA_KV_HEADS, SWA_HD)
        vp = p[:NP_ROWS, C_SV:].reshape(BATCH, LP, SWA_KV_HEADS, SWA_HD)
        kvp = jnp.stack([kp, vp], axis=2)
        meta_p.append(kvp[:, :N_META])
        win_p.append(kvp[:, L_REAL - WINDOW:L_REAL])
        ret_p.append(ret_state)
        conv_p.append(conv_tail[:, 32 - (CONV_WIDTH - 1):])
        win_s.append(new_win)
        ret_s.append(new_ret)
        conv_s.append(new_conv)

    y_prompt = h[:NP_ROWS].reshape(BATCH, LP, D_MODEL)[:, N_META:L_REAL]
    y_sample = h[NP_ROWS:].reshape(DEC_BATCH, DEC_SEQ, D_MODEL)
    return (y_prompt, y_sample, jnp.stack(meta_p), jnp.stack(win_p), jnp.stack(ret_p), jnp.stack(conv_p),
            jnp.stack(win_s), jnp.stack(ret_s), jnp.stack(conv_s))
```

```python
import functools
import math

import numpy as np
import jax
import jax.numpy as jnp
from jax import lax
from jax.experimental import pallas as pl
from jax.experimental.pallas import tpu as pltpu

F32 = jnp.float32
BF16 = jnp.bfloat16

D_MODEL = 1024
BATCH = 2
SEQ = 8192
DEPTH = 2
DEC_BATCH = 128
DEC_SEQ = 8
PAST_LEN = 8192
N_META = 16
CONV_CH = 512
CONV_WIDTH = 31
RET_HEADS = 4
RET_DK = 64
RET_DV = 128
SWA_HEADS = 8
SWA_KV_HEADS = 2
SWA_GROUP = SWA_HEADS // SWA_KV_HEADS
SWA_HD = 64
WINDOW = 128
N_BUCKETS = 32
REL_MAX_DIST = 128
PEER_HEADS = 8
N_KEYS = 128
N_EXPERTS = N_KEYS * N_KEYS
PEER_TOPK = 16
PEER_HALF = 64
EPS = 1e-6
NEG = -1e30

PROJ_COLS = 3328
C_CONV, C_RQ, C_RK, C_RV, C_RG, C_SQ, C_SK, C_SV = 0, 1024, 1280, 1536, 2048, 2560, 3072, 3200

L_REAL = N_META + SEQ
BLK = 128
N_BLK = 65
LP = N_BLK * BLK
NP_ROWS = BATCH * LP
NS_ROWS = DEC_BATCH * DEC_SEQ
N_ROWS = NP_ROWS + NS_ROWS
LAST_REAL = L_REAL - (N_BLK - 1) * BLK

TM = 384
TO = 256
TT = 768
SG = 8
EC = 1024
PEER_LANES = 256
CONV_T = 640
CONV_RB = 64
VMEM_LIMIT = 56 * 1024 * 1024


def _cparams(sem, flags=None):
    return pltpu.CompilerParams(dimension_semantics=sem, vmem_limit_bytes=VMEM_LIMIT, flags=flags)


def _norm_proj_kernel(x_ref, g_ref, w_ref, o_ref):
    x = x_ref[...]
    ms = jnp.mean(x * x, axis=-1, keepdims=True)
    xn = x * lax.rsqrt(ms + EPS) * g_ref[...]
    o_ref[...] = jnp.dot(xn.astype(BF16), w_ref[...], preferred_element_type=F32)


def _norm_proj(h, g, w_bf, l):
    return pl.pallas_call(
        _norm_proj_kernel,
        out_shape=jax.ShapeDtypeStruct((N_ROWS, PROJ_COLS), F32),
        grid=(N_ROWS // TM,),
        in_specs=[pl.BlockSpec((TM, D_MODEL), lambda i: (i, 0)),
                  pl.BlockSpec((1, D_MODEL), lambda i: (0, 0)),
                  pl.BlockSpec((None, D_MODEL, PROJ_COLS), lambda i: (l, 0, 0))],
        out_specs=pl.BlockSpec((TM, PROJ_COLS), lambda i: (i, 0)),
        compiler_params=_cparams(("parallel",)),
        name="norm_proj",
    )(h, g.reshape(1, D_MODEL), w_bf)


def _conv_kernel(c_ref, w_ref, b_ref, g_ref, beta_ref, o_ref, st_ref, xin_ref, xs_ref):
    t = pl.program_id(1)

    @pl.when(t == 0)
    def _():
        xin_ref[0:32, :] = jnp.zeros((32, CONV_CH), F32)

    @pl.when(t > 0)
    def _():
        xin_ref[0:32, :] = xin_ref[CONV_T:CONV_T + 32, :]

    c = c_ref[...]
    xin_ref[32:32 + CONV_T, :] = c[:, :CONV_CH] * jax.nn.sigmoid(c[:, CONV_CH:])
    n_sh = 32 + CONV_T - 8
    for r in range(1, 8):
        xs_ref[r - 1, 0:n_sh, :] = xin_ref[r:r + n_sh, :]

    w = w_ref[...]
    bias = b_ref[...]
    gam = g_ref[...]
    beta = beta_ref[...]
    for rb in range(CONV_T // CONV_RB):
        r0 = rb * CONV_RB
        acc = jnp.zeros((CONV_RB, CONV_CH), F32)
        for k in range(CONV_WIDTH):
            sh = (2 + k) % 8
            lo = r0 + 2 + k - sh
            win = xin_ref[lo:lo + CONV_RB, :] if sh == 0 else xs_ref[sh - 1, lo:lo + CONV_RB, :]
            acc = acc + win * w[k:k + 1, :]
        y = acc + bias
        mu = jnp.mean(y, axis=-1, keepdims=True)
        d = y - mu
        var = jnp.mean(d * d, axis=-1, keepdims=True)
        yn = d * lax.rsqrt(var + EPS) * gam + beta
        o_ref[r0:r0 + CONV_RB, :] = yn * jax.nn.sigmoid(yn)

    @pl.when(t == pl.num_programs(1) - 1)
    def _():
        lo = 32 + (L_REAL - 32) - (LP - CONV_T)
        st_ref[0] = xin_ref[lo:lo + 32, :]


def _conv_prompt(p, w32, b, g, beta):
    nt = LP // CONV_T
    return pl.pallas_call(
        _conv_kernel,
        out_shape=(jax.ShapeDtypeStruct((NP_ROWS, CONV_CH), F32),
                   jax.ShapeDtypeStruct((BATCH, 32, CONV_CH), F32)),
        grid=(BATCH, nt),
        in_specs=[pl.BlockSpec((CONV_T, 2 * CONV_CH), lambda bi, t: (bi * nt + t, 0)),
                  pl.BlockSpec((32, CONV_CH), lambda bi, t: (0, 0)),
                  pl.BlockSpec((1, CONV_CH), lambda bi, t: (0, 0)),
                  pl.BlockSpec((1, CONV_CH), lambda bi, t: (0, 0)),
                  pl.BlockSpec((1, CONV_CH), lambda bi, t: (0, 0))],
        out_specs=(pl.BlockSpec((CONV_T, CONV_CH), lambda bi, t: (bi * nt + t, 0)),
                   pl.BlockSpec((1, 32, CONV_CH), lambda bi, t: (bi, 0, 0))),
        scratch_shapes=[pltpu.VMEM((32 + CONV_T, CONV_CH), F32),
                        pltpu.VMEM((7, 32 + CONV_T - 8, CONV_CH), F32)],
        compiler_params=_cparams(("arbitrary", "arbitrary")),
        name="conv_prompt",
    )(p, w32, b.reshape(1, -1), g.reshape(1, -1), beta.reshape(1, -1))


def _swap_halves(x, first_half):
    return jnp.where(first_half, pltpu.roll(x, x.shape[1] - 32, axis=1), pltpu.roll(x, 32, axis=1))


def _ret_kernel(q_ref, k_ref, v_ref, rg_ref, cos_ref, sin_ref, dmat_ref, dq_ref, dk_ref, sg_ref,
                gng_ref, gnb_ref, o_ref, st_ref, s_ref):
    j = pl.program_id(1)

    @pl.when(j == 0)
    def _():
        s_ref[...] = jnp.zeros_like(s_ref)

    cos = cos_ref[...]
    sin = sin_ref[...]
    lane = lax.broadcasted_iota(jnp.int32, (BLK, RET_HEADS * RET_DK), 1)
    first_half = (lane % RET_DK) < (RET_DK // 2)
    q = q_ref[...]
    k = k_ref[...]
    q = q * cos + _swap_halves(q, first_half) * sin
    k = (k * cos + _swap_halves(k, first_half) * sin) * (RET_DK ** -0.5)
    v = v_ref[...]
    rg = rg_ref[...]
    hs = range(RET_HEADS)
    qh = [q[:, h * RET_DK:(h + 1) * RET_DK].astype(BF16) for h in hs]
    vh = [v[:, h * RET_DV:(h + 1) * RET_DV].astype(BF16) for h in hs]
    inner, q_state, kdec_t = [], [], []
    for h in hs:
        kh = k[:, h * RET_DK:(h + 1) * RET_DK]
        inner.append(lax.dot_general(qh[h], kh.astype(BF16), (((1,), (1,)), ((), ())),
                                     preferred_element_type=F32) * dmat_ref[h])
        q_state.append(jnp.dot(qh[h], s_ref[h].astype(BF16), preferred_element_type=F32))
        kdec_t.append((kh * dk_ref[0, h]).T.astype(BF16))
    outs = []
    for h in hs:
        outs.append(jnp.dot(inner[h].astype(BF16), vh[h], preferred_element_type=F32) + dq_ref[h] * q_state[h])
        s_ref[h] = sg_ref[0, h, 0:1, :] * s_ref[h] + jnp.dot(kdec_t[h], vh[h], preferred_element_type=F32)
    for h in hs:
        o = outs[h]
        mu = jnp.mean(o, axis=-1, keepdims=True)
        d = o - mu
        var = jnp.mean(d * d, axis=-1, keepdims=True)
        y = d * lax.rsqrt(var + EPS) * gng_ref[:, h * RET_DV:(h + 1) * RET_DV] \
            + gnb_ref[:, h * RET_DV:(h + 1) * RET_DV]
        gate = rg[:, h * RET_DV:(h + 1) * RET_DV]
        o_ref[:, h * RET_DV:(h + 1) * RET_DV] = y * (gate * jax.nn.sigmoid(gate))

    @pl.when(j == pl.num_programs(1) - 1)
    def _():
        st_ref[0] = s_ref[...]


def _ret_tables():
    lg = jnp.log(1.0 - 2.0 ** (-5.0 - jnp.arange(RET_HEADS, dtype=F32)))
    i = jnp.arange(BLK, dtype=F32)
    diff = i[:, None] - i[None, :]
    dmat = jnp.where(diff >= 0, jnp.exp(jnp.maximum(diff, 0.0)[None] * lg[:, None, None]), 0.0)
    dq = jnp.broadcast_to(jnp.exp((i[None] + 1.0) * lg[:, None])[:, :, None], (RET_HEADS, BLK, RET_DV))

    def kdec(c_eff):
        e = jnp.where(i[None] < c_eff, jnp.exp((c_eff - 1.0 - i)[None] * lg[:, None]), 0.0)
        return jnp.broadcast_to(e[:, :, None], (RET_HEADS, BLK, RET_DK))

    def sgam(c_eff):
        return jnp.broadcast_to(jnp.exp(c_eff * lg)[:, None, None], (RET_HEADS, 8, RET_DV))

    dk = jnp.stack([kdec(float(BLK)), kdec(float(LAST_REAL))])
    sg = jnp.stack([sgam(float(BLK)), sgam(float(LAST_REAL))])
    return dmat, dq, dk, sg


def _rotary_tables(pos):
    half = RET_DK // 2
    inv = 1.0 / (10000.0 ** (jnp.arange(half, dtype=F32) / half))
    ang = pos.astype(F32)[:, None] * inv[None]
    cos, sin = jnp.cos(ang), jnp.sin(ang)
    cos_t = jnp.tile(jnp.concatenate([cos, cos], axis=-1), (1, RET_HEADS))
    sin_t = jnp.tile(jnp.concatenate([-sin, sin], axis=-1), (1, RET_HEADS))
    return cos_t, sin_t


def _ret_prompt(p, cos_t, sin_t, tabs, gn_g, gn_b):
    dmat, dq, dk, sg = tabs
    nb = N_BLK
    last = nb - 1
    return pl.pallas_call(
        _ret_kernel,
        out_shape=(jax.ShapeDtypeStruct((NP_ROWS, RET_HEADS * RET_DV), F32),
                   jax.ShapeDtypeStruct((BATCH, RET_HEADS, RET_DK, RET_DV), F32)),
        grid=(BATCH, nb),
        in_specs=[pl.BlockSpec((BLK, 256), lambda b, j: (b * nb + j, C_RQ // 256)),
                  pl.BlockSpec((BLK, 256), lambda b, j: (b * nb + j, C_RK // 256)),
                  pl.BlockSpec((BLK, 512), lambda b, j: (b * nb + j, C_RV // 512)),
                  pl.BlockSpec((BLK, 512), lambda b, j: (b * nb + j, C_RG // 512)),
                  pl.BlockSpec((BLK, 256), lambda b, j: (j, 0)),
                  pl.BlockSpec((BLK, 256), lambda b, j: (j, 0)),
                  pl.BlockSpec((RET_HEADS, BLK, BLK), lambda b, j: (0, 0, 0)),
                  pl.BlockSpec((RET_HEADS, BLK, RET_DV), lambda b, j: (0, 0, 0)),
                  pl.BlockSpec((1, RET_HEADS, BLK, RET_DK), lambda b, j: (j // last, 0, 0, 0)),
                  pl.BlockSpec((1, RET_HEADS, 8, RET_DV), lambda b, j: (j // last, 0, 0, 0)),
                  pl.BlockSpec((1, 512), lambda b, j: (0, 0)),
                  pl.BlockSpec((1, 512), lambda b, j: (0, 0))],
        out_specs=(pl.BlockSpec((BLK, 512), lambda b, j: (b * nb + j, 0)),
                   pl.BlockSpec((1, RET_HEADS, RET_DK, RET_DV), lambda b, j: (b, 0, 0, 0))),
        scratch_shapes=[pltpu.VMEM((RET_HEADS, RET_DK, RET_DV), F32)],
        compiler_params=_cparams(("arbitrary", "arbitrary")),
        name="ret_prompt",
    )(p, p, p, p, cos_t, sin_t, dmat, dq, dk, sg, gn_g.reshape(1, -1), gn_b.reshape(1, -1))


def _group_rms(x, ones_bd, w):
    x2 = x * x
    hi = x2.astype(BF16)
    lo = (x2 - hi.astype(F32)).astype(BF16)
    ss = (jnp.dot(hi, ones_bd, preferred_element_type=F32)
          + jnp.dot(lo, ones_bd, preferred_element_type=F32))
    return x * lax.rsqrt(ss * (1.0 / SWA_HD) + EPS) * w


def _swa_kernel(sinks_ref, q_ref, kc_ref, vc_ref, kp_ref, vp_ref, km_ref, vm_ref, ones_ref,
                qn_ref, kn_ref, bc_ref, bp_ref, bm_ref, o_ref, kout_ref):
    j = pl.program_id(1)
    ones_q = ones_ref[...]
    ones_k = ones_ref[0:128, 0:128]
    qw = qn_ref[...]
    kw = kn_ref[...]
    q = _group_rms(q_ref[...], ones_q, qw) * (SWA_HD ** -0.5)
    kc = _group_rms(kc_ref[...], ones_k, kw)
    kp = _group_rms(kp_ref[...], ones_k, kw)
    pad = jnp.zeros((BLK - N_META, 128), BF16)
    km = jnp.concatenate([_group_rms(km_ref[...], ones_k, kw).astype(BF16), pad], axis=0)
    kout_ref[...] = kc
    vc = vc_ref[...].astype(BF16)
    vp = vp_ref[...].astype(BF16)
    vm = jnp.concatenate([vm_ref[...].astype(BF16), pad], axis=0)

    rows = SWA_GROUP * BLK
    qi = lax.broadcasted_iota(jnp.int32, (rows, BLK), 0) % BLK
    kj = lax.broadcasted_iota(jnp.int32, (rows, BLK), 1)
    valid_c = jnp.where(kj <= qi, j * BLK + kj, -1) >= N_META
    valid_p = jnp.where(kj > qi, (j - 1) * BLK + kj, -1) >= N_META
    valid_m = jnp.where(kj < N_META, kj, LP) <= j * BLK + qi

    dn = (((1,), (1,)), ((), ()))
    kvs = range(SWA_KV_HEADS)
    heads = [range(kv * SWA_GROUP, (kv + 1) * SWA_GROUP) for kv in kvs]
    ksl = [slice(kv * SWA_HD, (kv + 1) * SWA_HD) for kv in kvs]
    s_c, s_p, s_m = [], [], []
    for kv in kvs:
        qs = jnp.concatenate([q[:, h * SWA_HD:(h + 1) * SWA_HD] for h in heads[kv]], axis=0).astype(BF16)
        s_c.append(lax.dot_general(qs, kc[:, ksl[kv]].astype(BF16), dn, preferred_element_type=F32))
        s_p.append(lax.dot_general(qs, kp[:, ksl[kv]].astype(BF16), dn, preferred_element_type=F32))
        s_m.append(lax.dot_general(qs, km[:, ksl[kv]], dn, preferred_element_type=F32))
    p_c, p_p, p_m, den = [], [], [], []
    for kv in kvs:
        sc = jnp.where(valid_c, s_c[kv] + bc_ref[kv], NEG)
        sp = jnp.where(valid_p, s_p[kv] + bp_ref[kv], NEG)
        sm = jnp.where(valid_m, s_m[kv] + bm_ref[0, kv], NEG)
        sink = jnp.concatenate([jnp.full((BLK, 1), sinks_ref[h], F32) for h in heads[kv]], axis=0)
        m = jnp.maximum(jnp.max(jnp.maximum(jnp.maximum(sc, sp), sm), axis=-1, keepdims=True), sink)
        p_c.append(jnp.exp(sc - m))
        p_p.append(jnp.exp(sp - m))
        p_m.append(jnp.exp(sm - m))
        den.append(jnp.sum(p_c[kv] + p_p[kv] + p_m[kv], axis=-1, keepdims=True) + jnp.exp(sink - m))
    for kv in kvs:
        acc = (jnp.dot(p_c[kv].astype(BF16), vc[:, ksl[kv]], preferred_element_type=F32)
               + jnp.dot(p_p[kv].astype(BF16), vp[:, ksl[kv]], preferred_element_type=F32)
               + jnp.dot(p_m[kv].astype(BF16), vm[:, ksl[kv]], preferred_element_type=F32))
        out = acc / den[kv]
        for g, h in enumerate(heads[kv]):
            o_ref[:, h * SWA_HD:(h + 1) * SWA_HD] = out[g * BLK:(g + 1) * BLK, :]


def _one_hot(idx, n):
    return jnp.asarray(np.asarray(idx)[..., None] == np.arange(n), F32)


def _t5_bucket_np(dist):
    max_exact = N_BUCKETS // 2
    df = np.maximum(dist, 1).astype(np.float64)
    large = max_exact + (np.log(df / max_exact) / math.log(REL_MAX_DIST / max_exact)
                         * (N_BUCKETS - max_exact)).astype(np.int64)
    large = np.minimum(large, N_BUCKETS - 1)
    return np.where(dist < max_exact, dist, large).astype(np.int32)


def _bias_tables(rel_bias):
    qi = np.arange(BLK)[:, None]
    kj = np.arange(BLK)[None, :]
    b_cur = _t5_bucket_np(np.maximum(qi - kj, 0))
    b_prev = _t5_bucket_np(np.maximum(qi - kj + BLK, 0))
    m = np.arange(N_META)[None, :]
    b_m0 = _t5_bucket_np(np.maximum(qi - m, 0))
    b_m1 = _t5_bucket_np(np.maximum(qi + BLK - m, 0))
    assert (b_m1 == N_BUCKETS - 1).all()
    rb = rel_bias.astype(F32)

    def look(bk):
        return jnp.einsum("...b,bh->h...", _one_hot(bk, N_BUCKETS), rb, precision=lax.Precision.HIGHEST)

    return look(b_cur), look(b_prev), jnp.stack([look(b_m0), look(b_m1)])


def _swa_prompt(p, sinks, q_norm, k_norm, bias_tabs):
    srows = SWA_GROUP * BLK
    b_cur = bias_tabs[0].reshape(SWA_KV_HEADS, srows, BLK)
    b_prev = bias_tabs[1].reshape(SWA_KV_HEADS, srows, BLK)
    b_meta = jnp.pad(bias_tabs[2].reshape(2, SWA_KV_HEADS, srows, N_META),
                     ((0, 0), (0, 0), (0, 0), (0, BLK - N_META)))
    nb = N_BLK
    ones_bd = jnp.asarray(np.kron(np.eye(SWA_HEADS), np.ones((SWA_HD, SWA_HD))), BF16)
    qn = jnp.tile(q_norm.astype(F32), SWA_HEADS).reshape(1, -1)
    kn = jnp.tile(k_norm.astype(F32), SWA_KV_HEADS).reshape(1, -1)
    ck, cv = C_SK // 128, C_SV // 128
    return pl.pallas_call(
        _swa_kernel,
        out_shape=(jax.ShapeDtypeStruct((NP_ROWS, SWA_HEADS * SWA_HD), F32),
                   jax.ShapeDtypeStruct((NP_ROWS, SWA_KV_HEADS * SWA_HD), F32)),
        grid=(BATCH, nb),
        in_specs=[pl.BlockSpec(memory_space=pltpu.SMEM),
                  pl.BlockSpec((BLK, 512), lambda b, j: (b * nb + j, C_SQ // 512)),
                  pl.BlockSpec((BLK, 128), lambda b, j: (b * nb + j, ck)),
                  pl.BlockSpec((BLK, 128), lambda b, j: (b * nb + j, cv)),
                  pl.BlockSpec((BLK, 128), lambda b, j: (b * nb + jnp.maximum(j - 1, 0), ck)),
                  pl.BlockSpec((BLK, 128), lambda b, j: (b * nb + jnp.maximum(j - 1, 0), cv)),
                  pl.BlockSpec((N_META, 128), lambda b, j: (b * (LP // N_META), ck)),
                  pl.BlockSpec((N_META, 128), lambda b, j: (b * (LP // N_META), cv)),
                  pl.BlockSpec((512, 512), lambda b, j: (0, 0)),
                  pl.BlockSpec((1, 512), lambda b, j: (0, 0)),
                  pl.BlockSpec((1, 128), lambda b, j: (0, 0)),
                  pl.BlockSpec((SWA_KV_HEADS, srows, BLK), lambda b, j: (0, 0, 0)),
                  pl.BlockSpec((SWA_KV_HEADS, srows, BLK), lambda b, j: (0, 0, 0)),
                  pl.BlockSpec((1, SWA_KV_HEADS, srows, BLK), lambda b, j: (jnp.minimum(j, 1), 0, 0, 0))],
        out_specs=(pl.BlockSpec((BLK, 512), lambda b, j: (b * nb + j, 0)),
                   pl.BlockSpec((BLK, 128), lambda b, j: (b * nb + j, 0))),
        compiler_params=_cparams(("parallel", "parallel")),
        name="swa_prompt",
    )(sinks.astype(F32), p, p, p, p, p, p, p, ones_bd, qn, kn, b_cur, b_prev, b_meta)


def _out_ffn_kernel(h_ref, a_ref, b_ref, c_ref, as_ref, bs_ref, cs_ref, wo_ref, g_ref, wq_ref, kbd_ref,
                    h1_ref, xn_ref, st_ref):
    is_prompt = pl.program_id(0) < NP_ROWS // TO
    a = jnp.where(is_prompt, a_ref[...], as_ref[...])
    b = jnp.where(is_prompt, b_ref[...], bs_ref[...])
    c = jnp.where(is_prompt, c_ref[...], cs_ref[...])
    h1 = (h_ref[...]
          + jnp.dot(a.astype(BF16), wo_ref[0:512, :], preferred_element_type=F32)
          + jnp.dot(b.astype(BF16), wo_ref[512:1024, :], preferred_element_type=F32)
          + jnp.dot(c.astype(BF16), wo_ref[1024:1536, :], preferred_element_type=F32))
    h1_ref[...] = h1
    ms = jnp.mean(h1 * h1, axis=-1, keepdims=True)
    xn_f = h1 * lax.rsqrt(ms + EPS) * g_ref[...]
    xn_ref[...] = xn_f.T.astype(BF16)
    q = jnp.dot(xn_f.astype(BF16), wq_ref[...], preferred_element_type=F32).astype(BF16)
    kbd = kbd_ref[...]
    for hh in range(PEER_HEADS):
        st_ref[hh] = lax.dot_general(kbd, q[:, hh * 128:(hh + 1) * 128], (((1,), (1,)), ((), ())),
                                     preferred_element_type=F32)


def _out_ffn(h, mix_p, mix_s, wo_bf, g, wq_bf, kbd_bf, l):
    npb = NP_ROWS // TO
    pspec = pl.BlockSpec((TO, 512), lambda i: (jnp.minimum(i, npb - 1), 0))
    sspec = pl.BlockSpec((TO, 512), lambda i: (jnp.maximum(i - npb, 0), 0))
    return pl.pallas_call(
        _out_ffn_kernel,
        out_shape=(jax.ShapeDtypeStruct((N_ROWS, D_MODEL), F32),
                   jax.ShapeDtypeStruct((D_MODEL, N_ROWS), BF16),
                   jax.ShapeDtypeStruct((PEER_HEADS, 2 * N_KEYS, N_ROWS), F32)),
        grid=(N_ROWS // TO,),
        in_specs=[pl.BlockSpec((TO, D_MODEL), lambda i: (i, 0)),
                  pspec, pspec, pspec, sspec, sspec, sspec,
                  pl.BlockSpec((None, 1536, D_MODEL), lambda i: (l, 0, 0)),
                  pl.BlockSpec((1, D_MODEL), lambda i: (0, 0)),
                  pl.BlockSpec((None, D_MODEL, D_MODEL), lambda i: (l, 0, 0)),
                  pl.BlockSpec((2 * N_KEYS, 128), lambda i: (0, 0))],
        out_specs=(pl.BlockSpec((TO, D_MODEL), lambda i: (i, 0)),
                   pl.BlockSpec((D_MODEL, TO), lambda i: (0, i)),
                   pl.BlockSpec((PEER_HEADS, 2 * N_KEYS, TO), lambda i: (0, 0, i))),
        compiler_params=_cparams(("parallel",)),
        name="out_ffn",
    )(h, *mix_p, *mix_s, wo_bf, g.reshape(1, -1), wq_bf, kbd_bf)


def _top_vals(x, n, with_rank=False):
    vals = []
    rank = jnp.full(x.shape, float(n), F32)
    for r in range(n):
        mx = jnp.max(x, axis=0, keepdims=True)
        vals.append(mx)
        hit = x == mx
        if with_rank:
            rank = jnp.where(hit, float(r), rank)
        x = jnp.where(hit, NEG, x)
    return (vals, rank) if with_rank else vals


def _top_vals_paired(x, n):
    half = x.shape[0] // 2
    hi = jnp.maximum(x[:half], x[half:])
    lo = jnp.minimum(x[:half], x[half:])
    vals = []
    for _ in range(n):
        mx = jnp.max(hi, axis=0, keepdims=True)
        vals.append(mx)
        hit = hi == mx
        hi = jnp.where(hit, lo, hi)
        lo = jnp.where(hit, NEG, lo)
    return vals


def _topk_kernel(st_ref, rho_ref, c1_ref, r2_ref, e2_ref):
    s1 = st_ref[0, 0:N_KEYS, :]
    s2 = st_ref[0, N_KEYS:2 * N_KEYS, :]
    v1 = _top_vals_paired(s1, PEER_TOPK)
    v2, rank2 = _top_vals(s2, PEER_TOPK, with_rank=True)
    sv1 = jnp.concatenate(v1, axis=0)
    sv2 = jnp.concatenate(v2, axis=0)
    cand = jnp.concatenate([v1[0] + sv2, v1[1] + sv2]
                           + [v1[a] + sv2[0:8] for a in range(2, 8)]
                           + [sv1[8:16] + v2[0], jnp.full((8, s1.shape[1]), NEG, F32)], axis=0)
    top = _top_vals_paired(cand, PEER_TOPK)
    tau = top[PEER_TOPK - 1]
    z = jnp.ones_like(tau)
    for r in range(1, PEER_TOPK):
        z = z + jnp.exp(top[r] - top[0])
    rho = jnp.zeros(s1.shape, F32)
    for a in range(PEER_TOPK):
        n_a = jnp.sum(jnp.where(v1[a] + sv2 >= tau, 1.0, 0.0), axis=0, keepdims=True)
        rho = jnp.where(s1 == v1[a], n_a, rho)
    rho_ref[0] = rho
    c1_ref[0] = jnp.exp(s1 - v1[0]) * (0.5 / z)
    r2_ref[0] = rank2.astype(BF16)
    e2_ref[0] = jnp.exp(s2 - v2[0]).astype(BF16)


def _topk(st):
    shp = jax.ShapeDtypeStruct((PEER_HEADS, N_KEYS, N_ROWS), F32)
    shp_bf = jax.ShapeDtypeStruct((PEER_HEADS, N_KEYS, N_ROWS), BF16)
    spec = pl.BlockSpec((1, N_KEYS, TM), lambda i, h: (h, 0, i))
    return pl.pallas_call(
        _topk_kernel,
        out_shape=(shp, shp, shp_bf, shp_bf),
        grid=(N_ROWS // TM, PEER_HEADS),
        in_specs=[pl.BlockSpec((1, 2 * N_KEYS, TM), lambda i, h: (h, 0, i))],
        out_specs=(spec, spec, spec, spec),
        compiler_params=_cparams(("parallel", "parallel")),
        name="peer_topk",
    )(st)


N_CHUNK = N_EXPERTS // EC
N_WORK = (N_ROWS // TT) * N_CHUNK


def _peer_stages(xn_ref, u_ref, vt_ref, rho_ref, c1_ref, r2_ref, e2_ref, acc_ref,
                 at_w, at_r, gt_w, gt_r):
    sub = 16
    n_tb = TT // PEER_LANES
    n_ii = EC // N_KEYS

    def gate_build(ii, tb, jbs):
        cols = slice(tb * PEER_LANES, (tb + 1) * PEER_LANES)
        w = {jb: jnp.zeros((sub, PEER_LANES), BF16) for jb in jbs}
        for hh in range(PEER_HEADS):
            rho = jnp.broadcast_to(rho_ref[hh, ii:ii + 1, cols], (sub, PEER_LANES)).astype(BF16)
            cc = jnp.broadcast_to(c1_ref[hh, ii:ii + 1, cols], (sub, PEER_LANES)).astype(BF16)
            for jb in jbs:
                jr = slice(jb * sub, (jb + 1) * sub)
                w[jb] = w[jb] + jnp.where(r2_ref[hh, jr, cols] < rho, e2_ref[hh, jr, cols] * cc, 0.0)
        for jb in jbs:
            rows = slice(ii * N_KEYS + jb * sub, ii * N_KEYS + (jb + 1) * sub)
            a = at_r[rows, cols]
            gt_w[rows, cols] = w[jb] * (a * (1.0 + lax.erf(a * (2.0 ** -0.5))))

    at_w[...] = jnp.dot(u_ref[...], xn_ref[...], preferred_element_type=F32).astype(BF16)
    acc_ref[...] += jnp.dot(vt_ref[...], gt_r[...], preferred_element_type=F32)
    all_jb = range(N_KEYS // sub)
    for ii in range(n_ii):
        for tb in range(n_tb):
            gate_build(ii, tb, all_jb)


def _peer_kernel(xn_ref, h1_ref, u_ref, vt_ref, rho_ref, c1_ref, r2_ref, e2_ref, o_ref,
                 acc_ref, at, gt):
    g = pl.program_id(0)
    cv = (g - 1) % N_CHUNK
    slot = g % 2

    @pl.when(g == 0)
    def _():
        gt[1] = jnp.zeros((EC, TT), BF16)

    @pl.when(jnp.logical_or(g < 1, cv == 0))
    def _():
        acc_ref[...] = jnp.zeros_like(acc_ref)

    _peer_stages(xn_ref, u_ref, vt_ref, rho_ref, c1_ref, r2_ref, e2_ref, acc_ref,
                 at, at, gt.at[slot], gt.at[1 - slot])

    @pl.when(jnp.logical_and(g >= 1, cv == N_CHUNK - 1))
    def _():
        o_ref[...] = h1_ref[...] + acc_ref[...].T


def _peer(xn, h1, u_bf, vt_bf, rho, c1, r2, e2, l):
    nk = EC // N_KEYS

    def item(g, lag):
        w = jnp.clip(g - lag, 0, N_WORK - 1)
        return w // N_CHUNK, w % N_CHUNK

    return pl.pallas_call(
        _peer_kernel,
        out_shape=jax.ShapeDtypeStruct((N_ROWS, D_MODEL), F32),
        grid=(N_WORK + 1,),
        in_specs=[pl.BlockSpec((D_MODEL, TT), lambda g: (0, item(g, 0)[0])),
                  pl.BlockSpec((TT, D_MODEL), lambda g: (item(g, 1)[0], 0)),
                  pl.BlockSpec((None, EC, D_MODEL), lambda g: (l, item(g, 0)[1], 0)),
                  pl.BlockSpec((None, D_MODEL, EC), lambda g: (l, 0, item(g, 1)[1])),
                  pl.BlockSpec((PEER_HEADS, nk, TT), lambda g: (0, item(g, 0)[1], item(g, 0)[0])),
                  pl.BlockSpec((PEER_HEADS, nk, TT), lambda g: (0, item(g, 0)[1], item(g, 0)[0])),
                  pl.BlockSpec((PEER_HEADS, N_KEYS, TT), lambda g: (0, 0, item(g, 0)[0])),
                  pl.BlockSpec((PEER_HEADS, N_KEYS, TT), lambda g: (0, 0, item(g, 0)[0]))],
        out_specs=pl.BlockSpec((TT, D_MODEL), lambda g: (item(g, 1)[0], 0)),
        scratch_shapes=[pltpu.VMEM((D_MODEL, TT), F32), pltpu.VMEM((EC, TT), BF16),
                        pltpu.VMEM((2, EC, TT), BF16)],
        compiler_params=_cparams(("arbitrary",)),
        name="peer_dense",
    )(xn, h1, u_bf, vt_bf, rho, c1, r2, e2)


S_STACK = SWA_GROUP * DEC_SEQ
S_SMALL = N_META + DEC_SEQ
NT_DIMS = (((1,), (1,)), ((), ()))


def _sample_kernel(p_ref, cs_ref, rs_ref, meta_ref, win_ref,
                   cw_ref, cb_ref, cg_ref, cbeta_ref,
                   cos_ref, sin_ref, dmat_ref, dq_ref, dk_ref, sgam_ref, gng_ref, gnb_ref, eye_ref,
                   ones_ref, qn_ref, kn_ref, bwin_ref, bsm_ref, sink_ref,
                   conv_o_ref, ret_o_ref, swa_o_ref, ncs_ref, nrs_ref, nwin_ref,
                   xin_ref, kpad_ref, vpad_ref, ksm_ref, vsm_ref):
    kpad_ref[...] = jnp.zeros_like(kpad_ref)
    vpad_ref[...] = jnp.zeros_like(vpad_ref)
    ksm_ref[...] = jnp.zeros_like(ksm_ref)
    vsm_ref[...] = jnp.zeros_like(vsm_ref)
    lane = lax.broadcasted_iota(jnp.int32, (DEC_SEQ, RET_HEADS * RET_DK), 1)
    first_half = (lane % RET_DK) < (RET_DK // 2)
    cos = cos_ref[...]
    sin = sin_ref[...]
    ones_q = ones_ref[...]
    ones_k = ones_ref[0:128, 0:128]

    seqs = range(SG)
    rows = [slice(s * DEC_SEQ, (s + 1) * DEC_SEQ) for s in seqs]
    xin = [xin_ref.at[s] for s in seqs]
    kpad = [kpad_ref.at[s] for s in seqs]
    vpad = [vpad_ref.at[s] for s in seqs]
    ksm = [ksm_ref.at[s] for s in seqs]
    vsm = [vsm_ref.at[s] for s in seqs]

    for s in seqs:
        glu = (p_ref[rows[s], C_CONV:C_CONV + CONV_CH]
               * jax.nn.sigmoid(p_ref[rows[s], C_CONV + CONV_CH:C_RQ]))
        xin[s][0:CONV_WIDTH - 1, :] = cs_ref[0, s]
        xin[s][CONV_WIDTH - 1:CONV_WIDTH - 1 + DEC_SEQ, :] = glu
    acc = [jnp.zeros((DEC_SEQ, CONV_CH), F32) for _ in seqs]
    for k in range(CONV_WIDTH):
        wk = cw_ref[k:k + 1, :]
        for s in seqs:
            acc[s] = acc[s] + xin[s][k:k + DEC_SEQ, :] * wk
    for s in seqs:
        y = acc[s] + cb_ref[...]
        mu = jnp.mean(y, axis=-1, keepdims=True)
        d = y - mu
        var = jnp.mean(d * d, axis=-1, keepdims=True)
        yn = d * lax.rsqrt(var + EPS) * cg_ref[...] + cbeta_ref[...]
        conv_o_ref[rows[s], :] = yn * jax.nn.sigmoid(yn)
        ncs_ref[s] = xin[s][DEC_SEQ:DEC_SEQ + CONV_WIDTH - 1, :]

    qr = []
    for s in seqs:
        q = p_ref[rows[s], C_RQ:C_RK]
        k = p_ref[rows[s], C_RK:C_RV]
        qr.append(q * cos + _swap_halves(q, first_half) * sin)
        kpad[s][0:DEC_SEQ, :] = (k * cos + _swap_halves(k, first_half) * sin) * (RET_DK ** -0.5)
        vpad[s][0:DEC_SEQ, :] = p_ref[rows[s], C_RV:C_RG]
    pairs = [(s, h) for h in range(RET_HEADS) for s in seqs]
    inner, q_state, kdec_t = {}, {}, {}
    for s, h in pairs:
        qh = qr[s][:, h * RET_DK:(h + 1) * RET_DK]
        kp = kpad[s][:, h * RET_DK:(h + 1) * RET_DK]
        inner[s, h] = lax.dot_general(qh, kp, NT_DIMS, preferred_element_type=F32) * dmat_ref[h]
        q_state[s, h] = jnp.dot(qh, rs_ref[0, s, h], preferred_element_type=F32)
        kdec_t[s, h] = lax.dot_general(eye_ref[...], kp * dk_ref[h], NT_DIMS,
                                       preferred_element_type=F32)
    ret_raw = {}
    for s, h in pairs:
        vp = vpad[s][:, h * RET_DV:(h + 1) * RET_DV]
        ret_raw[s, h] = jnp.dot(inner[s, h], vp, preferred_element_type=F32) + dq_ref[h] * q_state[s, h]
        nrs_ref[s, h] = sgam_ref[h] * rs_ref[0, s, h] + jnp.dot(kdec_t[s, h], vp, preferred_element_type=F32)
    for s, h in pairs:
        o = ret_raw[s, h]
        mu = jnp.mean(o, axis=-1, keepdims=True)
        d = o - mu
        var = jnp.mean(d * d, axis=-1, keepdims=True)
        hs = slice(h * RET_DV, (h + 1) * RET_DV)
        yr = d * lax.rsqrt(var + EPS) * gng_ref[:, hs] + gnb_ref[:, hs]
        gate = p_ref[rows[s], C_RG + h * RET_DV:C_RG + (h + 1) * RET_DV]
        ret_o_ref[rows[s], hs] = yr * (gate * jax.nn.sigmoid(gate))

    qss = [jnp.dot(jnp.square(p_ref[rows[s], C_SQ:C_SK]), ones_q, preferred_element_type=F32) for s in seqs]
    kss = [jnp.dot(jnp.square(p_ref[rows[s], C_SK:C_SV]), ones_k, preferred_element_type=F32) for s in seqs]
    qa = []
    for s in seqs:
        qa.append(p_ref[rows[s], C_SQ:C_SK] * lax.rsqrt(qss[s] * (1.0 / SWA_HD) + EPS)
                  * qn_ref[...] * (SWA_HD ** -0.5))
        kn = p_ref[rows[s], C_SK:C_SV] * lax.rsqrt(kss[s] * (1.0 / SWA_HD) + EPS) * kn_ref[...]
        vn = p_ref[rows[s], C_SV:PROJ_COLS]
        ksm[s][0:N_META, :] = meta_ref[0, s, :, 0:128]
        vsm[s][0:N_META, :] = meta_ref[0, s, :, 128:256]
        ksm[s][N_META:S_SMALL, :] = kn
        vsm[s][N_META:S_SMALL, :] = vn
        nwin_ref[s, 0:WINDOW - DEC_SEQ, :] = win_ref[0, s, DEC_SEQ:WINDOW, :]
        nwin_ref[s, WINDOW - DEC_SEQ:WINDOW, 0:128] = kn
        nwin_ref[s, WINDOW - DEC_SEQ:WINDOW, 128:256] = vn
    spairs = [(s, kv) for kv in range(SWA_KV_HEADS) for s in seqs]
    s_w, s_s = {}, {}
    for s, kv in spairs:
        ksl = slice(kv * SWA_HD, (kv + 1) * SWA_HD)
        qs = jnp.concatenate([qa[s][:, (kv * SWA_GROUP + g) * SWA_HD:(kv * SWA_GROUP + g + 1) * SWA_HD]
                              for g in range(SWA_GROUP)], axis=0)
        s_w[s, kv] = lax.dot_general(qs, win_ref[0, s, :, ksl], NT_DIMS, preferred_element_type=F32) + bwin_ref[kv]
        s_s[s, kv] = lax.dot_general(qs, ksm[s][:, ksl], NT_DIMS, preferred_element_type=F32) + bsm_ref[kv]
    p_w, p_s, den = {}, {}, {}
    for s, kv in spairs:
        sink = sink_ref[kv, :, 0:1]
        m = jnp.maximum(jnp.maximum(jnp.max(s_w[s, kv], axis=-1, keepdims=True),
                                    jnp.max(s_s[s, kv], axis=-1, keepdims=True)), sink)
        p_w[s, kv] = jnp.exp(s_w[s, kv] - m)
        p_s[s, kv] = jnp.exp(s_s[s, kv] - m)
        den[s, kv] = (jnp.sum(p_w[s, kv], axis=-1, keepdims=True) + jnp.sum(p_s[s, kv], axis=-1, keepdims=True)
                      + jnp.exp(sink - m))
    for s, kv in spairs:
        ksl = slice(kv * SWA_HD, (kv + 1) * SWA_HD)
        vsl = slice(128 + kv * SWA_HD, 128 + (kv + 1) * SWA_HD)
        o = (jnp.dot(p_w[s, kv], win_ref[0, s, :, vsl], preferred_element_type=F32)
             + jnp.dot(p_s[s, kv], vsm[s][:, ksl], preferred_element_type=F32)) / den[s, kv]
        for g in range(SWA_GROUP):
            hh = kv * SWA_GROUP + g
            swa_o_ref[rows[s], hh * SWA_HD:(hh + 1) * SWA_HD] = o[g * DEC_SEQ:(g + 1) * DEC_SEQ, :]


def _sample_tables(rel_bias):
    lg = jnp.log(1.0 - 2.0 ** (-5.0 - jnp.arange(RET_HEADS, dtype=F32)))
    i = jnp.arange(DEC_SEQ, dtype=F32)
    diff = i[:, None] - i[None, :]
    dm = jnp.where(diff >= 0, jnp.exp(jnp.maximum(diff, 0.0)[None] * lg[:, None, None]), 0.0)
    dmat = jnp.zeros((RET_HEADS, DEC_SEQ, 128), F32).at[:, :, :DEC_SEQ].set(dm)
    dq = jnp.broadcast_to(jnp.exp((i[None] + 1.0) * lg[:, None])[:, :, None], (RET_HEADS, DEC_SEQ, RET_DV))
    kd = jnp.exp((DEC_SEQ - 1.0 - i)[None] * lg[:, None])
    dk = jnp.zeros((RET_HEADS, 128, RET_DK), F32).at[:, :DEC_SEQ, :].set(
        jnp.broadcast_to(kd[:, :, None], (RET_HEADS, DEC_SEQ, RET_DK)))
    sg = jnp.broadcast_to(jnp.exp(DEC_SEQ * lg)[:, None, None], (RET_HEADS, 1, RET_DV))
    cos_t, sin_t = _rotary_tables(PAST_LEN + jnp.arange(DEC_SEQ))

    ti = np.arange(S_STACK)[:, None] % DEC_SEQ
    j = np.arange(WINDOW)[None, :]
    bk_win = _t5_bucket_np(np.maximum(ti + WINDOW - j, 0))
    ok_win = j > ti
    c = np.arange(128)[None, :]
    jn = c - N_META
    bk_new = _t5_bucket_np(np.clip(ti - jn, 0, None))
    ok_sm = (c < N_META) | ((c < S_SMALL) & (jn <= ti))
    bk_sm = np.where(c < N_META, N_BUCKETS - 1, bk_new)
    rb = rel_bias.astype(F32)
    head = np.arange(SWA_KV_HEADS)[:, None] * SWA_GROUP + (np.arange(S_STACK) // DEC_SEQ)[None, :]
    head_oh = _one_hot(head, SWA_HEADS)

    def look(bk, ok):
        b = jnp.einsum("rcb,bh,krh->krc", _one_hot(bk, N_BUCKETS), rb, head_oh,
                       precision=lax.Precision.HIGHEST)
        return jnp.where(jnp.asarray(np.broadcast_to(ok[None], b.shape)), b, NEG)

    return dict(dmat=dmat, dq=dq, dk=dk, sg=sg, cos=cos_t, sin=sin_t,
                bwin=look(bk_win, ok_win), bsm=look(bk_sm, ok_sm), head_oh=head_oh)


def _sample_mixers(p, l, cache_meta_kv, cache_swa_kv, state_ret, state_conv, tabs, conv_w32, conv_b,
                   conv_g, conv_beta, gn_g, gn_b, q_norm, k_norm, sinks):
    row0 = NP_ROWS // (SG * DEC_SEQ)
    nrow = SG * DEC_SEQ
    ones_bd = jnp.asarray(np.kron(np.eye(SWA_HEADS), np.ones((SWA_HD, SWA_HD))), F32)
    qn = jnp.tile(q_norm.astype(F32), SWA_HEADS).reshape(1, -1)
    kn = jnp.tile(k_norm.astype(F32), SWA_KV_HEADS).reshape(1, -1)
    sink_rows = jnp.einsum("h,krh->kr", sinks.astype(F32), tabs["head_oh"], precision=lax.Precision.HIGHEST)
    sink_t = jnp.broadcast_to(sink_rows[:, :, None], (SWA_KV_HEADS, S_STACK, 128))
    meta = cache_meta_kv.reshape(DEPTH, DEC_BATCH, N_META, 256)
    win = cache_swa_kv.reshape(DEPTH, DEC_BATCH, WINDOW, 256)

    def const(shape):
        return pl.BlockSpec(shape, lambda i: (0,) * len(shape))

    return pl.pallas_call(
        _sample_kernel,
        out_shape=(jax.ShapeDtypeStruct((NS_ROWS, CONV_CH), F32),
                   jax.ShapeDtypeStruct((NS_ROWS, RET_HEADS * RET_DV), F32),
                   jax.ShapeDtypeStruct((NS_ROWS, SWA_HEADS * SWA_HD), F32),
                   jax.ShapeDtypeStruct((DEC_BATCH, CONV_WIDTH - 1, CONV_CH), F32),
                   jax.ShapeDtypeStruct((DEC_BATCH, RET_HEADS, RET_DK, RET_DV), F32),
                   jax.ShapeDtypeStruct((DEC_BATCH, WINDOW, 256), F32)),
        grid=(DEC_BATCH // SG,),
        in_specs=[pl.BlockSpec((nrow, PROJ_COLS), lambda i: (row0 + i, 0)),
                  pl.BlockSpec((1, SG, CONV_WIDTH - 1, CONV_CH), lambda i: (l, i, 0, 0)),
                  pl.BlockSpec((1, SG, RET_HEADS, RET_DK, RET_DV), lambda i: (l, i, 0, 0, 0)),
                  pl.BlockSpec((1, SG, N_META, 256), lambda i: (l, i, 0, 0)),
                  pl.BlockSpec((1, SG, WINDOW, 256), lambda i: (l, i, 0, 0)),
                  const((32, CONV_CH)), const((1, CONV_CH)), const((1, CONV_CH)), const((1, CONV_CH)),
                  const((DEC_SEQ, 256)), const((DEC_SEQ, 256)),
                  const((RET_HEADS, DEC_SEQ, 128)), const((RET_HEADS, DEC_SEQ, RET_DV)),
                  const((RET_HEADS, 128, RET_DK)), const((RET_HEADS, 1, RET_DV)),
                  const((1, 512)), const((1, 512)), const((RET_DK, RET_DK)),
                  const((512, 512)), const((1, 512)), const((1, 128)),
                  const((SWA_KV_HEADS, S_STACK, 128)), const((SWA_KV_HEADS, S_STACK, 128)),
                  const((SWA_KV_HEADS, S_STACK, 128))],
        out_specs=(pl.BlockSpec((nrow, CONV_CH), lambda i: (i, 0)),
                   pl.BlockSpec((nrow, 512), lambda i: (i, 0)),
                   pl.BlockSpec((nrow, 512), lambda i: (i, 0)),
                   pl.BlockSpec((SG, CONV_WIDTH - 1, CONV_CH), lambda i: (i, 0, 0)),
                   pl.BlockSpec((SG, RET_HEADS, RET_DK, RET_DV), lambda i: (i, 0, 0, 0)),
                   pl.BlockSpec((SG, WINDOW, 256), lambda i: (i, 0, 0))),
        scratch_shapes=[pltpu.VMEM((SG, 40, CONV_CH), F32), pltpu.VMEM((SG, 128, RET_HEADS * RET_DK), F32),
                        pltpu.VMEM((SG, 128, RET_HEADS * RET_DV), F32),
                        pltpu.VMEM((SG, 128, 128), F32), pltpu.VMEM((SG, 128, 128), F32)],
        compiler_params=_cparams(("parallel",)),
        name="sample_mixers",
    )(p, state_conv, state_ret, meta, win,
      conv_w32, conv_b.reshape(1, -1), conv_g.reshape(1, -1), conv_beta.reshape(1, -1),
      tabs["cos"], tabs["sin"], tabs["dmat"], tabs["dq"], tabs["dk"], tabs["sg"],
      gn_g.reshape(1, -1), gn_b.reshape(1, -1), jnp.eye(RET_DK, dtype=F32),
      ones_bd, qn, kn, tabs["bwin"], tabs["bsm"], sink_t)


def kernel(x_prompt, x_sample, cache_meta_kv, cache_swa_kv, state_ret, state_conv, meta_tokens, rel_bias,
           norm_mix, w_in, conv_w, conv_b, conv_ln_g, conv_ln_b, ret_gn_g, ret_gn_b, swa_q_norm,
           swa_k_norm, swa_sinks, w_out, norm_ffn, peer_wq, peer_keys, peer_u, peer_v):
    meta = jnp.broadcast_to(meta_tokens.astype(F32)[None], (BATCH, N_META, D_MODEL))
    pad = jnp.zeros((BATCH, LP - L_REAL, D_MODEL), F32)
    hp = jnp.concatenate([meta, x_prompt, pad], axis=1).reshape(NP_ROWS, D_MODEL)
    h = jnp.concatenate([hp, x_sample.reshape(NS_ROWS, D_MODEL)], axis=0)

    cos_t, sin_t = _rotary_tables(jnp.arange(LP))
    ret_tabs = _ret_tables()
    bias_tabs = _bias_tables(rel_bias)
    sample_tabs = _sample_tables(rel_bias)

    w_in_bf, w_out_bf, wq_bf = w_in.astype(BF16), w_out.astype(BF16), peer_wq.astype(BF16)
    u_bf = peer_u.astype(BF16)
    vt_bf = jnp.swapaxes(peer_v, 1, 2).astype(BF16)

    meta_p, win_p, ret_p, conv_p, win_s, ret_s, conv_s = [], [], [], [], [], [], []
    for l in range(DEPTH):
        p = _norm_proj(h, norm_mix[l], w_in_bf, l)

        w32 = jnp.concatenate([conv_w[l], jnp.zeros((1, CONV_CH), F32)], axis=0)
        conv_o, conv_tail = _conv_prompt(p, w32, conv_b[l], conv_ln_g[l], conv_ln_b[l])
        ret_o, ret_state = _ret_prompt(p, cos_t, sin_t, ret_tabs, ret_gn_g[l], ret_gn_b[l])
        swa_o, k_normed = _swa_prompt(p, swa_sinks[l], swa_q_norm[l], swa_k_norm[l], bias_tabs)
        s_conv, s_ret, s_swa, new_conv, new_ret, new_win = _sample_mixers(
            p, l, cache_meta_kv, cache_swa_kv, state_ret, state_conv, sample_tabs, w32, conv_b[l],
            conv_ln_g[l], conv_ln_b[l], ret_gn_g[l], ret_gn_b[l], swa_q_norm[l], swa_k_norm[l], swa_sinks[l])
        new_win = new_win.reshape(DEC_BATCH, WINDOW, 2, SWA_KV_HEADS, SWA_HD)

        kbd = jnp.zeros((2 * N_KEYS, 2 * PEER_HALF), F32)
        kbd = kbd.at[:N_KEYS, :PEER_HALF].set(peer_keys[l, 0]).at[N_KEYS:, PEER_HALF:].set(peer_keys[l, 1])
        h1, xn, st = _out_ffn(h, (conv_o, ret_o, swa_o), (s_conv, s_ret, s_swa), w_out_bf,
                              norm_ffn[l], wq_bf, kbd.astype(BF16), l)
        rho, c1, r2, e2 = _topk(st)
        h = _peer(xn, h1, u_bf, vt_bf, rho, c1, r2, e2, l)

        kp = k_normed.reshape(BATCH, LP, SWA_KV_HEADS, SWA_HD)
        pp = p[:NP_ROWS].reshape(BATCH, LP, PROJ_COLS)

        def kv_rows(lo, hi):
            vrows = pp[:, lo:hi, C_SV:].reshape(BATCH, hi - lo, SWA_KV_HEADS, SWA_HD)
            return jnp.stack([kp[:, lo:hi], vrows], axis=2)

        meta_p.append(kv_rows(0, N_META))
        win_p.append(kv_rows(L_REAL - WINDOW, L_REAL))
        ret_p.append(ret_state)
        conv_p.append(conv_tail[:, 32 - (CONV_WIDTH - 1):])
        win_s.append(new_win)
        ret_s.append(new_ret)
        conv_s.append(new_conv)

    y_prompt = h[:NP_ROWS].reshape(BATCH, LP, D_MODEL)[:, N_META:L_REAL]
    y_sample = h[NP_ROWS:].reshape(DEC_BATCH, DEC_SEQ, D_MODEL)
    return (y_prompt, y_sample, jnp.stack(meta_p), jnp.stack(win_p), jnp.stack(ret_p), jnp.stack(conv_p),
            jnp.stack(win_s), jnp.stack(ret_s), jnp.stack(conv_s))
```

```python
import functools
import math

import numpy as np
import jax
import jax.numpy as jnp
from jax import lax
from jax.experimental import pallas as pl
from jax.experimental.pallas import tpu as pltpu

F32 = jnp.float32
BF16 = jnp.bfloat16

D_MODEL = 1024
BATCH = 2
SEQ = 8192
DEPTH = 2
DEC_BATCH = 128
DEC_SEQ = 8
PAST_LEN = 8192
N_META = 16
CONV_CH = 512
CONV_WIDTH = 31
RET_HEADS = 4
RET_DK = 64
RET_DV = 128
SWA_HEADS = 8
SWA_KV_HEADS = 2
SWA_GROUP = SWA_HEADS // SWA_KV_HEADS
SWA_HD = 64
WINDOW = 128
N_BUCKETS = 32
REL_MAX_DIST = 128
PEER_HEADS = 8
N_KEYS = 128
N_EXPERTS = N_KEYS * N_KEYS
PEER_TOPK = 16
PEER_HALF = 64
EPS = 1e-6
NEG = -1e30

PROJ_COLS = 3328
C_CONV, C_RQ, C_RK, C_RV, C_RG, C_SQ, C_SK, C_SV = 0, 1024, 1280, 1536, 2048, 2560, 3072, 3200

L_REAL = N_META + SEQ
BLK = 128
N_BLK = 65
LP = N_BLK * BLK
NP_ROWS = BATCH * LP
NS_ROWS = DEC_BATCH * DEC_SEQ
N_ROWS = NP_ROWS + NS_ROWS
LAST_REAL = L_REAL - (N_BLK - 1) * BLK

TM = 384
TO = 256
TT = 768
SG = 8
EC = 1024
PEER_LANES = 256
CONV_T = 640
CONV_RB = 64
VMEM_LIMIT = 56 * 1024 * 1024


def _cparams(sem, flags=None):
    return pltpu.CompilerParams(dimension_semantics=sem, vmem_limit_bytes=VMEM_LIMIT, flags=flags)


def _norm_proj_kernel(x_ref, g_ref, w_ref, o_ref):
    x = x_ref[...]
    ms = jnp.mean(x * x, axis=-1, keepdims=True)
    xn = x * lax.rsqrt(ms + EPS) * g_ref[...]
    o_ref[...] = jnp.dot(xn.astype(BF16), w_ref[...], preferred_element_type=F32)


def _norm_proj(h, g, w_bf, l):
    return pl.pallas_call(
        _norm_proj_kernel,
        out_shape=jax.ShapeDtypeStruct((N_ROWS, PROJ_COLS), F32),
        grid=(N_ROWS // TM,),
        in_specs=[pl.BlockSpec((TM, D_MODEL), lambda i: (i, 0)),
                  pl.BlockSpec((1, D_MODEL), lambda i: (0, 0)),
                  pl.BlockSpec((None, D_MODEL, PROJ_COLS), lambda i: (l, 0, 0))],
        out_specs=pl.BlockSpec((TM, PROJ_COLS), lambda i: (i, 0)),
        compiler_params=_cparams(("parallel",)),
        name="norm_proj",
    )(h, g.reshape(1, D_MODEL), w_bf)


def _conv_kernel(c_ref, w_ref, b_ref, g_ref, beta_ref, o_ref, st_ref, xin_ref, xs_ref):
    t = pl.program_id(1)

    @pl.when(t == 0)
    def _():
        xin_ref[0:32, :] = jnp.zeros((32, CONV_CH), F32)

    @pl.when(t > 0)
    def _():
        xin_ref[0:32, :] = xin_ref[CONV_T:CONV_T + 32, :]

    c = c_ref[...]
    xin_ref[32:32 + CONV_T, :] = c[:, :CONV_CH] * jax.nn.sigmoid(c[:, CONV_CH:])
    n_sh = 32 + CONV_T - 8
    for r in range(1, 8):
        xs_ref[r - 1, 0:n_sh, :] = xin_ref[r:r + n_sh, :]

    w = w_ref[...]
    bias = b_ref[...]
    gam = g_ref[...]
    beta = beta_ref[...]
    for rb in range(CONV_T // CONV_RB):
        r0 = rb * CONV_RB
        acc = jnp.zeros((CONV_RB, CONV_CH), F32)
        for k in range(CONV_WIDTH):
            sh = (2 + k) % 8
            lo = r0 + 2 + k - sh
            win = xin_ref[lo:lo + CONV_RB, :] if sh == 0 else xs_ref[sh - 1, lo:lo + CONV_RB, :]
            acc = acc + win * w[k:k + 1, :]
        y = acc + bias
        mu = jnp.mean(y, axis=-1, keepdims=True)
        d = y - mu
        var = jnp.mean(d * d, axis=-1, keepdims=True)
        yn = d * lax.rsqrt(var + EPS) * gam + beta
        o_ref[r0:r0 + CONV_RB, :] = yn * jax.nn.sigmoid(yn)

    @pl.when(t == pl.num_programs(1) - 1)
    def _():
        lo = 32 + (L_REAL - 32) - (LP - CONV_T)
        st_ref[0] = xin_ref[lo:lo + 32, :]


def _conv_prompt(p, w32, b, g, beta):
    nt = LP // CONV_T
    return pl.pallas_call(
        _conv_kernel,
        out_shape=(jax.ShapeDtypeStruct((NP_ROWS, CONV_CH), F32),
                   jax.ShapeDtypeStruct((BATCH, 32, CONV_CH), F32)),
        grid=(BATCH, nt),
        in_specs=[pl.BlockSpec((CONV_T, 2 * CONV_CH), lambda bi, t: (bi * nt + t, 0)),
                  pl.BlockSpec((32, CONV_CH), lambda bi, t: (0, 0)),
                  pl.BlockSpec((1, CONV_CH), lambda bi, t: (0, 0)),
                  pl.BlockSpec((1, CONV_CH), lambda bi, t: (0, 0)),
                  pl.BlockSpec((1, CONV_CH), lambda bi, t: (0, 0))],
        out_specs=(pl.BlockSpec((CONV_T, CONV_CH), lambda bi, t: (bi * nt + t, 0)),
                   pl.BlockSpec((1, 32, CONV_CH), lambda bi, t: (bi, 0, 0))),
        scratch_shapes=[pltpu.VMEM((32 + CONV_T, CONV_CH), F32),
                        pltpu.VMEM((7, 32 + CONV_T - 8, CONV_CH), F32)],
        compiler_params=_cparams(("arbitrary", "arbitrary")),
        name="conv_prompt",
    )(p, w32, b.reshape(1, -1), g.reshape(1, -1), beta.reshape(1, -1))


def _swap_halves(x, first_half):
    return jnp.where(first_half, pltpu.roll(x, x.shape[1] - 32, axis=1), pltpu.roll(x, 32, axis=1))


def _ret_kernel(q_ref, k_ref, v_ref, rg_ref, cos_ref, sin_ref, dmat_ref, dq_ref, dk_ref, sg_ref,
                gng_ref, gnb_ref, o_ref, st_ref, s_ref):
    j = pl.program_id(1)

    @pl.when(j == 0)
    def _():
        s_ref[...] = jnp.zeros_like(s_ref)

    cos = cos_ref[...]
    sin = sin_ref[...]
    lane = lax.broadcasted_iota(jnp.int32, (BLK, RET_HEADS * RET_DK), 1)
    first_half = (lane % RET_DK) < (RET_DK // 2)
    q = q_ref[...]
    k = k_ref[...]
    q = q * cos + _swap_halves(q, first_half) * sin
    k = (k * cos + _swap_halves(k, first_half) * sin) * (RET_DK ** -0.5)
    v = v_ref[...]
    rg = rg_ref[...]
    hs = range(RET_HEADS)
    qh = [q[:, h * RET_DK:(h + 1) * RET_DK].astype(BF16) for h in hs]
    vh = [v[:, h * RET_DV:(h + 1) * RET_DV].astype(BF16) for h in hs]
    inner, q_state, kdec_t = [], [], []
    for h in hs:
        kh = k[:, h * RET_DK:(h + 1) * RET_DK]
        inner.append(lax.dot_general(qh[h], kh.astype(BF16), (((1,), (1,)), ((), ())),
                                     preferred_element_type=F32) * dmat_ref[h])
        q_state.append(jnp.dot(qh[h], s_ref[h].astype(BF16), preferred_element_type=F32))
        kdec_t.append((kh * dk_ref[0, h]).T.astype(BF16))
    outs = []
    for h in hs:
        outs.append(jnp.dot(inner[h].astype(BF16), vh[h], preferred_element_type=F32) + dq_ref[h] * q_state[h])
        s_ref[h] = sg_ref[0, h, 0:1, :] * s_ref[h] + jnp.dot(kdec_t[h], vh[h], preferred_element_type=F32)
    for h in hs:
        o = outs[h]
        mu = jnp.mean(o, axis=-1, keepdims=True)
        d = o - mu
        var = jnp.mean(d * d, axis=-1, keepdims=True)
        y = d * lax.rsqrt(var + EPS) * gng_ref[:, h * RET_DV:(h + 1) * RET_DV] \
            + gnb_ref[:, h * RET_DV:(h + 1) * RET_DV]
        gate = rg[:, h * RET_DV:(h + 1) * RET_DV]
        o_ref[:, h * RET_DV:(h + 1) * RET_DV] = y * (gate * jax.nn.sigmoid(gate))

    @pl.when(j == pl.num_programs(1) - 1)
    def _():
        st_ref[0] = s_ref[...]


def _ret_tables():
    lg = jnp.log(1.0 - 2.0 ** (-5.0 - jnp.arange(RET_HEADS, dtype=F32)))
    i = jnp.arange(BLK, dtype=F32)
    diff = i[:, None] - i[None, :]
    dmat = jnp.where(diff >= 0, jnp.exp(jnp.maximum(diff, 0.0)[None] * lg[:, None, None]), 0.0)
    dq = jnp.broadcast_to(jnp.exp((i[None] + 1.0) * lg[:, None])[:, :, None], (RET_HEADS, BLK, RET_DV))

    def kdec(c_eff):
        e = jnp.where(i[None] < c_eff, jnp.exp((c_eff - 1.0 - i)[None] * lg[:, None]), 0.0)
        return jnp.broadcast_to(e[:, :, None], (RET_HEADS, BLK, RET_DK))

    def sgam(c_eff):
        return jnp.broadcast_to(jnp.exp(c_eff * lg)[:, None, None], (RET_HEADS, 8, RET_DV))

    dk = jnp.stack([kdec(float(BLK)), kdec(float(LAST_REAL))])
    sg = jnp.stack([sgam(float(BLK)), sgam(float(LAST_REAL))])
    return dmat, dq, dk, sg


def _rotary_tables(pos):
    half = RET_DK // 2
    inv = 1.0 / (10000.0 ** (jnp.arange(half, dtype=F32) / half))
    ang = pos.astype(F32)[:, None] * inv[None]
    cos, sin = jnp.cos(ang), jnp.sin(ang)
    cos_t = jnp.tile(jnp.concatenate([cos, cos], axis=-1), (1, RET_HEADS))
    sin_t = jnp.tile(jnp.concatenate([-sin, sin], axis=-1), (1, RET_HEADS))
    return cos_t, sin_t


def _ret_prompt(p, cos_t, sin_t, tabs, gn_g, gn_b):
    dmat, dq, dk, sg = tabs
    nb = N_BLK
    last = nb - 1
    return pl.pallas_call(
        _ret_kernel,
        out_shape=(jax.ShapeDtypeStruct((NP_ROWS, RET_HEADS * RET_DV), F32),
                   jax.ShapeDtypeStruct((BATCH, RET_HEADS, RET_DK, RET_DV), F32)),
        grid=(BATCH, nb),
        in_specs=[pl.BlockSpec((BLK, 256), lambda b, j: (b * nb + j, C_RQ // 256)),
                  pl.BlockSpec((BLK, 256), lambda b, j: (b * nb + j, C_RK // 256)),
                  pl.BlockSpec((BLK, 512), lambda b, j: (b * nb + j, C_RV // 512)),
                  pl.BlockSpec((BLK, 512), lambda b, j: (b * nb + j, C_RG // 512)),
                  pl.BlockSpec((BLK, 256), lambda b, j: (j, 0)),
                  pl.BlockSpec((BLK, 256), lambda b, j: (j, 0)),
                  pl.BlockSpec((RET_HEADS, BLK, BLK), lambda b, j: (0, 0, 0)),
                  pl.BlockSpec((RET_HEADS, BLK, RET_DV), lambda b, j: (0, 0, 0)),
                  pl.BlockSpec((1, RET_HEADS, BLK, RET_DK), lambda b, j: (j // last, 0, 0, 0)),
                  pl.BlockSpec((1, RET_HEADS, 8, RET_DV), lambda b, j: (j // last, 0, 0, 0)),
                  pl.BlockSpec((1, 512), lambda b, j: (0, 0)),
                  pl.BlockSpec((1, 512), lambda b, j: (0, 0))],
        out_specs=(pl.BlockSpec((BLK, 512), lambda b, j: (b * nb + j, 0)),
                   pl.BlockSpec((1, RET_HEADS, RET_DK, RET_DV), lambda b, j: (b, 0, 0, 0))),
        scratch_shapes=[pltpu.VMEM((RET_HEADS, RET_DK, RET_DV), F32)],
        compiler_params=_cparams(("arbitrary", "arbitrary")),
        name="ret_prompt",
    )(p, p, p, p, cos_t, sin_t, dmat, dq, dk, sg, gn_g.reshape(1, -1), gn_b.reshape(1, -1))


def _group_rms(x, ones_bd, w):
    x2 = x * x
    hi = x2.astype(BF16)
    lo = (x2 - hi.astype(F32)).astype(BF16)
    ss = (jnp.dot(hi, ones_bd, preferred_element_type=F32)
          + jnp.dot(lo, ones_bd, preferred_element_type=F32))
    return x * lax.rsqrt(ss * (1.0 / SWA_HD) + EPS) * w


def _swa_kernel(sinks_ref, q_ref, kc_ref, vc_ref, kp_ref, vp_ref, km_ref, vm_ref, ones_ref,
                qn_ref, kn_ref, bc_ref, bp_ref, bm_ref, o_ref, kout_ref):
    j = pl.program_id(1)
    ones_q = ones_ref[...]
    ones_k = ones_ref[0:128, 0:128]
    qw = qn_ref[...]
    kw = kn_ref[...]
    q = _group_rms(q_ref[...], ones_q, qw) * (SWA_HD ** -0.5)
    kc = _group_rms(kc_ref[...], ones_k, kw)
    kp = _group_rms(kp_ref[...], ones_k, kw)
    pad = jnp.zeros((BLK - N_META, 128), BF16)
    km = jnp.concatenate([_group_rms(km_ref[...], ones_k, kw).astype(BF16), pad], axis=0)
    kout_ref[...] = kc
    vc = vc_ref[...].astype(BF16)
    vp = vp_ref[...].astype(BF16)
    vm = jnp.concatenate([vm_ref[...].astype(BF16), pad], axis=0)

    rows = SWA_GROUP * BLK
    qi = lax.broadcasted_iota(jnp.int32, (rows, BLK), 0) % BLK
    kj = lax.broadcasted_iota(jnp.int32, (rows, BLK), 1)
    valid_c = jnp.where(kj <= qi, j * BLK + kj, -1) >= N_META
    valid_p = jnp.where(kj > qi, (j - 1) * BLK + kj, -1) >= N_META
    valid_m = jnp.where(kj < N_META, kj, LP) <= j * BLK + qi

    dn = (((1,), (1,)), ((), ()))
    kvs = range(SWA_KV_HEADS)
    heads = [range(kv * SWA_GROUP, (kv + 1) * SWA_GROUP) for kv in kvs]
    ksl = [slice(kv * SWA_HD, (kv + 1) * SWA_HD) for kv in kvs]
    s_c, s_p, s_m = [], [], []
    for kv in kvs:
        qs = jnp.concatenate([q[:, h * SWA_HD:(h + 1) * SWA_HD] for h in heads[kv]], axis=0).astype(BF16)
        s_c.append(lax.dot_general(qs, kc[:, ksl[kv]].astype(BF16), dn, preferred_element_type=F32))
        s_p.append(lax.dot_general(qs, kp[:, ksl[kv]].astype(BF16), dn, preferred_element_type=F32))
        s_m.append(lax.dot_general(qs, km[:, ksl[kv]], dn, preferred_element_type=F32))
    p_c, p_p, p_m, den = [], [], [], []
    for kv in kvs:
        sc = jnp.where(valid_c, s_c[kv] + bc_ref[kv], NEG)
        sp = jnp.where(valid_p, s_p[kv] + bp_ref[kv], NEG)
        sm = jnp.where(valid_m, s_m[kv] + bm_ref[0, kv], NEG)
        sink = jnp.concatenate([jnp.full((BLK, 1), sinks_ref[h], F32) for h in heads[kv]], axis=0)
        m = jnp.maximum(jnp.max(jnp.maximum(jnp.maximum(sc, sp), sm), axis=-1, keepdims=True), sink)
        p_c.append(jnp.exp(sc - m))
        p_p.append(jnp.exp(sp - m))
        p_m.append(jnp.exp(sm - m))
        den.append(jnp.sum(p_c[kv] + p_p[kv] + p_m[kv], axis=-1, keepdims=True) + jnp.exp(sink - m))
    for kv in kvs:
        acc = (jnp.dot(p_c[kv].astype(BF16), vc[:, ksl[kv]], preferred_element_type=F32)
               + jnp.dot(p_p[kv].astype(BF16), vp[:, ksl[kv]], preferred_element_type=F32)
               + jnp.dot(p_m[kv].astype(BF16), vm[:, ksl[kv]], preferred_element_type=F32))
        out = acc / den[kv]
        for g, h in enumerate(heads[kv]):
            o_ref[:, h * SWA_HD:(h + 1) * SWA_HD] = out[g * BLK:(g + 1) * BLK, :]


def _one_hot(idx, n):
    return jnp.asarray(np.asarray(idx)[..., None] == np.arange(n), F32)


def _t5_bucket_np(dist):
    max_exact = N_BUCKETS // 2
    df = np.maximum(dist, 1).astype(np.float64)
    large = max_exact + (np.log(df / max_exact) / math.log(REL_MAX_DIST / max_exact)
                         * (N_BUCKETS - max_exact)).astype(np.int64)
    large = np.minimum(large, N_BUCKETS - 1)
    return np.where(dist < max_exact, dist, large).astype(np.int32)


def _bias_tables(rel_bias):
    qi = np.arange(BLK)[:, None]
    kj = np.arange(BLK)[None, :]
    b_cur = _t5_bucket_np(np.maximum(qi - kj, 0))
    b_prev = _t5_bucket_np(np.maximum(qi - kj + BLK, 0))
    m = np.arange(N_META)[None, :]
    b_m0 = _t5_bucket_np(np.maximum(qi - m, 0))
    b_m1 = _t5_bucket_np(np.maximum(qi + BLK - m, 0))
    assert (b_m1 == N_BUCKETS - 1).all()
    rb = rel_bias.astype(F32)

    def look(bk):
        return jnp.einsum("...b,bh->h...", _one_hot(bk, N_BUCKETS), rb, precision=lax.Precision.HIGHEST)

    return look(b_cur), look(b_prev), jnp.stack([look(b_m0), look(b_m1)])


def _swa_prompt(p, sinks, q_norm, k_norm, bias_tabs):
    srows = SWA_GROUP * BLK
    b_cur = bias_tabs[0].reshape(SWA_KV_HEADS, srows, BLK)
    b_prev = bias_tabs[1].reshape(SWA_KV_HEADS, srows, BLK)
    b_meta = jnp.pad(bias_tabs[2].reshape(2, SWA_KV_HEADS, srows, N_META),
                     ((0, 0), (0, 0), (0, 0), (0, BLK - N_META)))
    nb = N_BLK
    ones_bd = jnp.asarray(np.kron(np.eye(SWA_HEADS), np.ones((SWA_HD, SWA_HD))), BF16)
    qn = jnp.tile(q_norm.astype(F32), SWA_HEADS).reshape(1, -1)
    kn = jnp.tile(k_norm.astype(F32), SWA_KV_HEADS).reshape(1, -1)
    ck, cv = C_SK // 128, C_SV // 128
    return pl.pallas_call(
        _swa_kernel,
        out_shape=(jax.ShapeDtypeStruct((NP_ROWS, SWA_HEADS * SWA_HD), F32),
                   jax.ShapeDtypeStruct((NP_ROWS, SWA_KV_HEADS * SWA_HD), F32)),
        grid=(BATCH, nb),
        in_specs=[pl.BlockSpec(memory_space=pltpu.SMEM),
                  pl.BlockSpec((BLK, 512), lambda b, j: (b * nb + j, C_SQ // 512)),
                  pl.BlockSpec((BLK, 128), lambda b, j: (b * nb + j, ck)),
                  pl.BlockSpec((BLK, 128), lambda b, j: (b * nb + j, cv)),
                  pl.BlockSpec((BLK, 128), lambda b, j: (b * nb + jnp.maximum(j - 1, 0), ck)),
                  pl.BlockSpec((BLK, 128), lambda b, j: (b * nb + jnp.maximum(j - 1, 0), cv)),
                  pl.BlockSpec((N_META, 128), lambda b, j: (b * (LP // N_META), ck)),
                  pl.BlockSpec((N_META, 128), lambda b, j: (b * (LP // N_META), cv)),
                  pl.BlockSpec((512, 512), lambda b, j: (0, 0)),
                  pl.BlockSpec((1, 512), lambda b, j: (0, 0)),
                  pl.BlockSpec((1, 128), lambda b, j: (0, 0)),
                  pl.BlockSpec((SWA_KV_HEADS, srows, BLK), lambda b, j: (0, 0, 0)),
                  pl.BlockSpec((SWA_KV_HEADS, srows, BLK), lambda b, j: (0, 0, 0)),
                  pl.BlockSpec((1, SWA_KV_HEADS, srows, BLK), lambda b, j: (jnp.minimum(j, 1), 0, 0, 0))],
        out_specs=(pl.BlockSpec((BLK, 512), lambda b, j: (b * nb + j, 0)),
                   pl.BlockSpec((BLK, 128), lambda b, j: (b * nb + j, 0))),
        compiler_params=_cparams(("parallel", "parallel")),
        name="swa_prompt",
    )(sinks.astype(F32), p, p, p, p, p, p, p, ones_bd, qn, kn, b_cur, b_prev, b_meta)


def _out_ffn_kernel(h_ref, a_ref, b_ref, c_ref, as_ref, bs_ref, cs_ref, wo_ref, g_ref, wq_ref, kbd_ref,
                    h1_ref, xn_ref, st_ref):
    is_prompt = pl.program_id(0) < NP_ROWS // TO
    a = jnp.where(is_prompt, a_ref[...], as_ref[...])
    b = jnp.where(is_prompt, b_ref[...], bs_ref[...])
    c = jnp.where(is_prompt, c_ref[...], cs_ref[...])
    h1 = (h_ref[...]
          + jnp.dot(a.astype(BF16), wo_ref[0:512, :], preferred_element_type=F32)
          + jnp.dot(b.astype(BF16), wo_ref[512:1024, :], preferred_element_type=F32)
          + jnp.dot(c.astype(BF16), wo_ref[1024:1536, :], preferred_element_type=F32))
    h1_ref[...] = h1
    ms = jnp.mean(h1 * h1, axis=-1, keepdims=True)
    xn_f = h1 * lax.rsqrt(ms + EPS) * g_ref[...]
    xn_ref[...] = xn_f.T.astype(BF16)
    q = jnp.dot(xn_f.astype(BF16), wq_ref[...], preferred_element_type=F32).astype(BF16)
    kbd = kbd_ref[...]
    for hh in range(PEER_HEADS):
        st_ref[hh] = lax.dot_general(kbd, q[:, hh * 128:(hh + 1) * 128], (((1,), (1,)), ((), ())),
                                     preferred_element_type=F32)


def _out_ffn(h, mix_p, mix_s, wo_bf, g, wq_bf, kbd_bf, l):
    npb = NP_ROWS // TO
    pspec = pl.BlockSpec((TO, 512), lambda i: (jnp.minimum(i, npb - 1), 0))
    sspec = pl.BlockSpec((TO, 512), lambda i: (jnp.maximum(i - npb, 0), 0))
    return pl.pallas_call(
        _out_ffn_kernel,
        out_shape=(jax.ShapeDtypeStruct((N_ROWS, D_MODEL), F32),
                   jax.ShapeDtypeStruct((D_MODEL, N_ROWS), BF16),
                   jax.ShapeDtypeStruct((PEER_HEADS, 2 * N_KEYS, N_ROWS), F32)),
        grid=(N_ROWS // TO,),
        in_specs=[pl.BlockSpec((TO, D_MODEL), lambda i: (i, 0)),
                  pspec, pspec, pspec, sspec, sspec, sspec,
                  pl.BlockSpec((None, 1536, D_MODEL), lambda i: (l, 0, 0)),
                  pl.BlockSpec((1, D_MODEL), lambda i: (0, 0)),
                  pl.BlockSpec((None, D_MODEL, D_MODEL), lambda i: (l, 0, 0)),
                  pl.BlockSpec((2 * N_KEYS, 128), lambda i: (0, 0))],
        out_specs=(pl.BlockSpec((TO, D_MODEL), lambda i: (i, 0)),
                   pl.BlockSpec((D_MODEL, TO), lambda i: (0, i)),
                   pl.BlockSpec((PEER_HEADS, 2 * N_KEYS, TO), lambda i: (0, 0, i))),
        compiler_params=_cparams(("parallel",)),
        name="out_ffn",
    )(h, *mix_p, *mix_s, wo_bf, g.reshape(1, -1), wq_bf, kbd_bf)


def _top_vals(x, n, with_rank=False):
    vals = []
    rank = jnp.full(x.shape, float(n), F32)
    for r in range(n):
        mx = jnp.max(x, axis=0, keepdims=True)
        vals.append(mx)
        hit = x == mx
        if with_rank:
            rank = jnp.where(hit, float(r), rank)
        x = jnp.where(hit, NEG, x)
    return (vals, rank) if with_rank else vals


def _top_vals_paired(x, n):
    half = x.shape[0] // 2
    hi = jnp.maximum(x[:half], x[half:])
    lo = jnp.minimum(x[:half], x[half:])
    vals = []
    for _ in range(n):
        mx = jnp.max(hi, axis=0, keepdims=True)
        vals.append(mx)
        hit = hi == mx
        hi = jnp.where(hit, lo, hi)
        lo = jnp.where(hit, NEG, lo)
    return vals


def _topk_kernel(st_ref, rho_ref, c1_ref, r2_ref, e2_ref):
    s1 = st_ref[0, 0:N_KEYS, :]
    s2 = st_ref[0, N_KEYS:2 * N_KEYS, :]
    v1 = _top_vals_paired(s1, PEER_TOPK)
    v2, rank2 = _top_vals(s2, PEER_TOPK, with_rank=True)
    sv1 = jnp.concatenate(v1, axis=0)
    sv2 = jnp.concatenate(v2, axis=0)
    cand = jnp.concatenate([v1[0] + sv2, v1[1] + sv2]
                           + [v1[a] + sv2[0:8] for a in range(2, 8)]
                           + [sv1[8:16] + v2[0], jnp.full((8, s1.shape[1]), NEG, F32)], axis=0)
    top = _top_vals_paired(cand, PEER_TOPK)
    tau = top[PEER_TOPK - 1]
    z = jnp.ones_like(tau)
    for r in range(1, PEER_TOPK):
        z = z + jnp.exp(top[r] - top[0])
    rho = jnp.zeros(s1.shape, F32)
    for a in range(PEER_TOPK):
        n_a = jnp.sum(jnp.where(v1[a] + sv2 >= tau, 1.0, 0.0), axis=0, keepdims=True)
        rho = jnp.where(s1 == v1[a], n_a, rho)
    rho_ref[0] = rho
    c1_ref[0] = jnp.exp(s1 - v1[0]) * (0.5 / z)
    r2_ref[0] = rank2.astype(BF16)
    e2_ref[0] = jnp.exp(s2 - v2[0]).astype(BF16)


def _topk(st):
    shp = jax.ShapeDtypeStruct((PEER_HEADS, N_KEYS, N_ROWS), F32)
    shp_bf = jax.ShapeDtypeStruct((PEER_HEADS, N_KEYS, N_ROWS), BF16)
    spec = pl.BlockSpec((1, N_KEYS, TM), lambda i, h: (h, 0, i))
    return pl.pallas_call(
        _topk_kernel,
        out_shape=(shp, shp, shp_bf, shp_bf),
        grid=(N_ROWS // TM, PEER_HEADS),
        in_specs=[pl.BlockSpec((1, 2 * N_KEYS, TM), lambda i, h: (h, 0, i))],
        out_specs=(spec, spec, spec, spec),
        compiler_params=_cparams(("parallel", "parallel")),
        name="peer_topk",
    )(st)


N_CHUNK = N_EXPERTS // EC
N_WORK = (N_ROWS // TT) * N_CHUNK


def _peer_stages(xn_ref, u_ref, vt_ref, rho_ref, c1_ref, r2_ref, e2_ref, acc_ref,
                 at_w, at_r, gt_w, gt_r):
    sub = 16
    n_tb = TT // PEER_LANES
    n_ii = EC // N_KEYS

    def gate_build(ii, tb, jbs):
        cols = slice(tb * PEER_LANES, (tb + 1) * PEER_LANES)
        w = {jb: jnp.zeros((sub, PEER_LANES), BF16) for jb in jbs}
        for hh in range(PEER_HEADS):
            rho = jnp.broadcast_to(rho_ref[hh, ii:ii + 1, cols], (sub, PEER_LANES)).astype(BF16)
            cc = jnp.broadcast_to(c1_ref[hh, ii:ii + 1, cols], (sub, PEER_LANES)).astype(BF16)
            for jb in jbs:
                jr = slice(jb * sub, (jb + 1) * sub)
                w[jb] = w[jb] + jnp.where(r2_ref[hh, jr, cols] < rho, e2_ref[hh, jr, cols] * cc, 0.0)
        for jb in jbs:
            rows = slice(ii * N_KEYS + jb * sub, ii * N_KEYS + (jb + 1) * sub)
            a = at_r[rows, cols]
            gt_w[rows, cols] = w[jb] * (a * (1.0 + lax.erf(a * (2.0 ** -0.5))))

    at_w[...] = jnp.dot(u_ref[...], xn_ref[...], preferred_element_type=F32).astype(BF16)
    acc_ref[...] += jnp.dot(vt_ref[...], gt_r[...], preferred_element_type=F32)
    all_jb = range(N_KEYS // sub)
    for ii in range(n_ii):
        for tb in range(n_tb):
            gate_build(ii, tb, all_jb)


def _peer_kernel(xn_ref, h1_ref, u_ref, vt_ref, rho_ref, c1_ref, r2_ref, e2_ref, o_ref,
                 acc_ref, at, gt):
    g = pl.program_id(0)
    cv = (g - 1) % N_CHUNK
    slot = g % 2

    @pl.when(g == 0)
    def _():
        gt[1] = jnp.zeros((EC, TT), BF16)

    @pl.when(jnp.logical_or(g < 1, cv == 0))
    def _():
        acc_ref[...] = jnp.zeros_like(acc_ref)

    _peer_stages(xn_ref, u_ref, vt_ref, rho_ref, c1_ref, r2_ref, e2_ref, acc_ref,
                 at, at, gt.at[slot], gt.at[1 - slot])

    @pl.when(jnp.logical_and(g >= 1, cv == N_CHUNK - 1))
    def _():
        o_ref[...] = h1_ref[...] + acc_ref[...].T


def _peer(xn, h1, u_bf, vt_bf, rho, c1, r2, e2, l):
    nk = EC // N_KEYS

    def item(g, lag):
        w = jnp.clip(g - lag, 0, N_WORK - 1)
        return w // N_CHUNK, w % N_CHUNK

    return pl.pallas_call(
        _peer_kernel,
        out_shape=jax.ShapeDtypeStruct((N_ROWS, D_MODEL), F32),
        grid=(N_WORK + 1,),
        in_specs=[pl.BlockSpec((D_MODEL, TT), lambda g: (0, item(g, 0)[0])),
                  pl.BlockSpec((TT, D_MODEL), lambda g: (item(g, 1)[0], 0)),
                  pl.BlockSpec((None, EC, D_MODEL), lambda g: (l, item(g, 0)[1], 0)),
                  pl.BlockSpec((None, D_MODEL, EC), lambda g: (l, 0, item(g, 1)[1])),
                  pl.BlockSpec((PEER_HEADS, nk, TT), lambda g: (0, item(g, 0)[1], item(g, 0)[0])),
                  pl.BlockSpec((PEER_HEADS, nk, TT), lambda g: (0, item(g, 0)[1], item(g, 0)[0])),
                  pl.BlockSpec((PEER_HEADS, N_KEYS, TT), lambda g: (0, 0, item(g, 0)[0])),
                  pl.BlockSpec((PEER_HEADS, N_KEYS, TT), lambda g: (0, 0, item(g, 0)[0]))],
        out_specs=pl.BlockSpec((TT, D_MODEL), lambda g: (item(g, 1)[0], 0)),
        scratch_shapes=[pltpu.VMEM((D_MODEL, TT), F32), pltpu.VMEM((EC, TT), BF16),
                        pltpu.VMEM((2, EC, TT), BF16)],
        compiler_params=_cparams(("arbitrary",)),
        name="peer_dense",
    )(xn, h1, u_bf, vt_bf, rho, c1, r2, e2)


S_STACK = SWA_GROUP * DEC_SEQ
S_SMALL = N_META + DEC_SEQ
NT_DIMS = (((1,), (1,)), ((), ()))


def _sample_kernel(p_ref, cs_ref, rs_ref, meta_ref, win_ref,
                   cw_ref, cb_ref, cg_ref, cbeta_ref,
                   cos_ref, sin_ref, dmat_ref, dq_ref, dk_ref, sgam_ref, gng_ref, gnb_ref, eye_ref,
                   ones_ref, qn_ref, kn_ref, bwin_ref, bsm_ref, sink_ref,
                   conv_o_ref, ret_o_ref, swa_o_ref, ncs_ref, nrs_ref, nwin_ref,
                   xin_ref, kpad_ref, vpad_ref, ksm_ref, vsm_ref):
    kpad_ref[...] = jnp.zeros_like(kpad_ref)
    vpad_ref[...] = jnp.zeros_like(vpad_ref)
    ksm_ref[...] = jnp.zeros_like(ksm_ref)
    vsm_ref[...] = jnp.zeros_like(vsm_ref)
    lane = lax.broadcasted_iota(jnp.int32, (DEC_SEQ, RET_HEADS * RET_DK), 1)
    first_half = (lane % RET_DK) < (RET_DK // 2)
    cos = cos_ref[...]
    sin = sin_ref[...]
    ones_q = ones_ref[...]
    ones_k = ones_ref[0:128, 0:128]

    seqs = range(SG)
    rows = [slice(s * DEC_SEQ, (s + 1) * DEC_SEQ) for s in seqs]
    xin = [xin_ref.at[s] for s in seqs]
    kpad = [kpad_ref.at[s] for s in seqs]
    vpad = [vpad_ref.at[s] for s in seqs]
    ksm = [ksm_ref.at[s] for s in seqs]
    vsm = [vsm_ref.at[s] for s in seqs]

    for s in seqs:
        glu = (p_ref[rows[s], C_CONV:C_CONV + CONV_CH]
               * jax.nn.sigmoid(p_ref[rows[s], C_CONV + CONV_CH:C_RQ]))
        xin[s][0:CONV_WIDTH - 1, :] = cs_ref[0, s]
        xin[s][CONV_WIDTH - 1:CONV_WIDTH - 1 + DEC_SEQ, :] = glu
    acc = [jnp.zeros((DEC_SEQ, CONV_CH), F32) for _ in seqs]
    for k in range(CONV_WIDTH):
        wk = cw_ref[k:k + 1, :]
        for s in seqs:
            acc[s] = acc[s] + xin[s][k:k + DEC_SEQ, :] * wk
    for s in seqs:
        y = acc[s] + cb_ref[...]
        mu = jnp.mean(y, axis=-1, keepdims=True)
        d = y - mu
        var = jnp.mean(d * d, axis=-1, keepdims=True)
        yn = d * lax.rsqrt(var + EPS) * cg_ref[...] + cbeta_ref[...]
        conv_o_ref[rows[s], :] = yn * jax.nn.sigmoid(yn)
        ncs_ref[s] = xin[s][DEC_SEQ:DEC_SEQ + CONV_WIDTH - 1, :]

    qr = []
    for s in seqs:
        q = p_ref[rows[s], C_RQ:C_RK]
        k = p_ref[rows[s], C_RK:C_RV]
        qr.append(q * cos + _swap_halves(q, first_half) * sin)
        kpad[s][0:DEC_SEQ, :] = (k * cos + _swap_halves(k, first_half) * sin) * (RET_DK ** -0.5)
        vpad[s][0:DEC_SEQ, :] = p_ref[rows[s], C_RV:C_RG]
    pairs = [(s, h) for h in range(RET_HEADS) for s in seqs]
    inner, q_state, kdec_t = {}, {}, {}
    for s, h in pairs:
        qh = qr[s][:, h * RET_DK:(h + 1) * RET_DK]
        kp = kpad[s][:, h * RET_DK:(h + 1) * RET_DK]
        inner[s, h] = lax.dot_general(qh, kp, NT_DIMS, preferred_element_type=F32) * dmat_ref[h]
        q_state[s, h] = jnp.dot(qh, rs_ref[0, s, h], preferred_element_type=F32)
        kdec_t[s, h] = lax.dot_general(eye_ref[...], kp * dk_ref[h], NT_DIMS,
                                       preferred_element_type=F32)
    ret_raw = {}
    for s, h in pairs:
        vp = vpad[s][:, h * RET_DV:(h + 1) * RET_DV]
        ret_raw[s, h] = jnp.dot(inner[s, h], vp, preferred_element_type=F32) + dq_ref[h] * q_state[s, h]
        nrs_ref[s, h] = sgam_ref[h] * rs_ref[0, s, h] + jnp.dot(kdec_t[s, h], vp, preferred_element_type=F32)
    for s, h in pairs:
        o = ret_raw[s, h]
        mu = jnp.mean(o, axis=-1, keepdims=True)
        d = o - mu
        var = jnp.mean(d * d, axis=-1, keepdims=True)
        hs = slice(h * RET_DV, (h + 1) * RET_DV)
        yr = d * lax.rsqrt(var + EPS) * gng_ref[:, hs] + gnb_ref[:, hs]
        gate = p_ref[rows[s], C_RG + h * RET_DV:C_RG + (h + 1) * RET_DV]
        ret_o_ref[rows[s], hs] = yr * (gate * jax.nn.sigmoid(gate))

    qss = [jnp.dot(jnp.square(p_ref[rows[s], C_SQ:C_SK]), ones_q, preferred_element_type=F32) for s in seqs]
    kss = [jnp.dot(jnp.square(p_ref[rows[s], C_SK:C_SV]), ones_k, preferred_element_type=F32) for s in seqs]
    qa = []
    for s in seqs:
        qa.append(p_ref[rows[s], C_SQ:C_SK] * lax.rsqrt(qss[s] * (1.0 / SWA_HD) + EPS)
                  * qn_ref[...] * (SWA_HD ** -0.5))
        kn = p_ref[rows[s], C_SK:C_SV] * lax.rsqrt(kss[s] * (1.0 / SWA_HD) + EPS) * kn_ref[...]
        vn = p_ref[rows[s], C_SV:PROJ_COLS]
        ksm[s][0:N_META, :] = meta_ref[0, s, :, 0:128]
        vsm[s][0:N_META, :] = meta_ref[0, s, :, 128:256]
        ksm[s][N_META:S_SMALL, :] = kn
        vsm[s][N_META:S_SMALL, :] = vn
        nwin_ref[s, 0:WINDOW - DEC_SEQ, :] = win_ref[0, s, DEC_SEQ:WINDOW, :]
        nwin_ref[s, WINDOW - DEC_SEQ:WINDOW, 0:128] = kn
        nwin_ref[s, WINDOW - DEC_SEQ:WINDOW, 128:256] = vn
    spairs = [(s, kv) for kv in range(SWA_KV_HEADS) for s in seqs]
    s_w, s_s = {}, {}
    for s, kv in spairs:
        ksl = slice(kv * SWA_HD, (kv + 1) * SWA_HD)
        qs = jnp.concatenate([qa[s][:, (kv * SWA_GROUP + g) * SWA_HD:(kv * SWA_GROUP + g + 1) * SWA_HD]
                              for g in range(SWA_GROUP)], axis=0)
        s_w[s, kv] = lax.dot_general(qs, win_ref[0, s, :, ksl], NT_DIMS, preferred_element_type=F32) + bwin_ref[kv]
        s_s[s, kv] = lax.dot_general(qs, ksm[s][:, ksl], NT_DIMS, preferred_element_type=F32) + bsm_ref[kv]
    p_w, p_s, den = {}, {}, {}
    for s, kv in spairs:
        sink = sink_ref[kv, :, 0:1]
        m = jnp.maximum(jnp.maximum(jnp.max(s_w[s, kv], axis=-1, keepdims=True),
                                    jnp.max(s_s[s, kv], axis=-1, keepdims=True)), sink)
        p_w[s, kv] = jnp.exp(s_w[s, kv] - m)
        p_s[s, kv] = jnp.exp(s_s[s, kv] - m)
        den[s, kv] = (jnp.sum(p_w[s, kv], axis=-1, keepdims=True) + jnp.sum(p_s[s, kv], axis=-1, keepdims=True)
                      + jnp.exp(sink - m))
    for s, kv in spairs:
        ksl = slice(kv * SWA_HD, (kv + 1) * SWA_HD)
        vsl = slice(128 + kv * SWA_HD, 128 + (kv + 1) * SWA_HD)
        o = (jnp.dot(p_w[s, kv], win_ref[0, s, :, vsl], preferred_element_type=F32)
             + jnp.dot(p_s[s, kv], vsm[s][:, ksl], preferred_element_type=F32)) / den[s, kv]
        for g in range(SWA_GROUP):
            hh = kv * SWA_GROUP + g
            swa_o_ref[rows[s], hh * SWA_HD:(hh + 1) * SWA_HD] = o[g * DEC_SEQ:(g + 1) * DEC_SEQ, :]


def _sample_tables(rel_bias):
    lg = jnp.log(1.0 - 2.0 ** (-5.0 - jnp.arange(RET_HEADS, dtype=F32)))
    i = jnp.arange(DEC_SEQ, dtype=F32)
    diff = i[:, None] - i[None, :]
    dm = jnp.where(diff >= 0, jnp.exp(jnp.maximum(diff, 0.0)[None] * lg[:, None, None]), 0.0)
    dmat = jnp.zeros((RET_HEADS, DEC_SEQ, 128), F32).at[:, :, :DEC_SEQ].set(dm)
    dq = jnp.broadcast_to(jnp.exp((i[None] + 1.0) * lg[:, None])[:, :, None], (RET_HEADS, DEC_SEQ, RET_DV))
    kd = jnp.exp((DEC_SEQ - 1.0 - i)[None] * lg[:, None])
    dk = jnp.zeros((RET_HEADS, 128, RET_DK), F32).at[:, :DEC_SEQ, :].set(
        jnp.broadcast_to(kd[:, :, None], (RET_HEADS, DEC_SEQ, RET_DK)))
    sg = jnp.broadcast_to(jnp.exp(DEC_SEQ * lg)[:, None, None], (RET_HEADS, 1, RET_DV))
    cos_t, sin_t = _rotary_tables(PAST_LEN + jnp.arange(DEC_SEQ))

    ti = np.arange(S_STACK)[:, None] % DEC_SEQ
    j = np.arange(WINDOW)[None, :]
    bk_win = _t5_bucket_np(np.maximum(ti + WINDOW - j, 0))
    ok_win = j > ti
    c = np.arange(128)[None, :]
    jn = c - N_META
    bk_new = _t5_bucket_np(np.clip(ti - jn, 0, None))
    ok_sm = (c < N_META) | ((c < S_SMALL) & (jn <= ti))
    bk_sm = np.where(c < N_META, N_BUCKETS - 1, bk_new)
    rb = rel_bias.astype(F32)
    head = np.arange(SWA_KV_HEADS)[:, None] * SWA_GROUP + (np.arange(S_STACK) // DEC_SEQ)[None, :]
    head_oh = _one_hot(head, SWA_HEADS)

    def look(bk, ok):
        b = jnp.einsum("rcb,bh,krh->krc", _one_hot(bk, N_BUCKETS), rb, head_oh,
                       precision=lax.Precision.HIGHEST)
        return jnp.where(jnp.asarray(np.broadcast_to(ok[None], b.shape)), b, NEG)

    return dict(dmat=dmat, dq=dq, dk=dk, sg=sg, cos=cos_t, sin=sin_t,
                bwin=look(bk_win, ok_win), bsm=look(bk_sm, ok_sm), head_oh=head_oh)


def _sample_mixers(p, l, cache_meta_kv, cache_swa_kv, state_ret, state_conv, tabs, conv_w32, conv_b,
                   conv_g, conv_beta, gn_g, gn_b, q_norm, k_norm, sinks):
    row0 = NP_ROWS // (SG * DEC_SEQ)
    nrow = SG * DEC_SEQ
    ones_bd = jnp.asarray(np.kron(np.eye(SWA_HEADS), np.ones((SWA_HD, SWA_HD))), F32)
    qn = jnp.tile(q_norm.astype(F32), SWA_HEADS).reshape(1, -1)
    kn = jnp.tile(k_norm.astype(F32), SWA_KV_HEADS).reshape(1, -1)
    sink_rows = jnp.einsum("h,krh->kr", sinks.astype(F32), tabs["head_oh"], precision=lax.Precision.HIGHEST)
    sink_t = jnp.broadcast_to(sink_rows[:, :, None], (SWA_KV_HEADS, S_STACK, 128))
    meta = cache_meta_kv.reshape(DEPTH, DEC_BATCH, N_META, 256)
    win = cache_swa_kv.reshape(DEPTH, DEC_BATCH, WINDOW, 256)

    def const(shape):
        return pl.BlockSpec(shape, lambda i: (0,) * len(shape))

    return pl.pallas_call(
        _sample_kernel,
        out_shape=(jax.ShapeDtypeStruct((NS_ROWS, CONV_CH), F32),
                   jax.ShapeDtypeStruct((NS_ROWS, RET_HEADS * RET_DV), F32),
                   jax.ShapeDtypeStruct((NS_ROWS, SWA_HEADS * SWA_HD), F32),
                   jax.ShapeDtypeStruct((DEC_BATCH, CONV_WIDTH - 1, CONV_CH), F32),
                   jax.ShapeDtypeStruct((DEC_BATCH, RET_HEADS, RET_DK, RET_DV), F32),
                   jax.ShapeDtypeStruct((DEC_BATCH, WINDOW, 256), F32)),
        grid=(DEC_BATCH // SG,),
        in_specs=[pl.BlockSpec((nrow, PROJ_COLS), lambda i: (row0 + i, 0)),
                  pl.BlockSpec((1, SG, CONV_WIDTH - 1, CONV_CH), lambda i: (l, i, 0, 0)),
                  pl.BlockSpec((1, SG, RET_HEADS, RET_DK, RET_DV), lambda i: (l, i, 0, 0, 0)),
                  pl.BlockSpec((1, SG, N_META, 256), lambda i: (l, i, 0, 0)),
                  pl.BlockSpec((1, SG, WINDOW, 256), lambda i: (l, i, 0, 0)),
                  const((32, CONV_CH)), const((1, CONV_CH)), const((1, CONV_CH)), const((1, CONV_CH)),
                  const((DEC_SEQ, 256)), const((DEC_SEQ, 256)),
                  const((RET_HEADS, DEC_SEQ, 128)), const((RET_HEADS, DEC_SEQ, RET_DV)),
                  const((RET_HEADS, 128, RET_DK)), const((RET_HEADS, 1, RET_DV)),
                  const((1, 512)), const((1, 512)), const((RET_DK, RET_DK)),
                  const((512, 512)), const((1, 512)), const((1, 128)),
                  const((SWA_KV_HEADS, S_STACK, 128)), const((SWA_KV_HEADS, S_STACK, 128)),
                  const((SWA_KV_HEADS, S_STACK, 128))],
        out_specs=(pl.BlockSpec((nrow, CONV_CH), lambda i: (i, 0)),
                   pl.BlockSpec((nrow, 512), lambda i: (i, 0)),
                   pl.BlockSpec((nrow, 512), lambda i: (i, 0)),
                   pl.BlockSpec((SG, CONV_WIDTH - 1, CONV_CH), lambda i: (i, 0, 0)),
                   pl.BlockSpec((SG, RET_HEADS, RET_DK, RET_DV), lambda i: (i, 0, 0, 0)),
                   pl.BlockSpec((SG, WINDOW, 256), lambda i: (i, 0, 0))),
        scratch_shapes=[pltpu.VMEM((SG, 40, CONV_CH), F32), pltpu.VMEM((SG, 128, RET_HEADS * RET_DK), F32),
                        pltpu.VMEM((SG, 128, RET_HEADS * RET_DV), F32),
                        pltpu.VMEM((SG, 128, 128), F32), pltpu.VMEM((SG, 128, 128), F32)],
        compiler_params=_cparams(("parallel",)),
        name="sample_mixers",
    )(p, state_conv, state_ret, meta, win,
      conv_w32, conv_b.reshape(1, -1), conv_g.reshape(1, -1), conv_beta.reshape(1, -1),
      tabs["cos"], tabs["sin"], tabs["dmat"], tabs["dq"], tabs["dk"], tabs["sg"],
      gn_g.reshape(1, -1), gn_b.reshape(1, -1), jnp.eye(RET_DK, dtype=F32),
      ones_bd, qn, kn, tabs["bwin"], tabs["bsm"], sink_t)


def kernel(x_prompt, x_sample, cache_meta_kv, cache_swa_kv, state_ret, state_conv, meta_tokens, rel_bias,
           norm_mix, w_in, conv_w, conv_b, conv_ln_g, conv_ln_b, ret_gn_g, ret_gn_b, swa_q_norm,
           swa_k_norm, swa_sinks, w_out, norm_ffn, peer_wq, peer_keys, peer_u, peer_v):
    meta = jnp.broadcast_to(meta_tokens.astype(F32)[None], (BATCH, N_META, D_MODEL))
    pad = jnp.zeros((BATCH, LP - L_REAL, D_MODEL), F32)
    hp = jnp.concatenate([meta, x_prompt, pad], axis=1).reshape(NP_ROWS, D_MODEL)
    h = jnp.concatenate([hp, x_sample.reshape(NS_ROWS, D_MODEL)], axis=0)

    cos_t, sin_t = _rotary_tables(jnp.arange(LP))
    ret_tabs = _ret_tables()
    bias_tabs = _bias_tables(rel_bias)
    sample_tabs = _sample_tables(rel_bias)

    w_in_bf, w_out_bf, wq_bf = w_in.astype(BF16), w_out.astype(BF16), peer_wq.astype(BF16)
    u_bf = peer_u.astype(BF16)
    vt_bf = jnp.swapaxes(peer_v, 1, 2).astype(BF16)

    meta_p, win_p, ret_p, conv_p, win_s, ret_s, conv_s = [], [], [], [], [], [], []
    for l in range(DEPTH):
        p = _norm_proj(h, norm_mix[l], w_in_bf, l)

        w32 = jnp.concatenate([conv_w[l], jnp.zeros((1, CONV_CH), F32)], axis=0)
        conv_o, conv_tail = _conv_prompt(p, w32, conv_b[l], conv_ln_g[l], conv_ln_b[l])
        ret_o, ret_state = _ret_prompt(p, cos_t, sin_t, ret_tabs, ret_gn_g[l], ret_gn_b[l])
        swa_o, k_normed = _swa_prompt(p, swa_sinks[l], swa_q_norm[l], swa_k_norm[l], bias_tabs)
        s_conv, s_ret, s_swa, new_conv, new_ret, new_win = _sample_mixers(
            p, l, cache_meta_kv, cache_swa_kv, state_ret, state_conv, sample_tabs, w32, conv_b[l],
            conv_ln_g[l], conv_ln_b[l], ret_gn_g[l], ret_gn_b[l], swa_q_norm[l], swa_k_norm[l], swa_sinks[l])
        new_win = new_win.reshape(DEC_BATCH, WINDOW, 2, SWA_KV_HEADS, SWA_HD)

        kbd = jnp.zeros((2 * N_KEYS, 2 * PEER_HALF), F32)
        kbd = kbd.at[:N_KEYS, :PEER_HALF].set(peer_keys[l, 0]).at[N_KEYS:, PEER_HALF:].set(peer_keys[l, 1])
        h1, xn, st = _out_ffn(h, (conv_o, ret_o, swa_o), (s_conv, s_ret, s_swa), w_out_bf,
                              norm_ffn[l], wq_bf, kbd.astype(BF16), l)
        rho, c1, r2, e2 = _topk(st)
        h = _peer(xn, h1, u_bf, vt_bf, rho, c1, r2, e2, l)

        kp = k_normed.reshape(BATCH, LP, SWA_KV_HEADS, SWA_HD)

        def kv_rows(lo, hi, p=p, kp=kp):
            vrows = jnp.stack([p[b * LP + lo:b * LP + hi, C_SV:] for b in range(BATCH)])
            return jnp.stack([kp[:, lo:hi], vrows.reshape(BATCH, hi - lo, SWA_KV_HEADS, SWA_HD)], axis=2)

        meta_p.append(kv_rows(0, N_META))
        win_p.append(kv_rows(L_REAL - WINDOW, L_REAL))
        ret_p.append(ret_state)
        conv_p.append(conv_tail[:, 32 - (CONV_WIDTH - 1):])
        win_s.append(new_win)
        ret_s.append(new_ret)
        conv_s.append(new_conv)

    y_prompt = h[:NP_ROWS].reshape(BATCH, LP, D_MODEL)[:, N_META:L_REAL]
    y_sample = h[NP_ROWS:].reshape(DEC_BATCH, DEC_SEQ, D_MODEL)
    return (y_prompt, y_sample, jnp.stack(meta_p), jnp.stack(win_p), jnp.stack(ret_p), jnp.stack(conv_p),
            jnp.stack(win_s), jnp.stack(ret_s), jnp.stack(conv_s))
```

```python
import functools
import math

import numpy as np
import jax
import jax.numpy as jnp
from jax import lax
from jax.experimental import pallas as pl
from jax.experimental.pallas import tpu as pltpu

F32 = jnp.float32
BF16 = jnp.bfloat16

D_MODEL = 1024
BATCH = 2
SEQ = 8192
DEPTH = 2
DEC_BATCH = 128
DEC_SEQ = 8
PAST_LEN = 8192
N_META = 16
CONV_CH = 512
CONV_WIDTH = 31
RET_HEADS = 4
RET_DK = 64
RET_DV = 128
SWA_HEADS = 8
SWA_KV_HEADS = 2
SWA_GROUP = SWA_HEADS // SWA_KV_HEADS
SWA_HD = 64
WINDOW = 128
N_BUCKETS = 32
REL_MAX_DIST = 128
PEER_HEADS = 8
N_KEYS = 128
N_EXPERTS = N_KEYS * N_KEYS
PEER_TOPK = 16
PEER_HALF = 64
EPS = 1e-6
NEG = -1e30

PROJ_COLS = 3328
C_CONV, C_RQ, C_RK, C_RV, C_RG, C_SQ, C_SK, C_SV = 0, 1024, 1280, 1536, 2048, 2560, 3072, 3200

L_REAL = N_META + SEQ
BLK = 128
N_BLK = 65
LP = N_BLK * BLK
NP_ROWS = BATCH * LP
NS_ROWS = DEC_BATCH * DEC_SEQ
N_ROWS = NP_ROWS + NS_ROWS
LAST_REAL = L_REAL - (N_BLK - 1) * BLK

TM = 384
TO = 256
TT = 768
SG = 8
EC = 2048
PEER_LANES = 256
CONV_T = 640
CONV_RB = 64
VMEM_LIMIT = 58 * 1024 * 1024


def _cparams(sem, flags=None):
    return pltpu.CompilerParams(dimension_semantics=sem, vmem_limit_bytes=VMEM_LIMIT, flags=flags)


def _norm_proj_kernel(x_ref, g_ref, w_ref, o_ref):
    x = x_ref[...]
    ms = jnp.mean(x * x, axis=-1, keepdims=True)
    xn = x * lax.rsqrt(ms + EPS) * g_ref[...]
    o_ref[...] = jnp.dot(xn.astype(BF16), w_ref[...], preferred_element_type=F32)


def _norm_proj(h, g, w_bf, l):
    return pl.pallas_call(
        _norm_proj_kernel,
        out_shape=jax.ShapeDtypeStruct((N_ROWS, PROJ_COLS), F32),
        grid=(N_ROWS // TM,),
        in_specs=[pl.BlockSpec((TM, D_MODEL), lambda i: (i, 0)),
                  pl.BlockSpec((1, D_MODEL), lambda i: (0, 0)),
                  pl.BlockSpec((None, D_MODEL, PROJ_COLS), lambda i: (l, 0, 0))],
        out_specs=pl.BlockSpec((TM, PROJ_COLS), lambda i: (i, 0)),
        compiler_params=_cparams(("parallel",)),
        name="norm_proj",
    )(h, g.reshape(1, D_MODEL), w_bf)


def _conv_kernel(c_ref, w_ref, b_ref, g_ref, beta_ref, o_ref, st_ref, xin_ref, xs_ref):
    t = pl.program_id(1)

    @pl.when(t == 0)
    def _():
        xin_ref[0:32, :] = jnp.zeros((32, CONV_CH), F32)

    @pl.when(t > 0)
    def _():
        xin_ref[0:32, :] = xin_ref[CONV_T:CONV_T + 32, :]

    c = c_ref[...]
    xin_ref[32:32 + CONV_T, :] = c[:, :CONV_CH] * jax.nn.sigmoid(c[:, CONV_CH:])
    n_sh = 32 + CONV_T - 8
    for r in range(1, 8):
        xs_ref[r - 1, 0:n_sh, :] = xin_ref[r:r + n_sh, :]

    w = w_ref[...]
    bias = b_ref[...]
    gam = g_ref[...]
    beta = beta_ref[...]
    for rb in range(CONV_T // CONV_RB):
        r0 = rb * CONV_RB
        acc = jnp.zeros((CONV_RB, CONV_CH), F32)
        for k in range(CONV_WIDTH):
            sh = (2 + k) % 8
            lo = r0 + 2 + k - sh
            win = xin_ref[lo:lo + CONV_RB, :] if sh == 0 else xs_ref[sh - 1, lo:lo + CONV_RB, :]
            acc = acc + win * w[k:k + 1, :]
        y = acc + bias
        mu = jnp.mean(y, axis=-1, keepdims=True)
        d = y - mu
        var = jnp.mean(d * d, axis=-1, keepdims=True)
        yn = d * lax.rsqrt(var + EPS) * gam + beta
        o_ref[r0:r0 + CONV_RB, :] = yn * jax.nn.sigmoid(yn)

    @pl.when(t == pl.num_programs(1) - 1)
    def _():
        lo = 32 + (L_REAL - 32) - (LP - CONV_T)
        st_ref[0] = xin_ref[lo:lo + 32, :]


def _conv_prompt(p, w32, b, g, beta):
    nt = LP // CONV_T
    return pl.pallas_call(
        _conv_kernel,
        out_shape=(jax.ShapeDtypeStruct((NP_ROWS, CONV_CH), F32),
                   jax.ShapeDtypeStruct((BATCH, 32, CONV_CH), F32)),
        grid=(BATCH, nt),
        in_specs=[pl.BlockSpec((CONV_T, 2 * CONV_CH), lambda bi, t: (bi * nt + t, 0)),
                  pl.BlockSpec((32, CONV_CH), lambda bi, t: (0, 0)),
                  pl.BlockSpec((1, CONV_CH), lambda bi, t: (0, 0)),
                  pl.BlockSpec((1, CONV_CH), lambda bi, t: (0, 0)),
                  pl.BlockSpec((1, CONV_CH), lambda bi, t: (0, 0))],
        out_specs=(pl.BlockSpec((CONV_T, CONV_CH), lambda bi, t: (bi * nt + t, 0)),
                   pl.BlockSpec((1, 32, CONV_CH), lambda bi, t: (bi, 0, 0))),
        scratch_shapes=[pltpu.VMEM((32 + CONV_T, CONV_CH), F32),
                        pltpu.VMEM((7, 32 + CONV_T - 8, CONV_CH), F32)],
        compiler_params=_cparams(("arbitrary", "arbitrary")),
        name="conv_prompt",
    )(p, w32, b.reshape(1, -1), g.reshape(1, -1), beta.reshape(1, -1))


def _swap_halves(x, first_half):
    return jnp.where(first_half, pltpu.roll(x, x.shape[1] - 32, axis=1), pltpu.roll(x, 32, axis=1))


def _ret_kernel(q_ref, k_ref, v_ref, rg_ref, cos_ref, sin_ref, dmat_ref, dq_ref, dk_ref, sg_ref,
                gng_ref, gnb_ref, o_ref, st_ref, s_ref):
    j = pl.program_id(1)

    @pl.when(j == 0)
    def _():
        s_ref[...] = jnp.zeros_like(s_ref)

    cos = cos_ref[...]
    sin = sin_ref[...]
    lane = lax.broadcasted_iota(jnp.int32, (BLK, RET_HEADS * RET_DK), 1)
    first_half = (lane % RET_DK) < (RET_DK // 2)
    q = q_ref[...]
    k = k_ref[...]
    q = q * cos + _swap_halves(q, first_half) * sin
    k = (k * cos + _swap_halves(k, first_half) * sin) * (RET_DK ** -0.5)
    v = v_ref[...]
    rg = rg_ref[...]
    hs = range(RET_HEADS)
    qh = [q[:, h * RET_DK:(h + 1) * RET_DK].astype(BF16) for h in hs]
    vh = [v[:, h * RET_DV:(h + 1) * RET_DV].astype(BF16) for h in hs]
    inner, q_state, kdec_t = [], [], []
    for h in hs:
        kh = k[:, h * RET_DK:(h + 1) * RET_DK]
        inner.append(lax.dot_general(qh[h], kh.astype(BF16), (((1,), (1,)), ((), ())),
                                     preferred_element_type=F32) * dmat_ref[h])
        q_state.append(jnp.dot(qh[h], s_ref[h].astype(BF16), preferred_element_type=F32))
        kdec_t.append((kh * dk_ref[0, h]).T.astype(BF16))
    outs = []
    for h in hs:
        outs.append(jnp.dot(inner[h].astype(BF16), vh[h], preferred_element_type=F32) + dq_ref[h] * q_state[h])
        s_ref[h] = sg_ref[0, h, 0:1, :] * s_ref[h] + jnp.dot(kdec_t[h], vh[h], preferred_element_type=F32)
    for h in hs:
        o = outs[h]
        mu = jnp.mean(o, axis=-1, keepdims=True)
        d = o - mu
        var = jnp.mean(d * d, axis=-1, keepdims=True)
        y = d * lax.rsqrt(var + EPS) * gng_ref[:, h * RET_DV:(h + 1) * RET_DV] \
            + gnb_ref[:, h * RET_DV:(h + 1) * RET_DV]
        gate = rg[:, h * RET_DV:(h + 1) * RET_DV]
        o_ref[:, h * RET_DV:(h + 1) * RET_DV] = y * (gate * jax.nn.sigmoid(gate))

    @pl.when(j == pl.num_programs(1) - 1)
    def _():
        st_ref[0] = s_ref[...]


def _ret_tables():
    lg = jnp.log(1.0 - 2.0 ** (-5.0 - jnp.arange(RET_HEADS, dtype=F32)))
    i = jnp.arange(BLK, dtype=F32)
    diff = i[:, None] - i[None, :]
    dmat = jnp.where(diff >= 0, jnp.exp(jnp.maximum(diff, 0.0)[None] * lg[:, None, None]), 0.0)
    dq = jnp.broadcast_to(jnp.exp((i[None] + 1.0) * lg[:, None])[:, :, None], (RET_HEADS, BLK, RET_DV))

    def kdec(c_eff):
        e = jnp.where(i[None] < c_eff, jnp.exp((c_eff - 1.0 - i)[None] * lg[:, None]), 0.0)
        return jnp.broadcast_to(e[:, :, None], (RET_HEADS, BLK, RET_DK))

    def sgam(c_eff):
        return jnp.broadcast_to(jnp.exp(c_eff * lg)[:, None, None], (RET_HEADS, 8, RET_DV))

    dk = jnp.stack([kdec(float(BLK)), kdec(float(LAST_REAL))])
    sg = jnp.stack([sgam(float(BLK)), sgam(float(LAST_REAL))])
    return dmat, dq, dk, sg


def _rotary_tables(pos):
    half = RET_DK // 2
    inv = 1.0 / (10000.0 ** (jnp.arange(half, dtype=F32) / half))
    ang = pos.astype(F32)[:, None] * inv[None]
    cos, sin = jnp.cos(ang), jnp.sin(ang)
    cos_t = jnp.tile(jnp.concatenate([cos, cos], axis=-1), (1, RET_HEADS))
    sin_t = jnp.tile(jnp.concatenate([-sin, sin], axis=-1), (1, RET_HEADS))
    return cos_t, sin_t


def _ret_prompt(p, cos_t, sin_t, tabs, gn_g, gn_b):
    dmat, dq, dk, sg = tabs
    nb = N_BLK
    last = nb - 1
    return pl.pallas_call(
        _ret_kernel,
        out_shape=(jax.ShapeDtypeStruct((NP_ROWS, RET_HEADS * RET_DV), F32),
                   jax.ShapeDtypeStruct((BATCH, RET_HEADS, RET_DK, RET_DV), F32)),
        grid=(BATCH, nb),
        in_specs=[pl.BlockSpec((BLK, 256), lambda b, j: (b * nb + j, C_RQ // 256)),
                  pl.BlockSpec((BLK, 256), lambda b, j: (b * nb + j, C_RK // 256)),
                  pl.BlockSpec((BLK, 512), lambda b, j: (b * nb + j, C_RV // 512)),
                  pl.BlockSpec((BLK, 512), lambda b, j: (b * nb + j, C_RG // 512)),
                  pl.BlockSpec((BLK, 256), lambda b, j: (j, 0)),
                  pl.BlockSpec((BLK, 256), lambda b, j: (j, 0)),
                  pl.BlockSpec((RET_HEADS, BLK, BLK), lambda b, j: (0, 0, 0)),
                  pl.BlockSpec((RET_HEADS, BLK, RET_DV), lambda b, j: (0, 0, 0)),
                  pl.BlockSpec((1, RET_HEADS, BLK, RET_DK), lambda b, j: (j // last, 0, 0, 0)),
                  pl.BlockSpec((1, RET_HEADS, 8, RET_DV), lambda b, j: (j // last, 0, 0, 0)),
                  pl.BlockSpec((1, 512), lambda b, j: (0, 0)),
                  pl.BlockSpec((1, 512), lambda b, j: (0, 0))],
        out_specs=(pl.BlockSpec((BLK, 512), lambda b, j: (b * nb + j, 0)),
                   pl.BlockSpec((1, RET_HEADS, RET_DK, RET_DV), lambda b, j: (b, 0, 0, 0))),
        scratch_shapes=[pltpu.VMEM((RET_HEADS, RET_DK, RET_DV), F32)],
        compiler_params=_cparams(("arbitrary", "arbitrary")),
        name="ret_prompt",
    )(p, p, p, p, cos_t, sin_t, dmat, dq, dk, sg, gn_g.reshape(1, -1), gn_b.reshape(1, -1))


def _group_rms(x, ones_bd, w):
    x2 = x * x
    hi = x2.astype(BF16)
    lo = (x2 - hi.astype(F32)).astype(BF16)
    ss = (jnp.dot(hi, ones_bd, preferred_element_type=F32)
          + jnp.dot(lo, ones_bd, preferred_element_type=F32))
    return x * lax.rsqrt(ss * (1.0 / SWA_HD) + EPS) * w


def _swa_kernel(sinks_ref, q_ref, kc_ref, vc_ref, kp_ref, vp_ref, km_ref, vm_ref, ones_ref,
                qn_ref, kn_ref, bc_ref, bp_ref, bm_ref, o_ref, kout_ref):
    j = pl.program_id(1)
    ones_q = ones_ref[...]
    ones_k = ones_ref[0:128, 0:128]
    qw = qn_ref[...]
    kw = kn_ref[...]
    q = _group_rms(q_ref[...], ones_q, qw) * (SWA_HD ** -0.5)
    kc = _group_rms(kc_ref[...], ones_k, kw)
    kp = _group_rms(kp_ref[...], ones_k, kw)
    pad = jnp.zeros((BLK - N_META, 128), BF16)
    km = jnp.concatenate([_group_rms(km_ref[...], ones_k, kw).astype(BF16), pad], axis=0)
    kout_ref[...] = kc
    vc = vc_ref[...].astype(BF16)
    vp = vp_ref[...].astype(BF16)
    vm = jnp.concatenate([vm_ref[...].astype(BF16), pad], axis=0)

    rows = SWA_GROUP * BLK
    qi = lax.broadcasted_iota(jnp.int32, (rows, BLK), 0) % BLK
    kj = lax.broadcasted_iota(jnp.int32, (rows, BLK), 1)
    valid_c = jnp.where(kj <= qi, j * BLK + kj, -1) >= N_META
    valid_p = jnp.where(kj > qi, (j - 1) * BLK + kj, -1) >= N_META
    valid_m = jnp.where(kj < N_META, kj, LP) <= j * BLK + qi

    dn = (((1,), (1,)), ((), ()))
    kvs = range(SWA_KV_HEADS)
    heads = [range(kv * SWA_GROUP, (kv + 1) * SWA_GROUP) for kv in kvs]
    ksl = [slice(kv * SWA_HD, (kv + 1) * SWA_HD) for kv in kvs]
    s_c, s_p, s_m = [], [], []
    for kv in kvs:
        qs = jnp.concatenate([q[:, h * SWA_HD:(h + 1) * SWA_HD] for h in heads[kv]], axis=0).astype(BF16)
        s_c.append(lax.dot_general(qs, kc[:, ksl[kv]].astype(BF16), dn, preferred_element_type=F32))
        s_p.append(lax.dot_general(qs, kp[:, ksl[kv]].astype(BF16), dn, preferred_element_type=F32))
        s_m.append(lax.dot_general(qs, km[:, ksl[kv]], dn, preferred_element_type=F32))
    p_c, p_p, p_m, den = [], [], [], []
    for kv in kvs:
        sc = jnp.where(valid_c, s_c[kv] + bc_ref[kv], NEG)
        sp = jnp.where(valid_p, s_p[kv] + bp_ref[kv], NEG)
        sm = jnp.where(valid_m, s_m[kv] + bm_ref[0, kv], NEG)
        sink = jnp.concatenate([jnp.full((BLK, 1), sinks_ref[h], F32) for h in heads[kv]], axis=0)
        m = jnp.maximum(jnp.max(jnp.maximum(jnp.maximum(sc, sp), sm), axis=-1, keepdims=True), sink)
        p_c.append(jnp.exp(sc - m))
        p_p.append(jnp.exp(sp - m))
        p_m.append(jnp.exp(sm - m))
        den.append(jnp.sum(p_c[kv] + p_p[kv] + p_m[kv], axis=-1, keepdims=True) + jnp.exp(sink - m))
    for kv in kvs:
        acc = (jnp.dot(p_c[kv].astype(BF16), vc[:, ksl[kv]], preferred_element_type=F32)
               + jnp.dot(p_p[kv].astype(BF16), vp[:, ksl[kv]], preferred_element_type=F32)
               + jnp.dot(p_m[kv].astype(BF16), vm[:, ksl[kv]], preferred_element_type=F32))
        out = acc / den[kv]
        for g, h in enumerate(heads[kv]):
            o_ref[:, h * SWA_HD:(h + 1) * SWA_HD] = out[g * BLK:(g + 1) * BLK, :]


def _one_hot(idx, n):
    return jnp.asarray(np.asarray(idx)[..., None] == np.arange(n), F32)


def _t5_bucket_np(dist):
    max_exact = N_BUCKETS // 2
    df = np.maximum(dist, 1).astype(np.float64)
    large = max_exact + (np.log(df / max_exact) / math.log(REL_MAX_DIST / max_exact)
                         * (N_BUCKETS - max_exact)).astype(np.int64)
    large = np.minimum(large, N_BUCKETS - 1)
    return np.where(dist < max_exact, dist, large).astype(np.int32)


def _bias_tables(rel_bias):
    qi = np.arange(BLK)[:, None]
    kj = np.arange(BLK)[None, :]
    b_cur = _t5_bucket_np(np.maximum(qi - kj, 0))
    b_prev = _t5_bucket_np(np.maximum(qi - kj + BLK, 0))
    m = np.arange(N_META)[None, :]
    b_m0 = _t5_bucket_np(np.maximum(qi - m, 0))
    b_m1 = _t5_bucket_np(np.maximum(qi + BLK - m, 0))
    assert (b_m1 == N_BUCKETS - 1).all()
    rb = rel_bias.astype(F32)

    def look(bk):
        return jnp.einsum("...b,bh->h...", _one_hot(bk, N_BUCKETS), rb, precision=lax.Precision.HIGHEST)

    return look(b_cur), look(b_prev), jnp.stack([look(b_m0), look(b_m1)])


def _swa_prompt(p, sinks, q_norm, k_norm, bias_tabs):
    srows = SWA_GROUP * BLK
    b_cur = bias_tabs[0].reshape(SWA_KV_HEADS, srows, BLK)
    b_prev = bias_tabs[1].reshape(SWA_KV_HEADS, srows, BLK)
    b_meta = jnp.pad(bias_tabs[2].reshape(2, SWA_KV_HEADS, srows, N_META),
                     ((0, 0), (0, 0), (0, 0), (0, BLK - N_META)))
    nb = N_BLK
    ones_bd = jnp.asarray(np.kron(np.eye(SWA_HEADS), np.ones((SWA_HD, SWA_HD))), BF16)
    qn = jnp.tile(q_norm.astype(F32), SWA_HEADS).reshape(1, -1)
    kn = jnp.tile(k_norm.astype(F32), SWA_KV_HEADS).reshape(1, -1)
    ck, cv = C_SK // 128, C_SV // 128
    return pl.pallas_call(
        _swa_kernel,
        out_shape=(jax.ShapeDtypeStruct((NP_ROWS, SWA_HEADS * SWA_HD), F32),
                   jax.ShapeDtypeStruct((NP_ROWS, SWA_KV_HEADS * SWA_HD), F32)),
        grid=(BATCH, nb),
        in_specs=[pl.BlockSpec(memory_space=pltpu.SMEM),
                  pl.BlockSpec((BLK, 512), lambda b, j: (b * nb + j, C_SQ // 512)),
                  pl.BlockSpec((BLK, 128), lambda b, j: (b * nb + j, ck)),
                  pl.BlockSpec((BLK, 128), lambda b, j: (b * nb + j, cv)),
                  pl.BlockSpec((BLK, 128), lambda b, j: (b * nb + jnp.maximum(j - 1, 0), ck)),
                  pl.BlockSpec((BLK, 128), lambda b, j: (b * nb + jnp.maximum(j - 1, 0), cv)),
                  pl.BlockSpec((N_META, 128), lambda b, j: (b * (LP // N_META), ck)),
                  pl.BlockSpec((N_META, 128), lambda b, j: (b * (LP // N_META), cv)),
                  pl.BlockSpec((512, 512), lambda b, j: (0, 0)),
                  pl.BlockSpec((1, 512), lambda b, j: (0, 0)),
                  pl.BlockSpec((1, 128), lambda b, j: (0, 0)),
                  pl.BlockSpec((SWA_KV_HEADS, srows, BLK), lambda b, j: (0, 0, 0)),
                  pl.BlockSpec((SWA_KV_HEADS, srows, BLK), lambda b, j: (0, 0, 0)),
                  pl.BlockSpec((1, SWA_KV_HEADS, srows, BLK), lambda b, j: (jnp.minimum(j, 1), 0, 0, 0))],
        out_specs=(pl.BlockSpec((BLK, 512), lambda b, j: (b * nb + j, 0)),
                   pl.BlockSpec((BLK, 128), lambda b, j: (b * nb + j, 0))),
        compiler_params=_cparams(("parallel", "parallel")),
        name="swa_prompt",
    )(sinks.astype(F32), p, p, p, p, p, p, p, ones_bd, qn, kn, b_cur, b_prev, b_meta)


def _out_ffn_kernel(h_ref, a_ref, b_ref, c_ref, as_ref, bs_ref, cs_ref, wo_ref, g_ref, wq_ref, kbd_ref,
                    h1_ref, xn_ref, st_ref):
    is_prompt = pl.program_id(0) < NP_ROWS // TO
    a = jnp.where(is_prompt, a_ref[...], as_ref[...])
    b = jnp.where(is_prompt, b_ref[...], bs_ref[...])
    c = jnp.where(is_prompt, c_ref[...], cs_ref[...])
    h1 = (h_ref[...]
          + jnp.dot(a.astype(BF16), wo_ref[0:512, :], preferred_element_type=F32)
          + jnp.dot(b.astype(BF16), wo_ref[512:1024, :], preferred_element_type=F32)
          + jnp.dot(c.astype(BF16), wo_ref[1024:1536, :], preferred_element_type=F32))
    h1_ref[...] = h1
    ms = jnp.mean(h1 * h1, axis=-1, keepdims=True)
    xn_f = h1 * lax.rsqrt(ms + EPS) * g_ref[...]
    xn_ref[...] = xn_f.T.astype(BF16)
    q = jnp.dot(xn_f.astype(BF16), wq_ref[...], preferred_element_type=F32).astype(BF16)
    kbd = kbd_ref[...]
    for hh in range(PEER_HEADS):
        st_ref[hh] = lax.dot_general(kbd, q[:, hh * 128:(hh + 1) * 128], (((1,), (1,)), ((), ())),
                                     preferred_element_type=F32)


def _out_ffn(h, mix_p, mix_s, wo_bf, g, wq_bf, kbd_bf, l):
    npb = NP_ROWS // TO
    pspec = pl.BlockSpec((TO, 512), lambda i: (jnp.minimum(i, npb - 1), 0))
    sspec = pl.BlockSpec((TO, 512), lambda i: (jnp.maximum(i - npb, 0), 0))
    return pl.pallas_call(
        _out_ffn_kernel,
        out_shape=(jax.ShapeDtypeStruct((N_ROWS, D_MODEL), F32),
                   jax.ShapeDtypeStruct((D_MODEL, N_ROWS), BF16),
                   jax.ShapeDtypeStruct((PEER_HEADS, 2 * N_KEYS, N_ROWS), F32)),
        grid=(N_ROWS // TO,),
        in_specs=[pl.BlockSpec((TO, D_MODEL), lambda i: (i, 0)),
                  pspec, pspec, pspec, sspec, sspec, sspec,
                  pl.BlockSpec((None, 1536, D_MODEL), lambda i: (l, 0, 0)),
                  pl.BlockSpec((1, D_MODEL), lambda i: (0, 0)),
                  pl.BlockSpec((None, D_MODEL, D_MODEL), lambda i: (l, 0, 0)),
                  pl.BlockSpec((2 * N_KEYS, 128), lambda i: (0, 0))],
        out_specs=(pl.BlockSpec((TO, D_MODEL), lambda i: (i, 0)),
                   pl.BlockSpec((D_MODEL, TO), lambda i: (0, i)),
                   pl.BlockSpec((PEER_HEADS, 2 * N_KEYS, TO), lambda i: (0, 0, i))),
        compiler_params=_cparams(("parallel",)),
        name="out_ffn",
    )(h, *mix_p, *mix_s, wo_bf, g.reshape(1, -1), wq_bf, kbd_bf)


def _top_vals(x, n, with_rank=False):
    vals = []
    rank = jnp.full(x.shape, float(n), F32)
    for r in range(n):
        mx = jnp.max(x, axis=0, keepdims=True)
        vals.append(mx)
        hit = x == mx
        if with_rank:
            rank = jnp.where(hit, float(r), rank)
        x = jnp.where(hit, NEG, x)
    return (vals, rank) if with_rank else vals


def _top_vals_paired(x, n):
    half = x.shape[0] // 2
    hi = jnp.maximum(x[:half], x[half:])
    lo = jnp.minimum(x[:half], x[half:])
    vals = []
    for _ in range(n):
        mx = jnp.max(hi, axis=0, keepdims=True)
        vals.append(mx)
        hit = hi == mx
        hi = jnp.where(hit, lo, hi)
        lo = jnp.where(hit, NEG, lo)
    return vals


def _topk_kernel(st_ref, rho_ref, c1_ref, r2_ref, e2_ref):
    s1 = st_ref[0, 0:N_KEYS, :]
    s2 = st_ref[0, N_KEYS:2 * N_KEYS, :]
    v1 = _top_vals_paired(s1, PEER_TOPK)
    v2, rank2 = _top_vals(s2, PEER_TOPK, with_rank=True)
    sv1 = jnp.concatenate(v1, axis=0)
    sv2 = jnp.concatenate(v2, axis=0)
    cand = jnp.concatenate([v1[0] + sv2, v1[1] + sv2]
                           + [v1[a] + sv2[0:8] for a in range(2, 8)]
                           + [sv1[8:16] + v2[0], jnp.full((8, s1.shape[1]), NEG, F32)], axis=0)
    top = _top_vals_paired(cand, PEER_TOPK)
    tau = top[PEER_TOPK - 1]
    z = jnp.ones_like(tau)
    for r in range(1, PEER_TOPK):
        z = z + jnp.exp(top[r] - top[0])
    rho = jnp.zeros(s1.shape, F32)
    for a in range(PEER_TOPK):
        n_a = jnp.sum(jnp.where(v1[a] + sv2 >= tau, 1.0, 0.0), axis=0, keepdims=True)
        rho = jnp.where(s1 == v1[a], n_a, rho)
    rho_ref[0] = rho
    c1_ref[0] = jnp.exp(s1 - v1[0]) * (0.5 / z)
    r2_ref[0] = rank2.astype(BF16)
    e2_ref[0] = jnp.exp(s2 - v2[0]).astype(BF16)


def _topk(st):
    shp = jax.ShapeDtypeStruct((PEER_HEADS, N_KEYS, N_ROWS), F32)
    shp_bf = jax.ShapeDtypeStruct((PEER_HEADS, N_KEYS, N_ROWS), BF16)
    spec = pl.BlockSpec((1, N_KEYS, TM), lambda i, h: (h, 0, i))
    return pl.pallas_call(
        _topk_kernel,
        out_shape=(shp, shp, shp_bf, shp_bf),
        grid=(N_ROWS // TM, PEER_HEADS),
        in_specs=[pl.BlockSpec((1, 2 * N_KEYS, TM), lambda i, h: (h, 0, i))],
        out_specs=(spec, spec, spec, spec),
        compiler_params=_cparams(("parallel", "parallel")),
        name="peer_topk",
    )(st)


N_CHUNK = N_EXPERTS // EC
N_WORK = (N_ROWS // TT) * N_CHUNK


def _peer_stages(xn_ref, u_ref, vt_ref, rho_ref, c1_ref, r2_ref, e2_ref, acc_ref,
                 at_w, at_r, gt_w, gt_r):
    sub = 16
    n_tb = TT // PEER_LANES
    n_ii = EC // N_KEYS

    def gate_build(ii, tb, jbs):
        cols = slice(tb * PEER_LANES, (tb + 1) * PEER_LANES)
        w = {jb: jnp.zeros((sub, PEER_LANES), BF16) for jb in jbs}
        for hh in range(PEER_HEADS):
            rho = jnp.broadcast_to(rho_ref[hh, ii:ii + 1, cols], (sub, PEER_LANES)).astype(BF16)
            cc = jnp.broadcast_to(c1_ref[hh, ii:ii + 1, cols], (sub, PEER_LANES)).astype(BF16)
            for jb in jbs:
                jr = slice(jb * sub, (jb + 1) * sub)
                w[jb] = w[jb] + jnp.where(r2_ref[hh, jr, cols] < rho, e2_ref[hh, jr, cols] * cc, 0.0)
        for jb in jbs:
            rows = slice(ii * N_KEYS + jb * sub, ii * N_KEYS + (jb + 1) * sub)
            a = at_r[rows, cols]
            gt_w[rows, cols] = w[jb] * (a * (1.0 + lax.erf(a * (2.0 ** -0.5))))

    at_w[...] = jnp.dot(u_ref[...], xn_ref[...], preferred_element_type=F32).astype(BF16)
    acc_ref[...] += jnp.dot(vt_ref[...], gt_r[...], preferred_element_type=F32)
    all_jb = range(N_KEYS // sub)
    for ii in range(n_ii):
        for tb in range(n_tb):
            gate_build(ii, tb, all_jb)


def _peer_kernel(xn_ref, h1_ref, u_ref, vt_ref, rho_ref, c1_ref, r2_ref, e2_ref, o_ref,
                 acc_ref, at, gt):
    g = pl.program_id(0)
    cv = (g - 1) % N_CHUNK
    slot = g % 2

    @pl.when(g == 0)
    def _():
        gt[1] = jnp.zeros((EC, TT), BF16)

    @pl.when(jnp.logical_or(g < 1, cv == 0))
    def _():
        acc_ref[...] = jnp.zeros_like(acc_ref)

    _peer_stages(xn_ref, u_ref, vt_ref, rho_ref, c1_ref, r2_ref, e2_ref, acc_ref,
                 at, at, gt.at[slot], gt.at[1 - slot])

    @pl.when(jnp.logical_and(g >= 1, cv == N_CHUNK - 1))
    def _():
        o_ref[...] = h1_ref[...] + acc_ref[...].T


def _peer(xn, h1, u_bf, vt_bf, rho, c1, r2, e2, l):
    nk = EC // N_KEYS

    def item(g, lag):
        w = jnp.clip(g - lag, 0, N_WORK - 1)
        return w // N_CHUNK, w % N_CHUNK

    return pl.pallas_call(
        _peer_kernel,
        out_shape=jax.ShapeDtypeStruct((N_ROWS, D_MODEL), F32),
        grid=(N_WORK + 1,),
        in_specs=[pl.BlockSpec((D_MODEL, TT), lambda g: (0, item(g, 0)[0])),
                  pl.BlockSpec((TT, D_MODEL), lambda g: (item(g, 1)[0], 0)),
                  pl.BlockSpec((None, EC, D_MODEL), lambda g: (l, item(g, 0)[1], 0)),
                  pl.BlockSpec((None, D_MODEL, EC), lambda g: (l, 0, item(g, 1)[1])),
                  pl.BlockSpec((PEER_HEADS, nk, TT), lambda g: (0, item(g, 0)[1], item(g, 0)[0])),
                  pl.BlockSpec((PEER_HEADS, nk, TT), lambda g: (0, item(g, 0)[1], item(g, 0)[0])),
                  pl.BlockSpec((PEER_HEADS, N_KEYS, TT), lambda g: (0, 0, item(g, 0)[0])),
                  pl.BlockSpec((PEER_HEADS, N_KEYS, TT), lambda g: (0, 0, item(g, 0)[0]))],
        out_specs=pl.BlockSpec((TT, D_MODEL), lambda g: (item(g, 1)[0], 0)),
        scratch_shapes=[pltpu.VMEM((D_MODEL, TT), F32), pltpu.VMEM((EC, TT), BF16),
                        pltpu.VMEM((2, EC, TT), BF16)],
        compiler_params=_cparams(("arbitrary",)),
        name="peer_dense",
    )(xn, h1, u_bf, vt_bf, rho, c1, r2, e2)


S_STACK = SWA_GROUP * DEC_SEQ
S_SMALL = N_META + DEC_SEQ
NT_DIMS = (((1,), (1,)), ((), ()))


def _sample_kernel(p_ref, cs_ref, rs_ref, meta_ref, win_ref,
                   cw_ref, cb_ref, cg_ref, cbeta_ref,
                   cos_ref, sin_ref, dmat_ref, dq_ref, dk_ref, sgam_ref, gng_ref, gnb_ref, eye_ref,
                   ones_ref, qn_ref, kn_ref, bwin_ref, bsm_ref, sink_ref,
                   conv_o_ref, ret_o_ref, swa_o_ref, ncs_ref, nrs_ref, nwin_ref,
                   xin_ref, kpad_ref, vpad_ref, ksm_ref, vsm_ref):
    kpad_ref[...] = jnp.zeros_like(kpad_ref)
    vpad_ref[...] = jnp.zeros_like(vpad_ref)
    ksm_ref[...] = jnp.zeros_like(ksm_ref)
    vsm_ref[...] = jnp.zeros_like(vsm_ref)
    lane = lax.broadcasted_iota(jnp.int32, (DEC_SEQ, RET_HEADS * RET_DK), 1)
    first_half = (lane % RET_DK) < (RET_DK // 2)
    cos = cos_ref[...]
    sin = sin_ref[...]
    ones_q = ones_ref[...]
    ones_k = ones_ref[0:128, 0:128]

    seqs = range(SG)
    rows = [slice(s * DEC_SEQ, (s + 1) * DEC_SEQ) for s in seqs]
    xin = [xin_ref.at[s] for s in seqs]
    kpad = [kpad_ref.at[s] for s in seqs]
    vpad = [vpad_ref.at[s] for s in seqs]
    ksm = [ksm_ref.at[s] for s in seqs]
    vsm = [vsm_ref.at[s] for s in seqs]

    for s in seqs:
        glu = (p_ref[rows[s], C_CONV:C_CONV + CONV_CH]
               * jax.nn.sigmoid(p_ref[rows[s], C_CONV + CONV_CH:C_RQ]))
        xin[s][0:CONV_WIDTH - 1, :] = cs_ref[0, s]
        xin[s][CONV_WIDTH - 1:CONV_WIDTH - 1 + DEC_SEQ, :] = glu
    acc = [jnp.zeros((DEC_SEQ, CONV_CH), F32) for _ in seqs]
    for k in range(CONV_WIDTH):
        wk = cw_ref[k:k + 1, :]
        for s in seqs:
            acc[s] = acc[s] + xin[s][k:k + DEC_SEQ, :] * wk
    for s in seqs:
        y = acc[s] + cb_ref[...]
        mu = jnp.mean(y, axis=-1, keepdims=True)
        d = y - mu
        var = jnp.mean(d * d, axis=-1, keepdims=True)
        yn = d * lax.rsqrt(var + EPS) * cg_ref[...] + cbeta_ref[...]
        conv_o_ref[rows[s], :] = yn * jax.nn.sigmoid(yn)
        ncs_ref[s] = xin[s][DEC_SEQ:DEC_SEQ + CONV_WIDTH - 1, :]

    qr = []
    for s in seqs:
        q = p_ref[rows[s], C_RQ:C_RK]
        k = p_ref[rows[s], C_RK:C_RV]
        qr.append(q * cos + _swap_halves(q, first_half) * sin)
        kpad[s][0:DEC_SEQ, :] = (k * cos + _swap_halves(k, first_half) * sin) * (RET_DK ** -0.5)
        vpad[s][0:DEC_SEQ, :] = p_ref[rows[s], C_RV:C_RG]
    pairs = [(s, h) for h in range(RET_HEADS) for s in seqs]
    inner, q_state, kdec_t = {}, {}, {}
    for s, h in pairs:
        qh = qr[s][:, h * RET_DK:(h + 1) * RET_DK]
        kp = kpad[s][:, h * RET_DK:(h + 1) * RET_DK]
        inner[s, h] = lax.dot_general(qh, kp, NT_DIMS, preferred_element_type=F32) * dmat_ref[h]
        q_state[s, h] = jnp.dot(qh, rs_ref[0, s, h], preferred_element_type=F32)
        kdec_t[s, h] = lax.dot_general(eye_ref[...], kp * dk_ref[h], NT_DIMS,
                                       preferred_element_type=F32)
    ret_raw = {}
    for s, h in pairs:
        vp = vpad[s][:, h * RET_DV:(h + 1) * RET_DV]
        ret_raw[s, h] = jnp.dot(inner[s, h], vp, preferred_element_type=F32) + dq_ref[h] * q_state[s, h]
        nrs_ref[s, h] = sgam_ref[h] * rs_ref[0, s, h] + jnp.dot(kdec_t[s, h], vp, preferred_element_type=F32)
    for s, h in pairs:
        o = ret_raw[s, h]
        mu = jnp.mean(o, axis=-1, keepdims=True)
        d = o - mu
        var = jnp.mean(d * d, axis=-1, keepdims=True)
        hs = slice(h * RET_DV, (h + 1) * RET_DV)
        yr = d * lax.rsqrt(var + EPS) * gng_ref[:, hs] + gnb_ref[:, hs]
        gate = p_ref[rows[s], C_RG + h * RET_DV:C_RG + (h + 1) * RET_DV]
        ret_o_ref[rows[s], hs] = yr * (gate * jax.nn.sigmoid(gate))

    qss = [jnp.dot(jnp.square(p_ref[rows[s], C_SQ:C_SK]), ones_q, preferred_element_type=F32) for s in seqs]
    kss = [jnp.dot(jnp.square(p_ref[rows[s], C_SK:C_SV]), ones_k, preferred_element_type=F32) for s in seqs]
    qa = []
    for s in seqs:
        qa.append(p_ref[rows[s], C_SQ:C_SK] * lax.rsqrt(qss[s] * (1.0 / SWA_HD) + EPS)
                  * qn_ref[...] * (SWA_HD ** -0.5))
        kn = p_ref[rows[s], C_SK:C_SV] * lax.rsqrt(kss[s] * (1.0 / SWA_HD) + EPS) * kn_ref[...]
        vn = p_ref[rows[s], C_SV:PROJ_COLS]
        ksm[s][0:N_META, :] = meta_ref[0, s, :, 0:128]
        vsm[s][0:N_META, :] = meta_ref[0, s, :, 128:256]
        ksm[s][N_META:S_SMALL, :] = kn
        vsm[s][N_META:S_SMALL, :] = vn
        nwin_ref[s, 0:WINDOW - DEC_SEQ, :] = win_ref[0, s, DEC_SEQ:WINDOW, :]
        nwin_ref[s, WINDOW - DEC_SEQ:WINDOW, 0:128] = kn
        nwin_ref[s, WINDOW - DEC_SEQ:WINDOW, 128:256] = vn
    spairs = [(s, kv) for kv in range(SWA_KV_HEADS) for s in seqs]
    s_w, s_s = {}, {}
    for s, kv in spairs:
        ksl = slice(kv * SWA_HD, (kv + 1) * SWA_HD)
        qs = jnp.concatenate([qa[s][:, (kv * SWA_GROUP + g) * SWA_HD:(kv * SWA_GROUP + g + 1) * SWA_HD]
                              for g in range(SWA_GROUP)], axis=0)
        s_w[s, kv] = lax.dot_general(qs, win_ref[0, s, :, ksl], NT_DIMS, preferred_element_type=F32) + bwin_ref[kv]
        s_s[s, kv] = lax.dot_general(qs, ksm[s][:, ksl], NT_DIMS, preferred_element_type=F32) + bsm_ref[kv]
    p_w, p_s, den = {}, {}, {}
    for s, kv in spairs:
        sink = sink_ref[kv, :, 0:1]
        m = jnp.maximum(jnp.maximum(jnp.max(s_w[s, kv], axis=-1, keepdims=True),
                                    jnp.max(s_s[s, kv], axis=-1, keepdims=True)), sink)
        p_w[s, kv] = jnp.exp(s_w[s, kv] - m)
        p_s[s, kv] = jnp.exp(s_s[s, kv] - m)
        den[s, kv] = (jnp.sum(p_w[s, kv], axis=-1, keepdims=True) + jnp.sum(p_s[s, kv], axis=-1, keepdims=True)
                      + jnp.exp(sink - m))
    for s, kv in spairs:
        ksl = slice(kv * SWA_HD, (kv + 1) * SWA_HD)
        vsl = slice(128 + kv * SWA_HD, 128 + (kv + 1) * SWA_HD)
        o = (jnp.dot(p_w[s, kv], win_ref[0, s, :, vsl], preferred_element_type=F32)
             + jnp.dot(p_s[s, kv], vsm[s][:, ksl], preferred_element_type=F32)) / den[s, kv]
        for g in range(SWA_GROUP):
            hh = kv * SWA_GROUP + g
            swa_o_ref[rows[s], hh * SWA_HD:(hh + 1) * SWA_HD] = o[g * DEC_SEQ:(g + 1) * DEC_SEQ, :]


def _sample_tables(rel_bias):
    lg = jnp.log(1.0 - 2.0 ** (-5.0 - jnp.arange(RET_HEADS, dtype=F32)))
    i = jnp.arange(DEC_SEQ, dtype=F32)
    diff = i[:, None] - i[None, :]
    dm = jnp.where(diff >= 0, jnp.exp(jnp.maximum(diff, 0.0)[None] * lg[:, None, None]), 0.0)
    dmat = jnp.zeros((RET_HEADS, DEC_SEQ, 128), F32).at[:, :, :DEC_SEQ].set(dm)
    dq = jnp.broadcast_to(jnp.exp((i[None] + 1.0) * lg[:, None])[:, :, None], (RET_HEADS, DEC_SEQ, RET_DV))
    kd = jnp.exp((DEC_SEQ - 1.0 - i)[None] * lg[:, None])
    dk = jnp.zeros((RET_HEADS, 128, RET_DK), F32).at[:, :DEC_SEQ, :].set(
        jnp.broadcast_to(kd[:, :, None], (RET_HEADS, DEC_SEQ, RET_DK)))
    sg = jnp.broadcast_to(jnp.exp(DEC_SEQ * lg)[:, None, None], (RET_HEADS, 1, RET_DV))
    cos_t, sin_t = _rotary_tables(PAST_LEN + jnp.arange(DEC_SEQ))

    ti = np.arange(S_STACK)[:, None] % DEC_SEQ
    j = np.arange(WINDOW)[None, :]
    bk_win = _t5_bucket_np(np.maximum(ti + WINDOW - j, 0))
    ok_win = j > ti
    c = np.arange(128)[None, :]
    jn = c - N_META
    bk_new = _t5_bucket_np(np.clip(ti - jn, 0, None))
    ok_sm = (c < N_META) | ((c < S_SMALL) & (jn <= ti))
    bk_sm = np.where(c < N_META, N_BUCKETS - 1, bk_new)
    rb = rel_bias.astype(F32)
    head = np.arange(SWA_KV_HEADS)[:, None] * SWA_GROUP + (np.arange(S_STACK) // DEC_SEQ)[None, :]
    head_oh = _one_hot(head, SWA_HEADS)

    def look(bk, ok):
        b = jnp.einsum("rcb,bh,krh->krc", _one_hot(bk, N_BUCKETS), rb, head_oh,
                       precision=lax.Precision.HIGHEST)
        return jnp.where(jnp.asarray(np.broadcast_to(ok[None], b.shape)), b, NEG)

    return dict(dmat=dmat, dq=dq, dk=dk, sg=sg, cos=cos_t, sin=sin_t,
                bwin=look(bk_win, ok_win), bsm=look(bk_sm, ok_sm), head_oh=head_oh)


def _sample_mixers(p, l, cache_meta_kv, cache_swa_kv, state_ret, state_conv, tabs, conv_w32, conv_b,
                   conv_g, conv_beta, gn_g, gn_b, q_norm, k_norm, sinks):
    row0 = NP_ROWS // (SG * DEC_SEQ)
    nrow = SG * DEC_SEQ
    ones_bd = jnp.asarray(np.kron(np.eye(SWA_HEADS), np.ones((SWA_HD, SWA_HD))), F32)
    qn = jnp.tile(q_norm.astype(F32), SWA_HEADS).reshape(1, -1)
    kn = jnp.tile(k_norm.astype(F32), SWA_KV_HEADS).reshape(1, -1)
    sink_rows = jnp.einsum("h,krh->kr", sinks.astype(F32), tabs["head_oh"], precision=lax.Precision.HIGHEST)
    sink_t = jnp.broadcast_to(sink_rows[:, :, None], (SWA_KV_HEADS, S_STACK, 128))
    meta = cache_meta_kv.reshape(DEPTH, DEC_BATCH, N_META, 256)
    win = cache_swa_kv.reshape(DEPTH, DEC_BATCH, WINDOW, 256)

    def const(shape):
        return pl.BlockSpec(shape, lambda i: (0,) * len(shape))

    return pl.pallas_call(
        _sample_kernel,
        out_shape=(jax.ShapeDtypeStruct((NS_ROWS, CONV_CH), F32),
                   jax.ShapeDtypeStruct((NS_ROWS, RET_HEADS * RET_DV), F32),
                   jax.ShapeDtypeStruct((NS_ROWS, SWA_HEADS * SWA_HD), F32),
                   jax.ShapeDtypeStruct((DEC_BATCH, CONV_WIDTH - 1, CONV_CH), F32),
                   jax.ShapeDtypeStruct((DEC_BATCH, RET_HEADS, RET_DK, RET_DV), F32),
                   jax.ShapeDtypeStruct((DEC_BATCH, WINDOW, 256), F32)),
        grid=(DEC_BATCH // SG,),
        in_specs=[pl.BlockSpec((nrow, PROJ_COLS), lambda i: (row0 + i, 0)),
                  pl.BlockSpec((1, SG, CONV_WIDTH - 1, CONV_CH), lambda i: (l, i, 0, 0)),
                  pl.BlockSpec((1, SG, RET_HEADS, RET_DK, RET_DV), lambda i: (l, i, 0, 0, 0)),
                  pl.BlockSpec((1, SG, N_META, 256), lambda i: (l, i, 0, 0)),
                  pl.BlockSpec((1, SG, WINDOW, 256), lambda i: (l, i, 0, 0)),
                  const((32, CONV_CH)), const((1, CONV_CH)), const((1, CONV_CH)), const((1, CONV_CH)),
                  const((DEC_SEQ, 256)), const((DEC_SEQ, 256)),
                  const((RET_HEADS, DEC_SEQ, 128)), const((RET_HEADS, DEC_SEQ, RET_DV)),
                  const((RET_HEADS, 128, RET_DK)), const((RET_HEADS, 1, RET_DV)),
                  const((1, 512)), const((1, 512)), const((RET_DK, RET_DK)),
                  const((512, 512)), const((1, 512)), const((1, 128)),
                  const((SWA_KV_HEADS, S_STACK, 128)), const((SWA_KV_HEADS, S_STACK, 128)),
                  const((SWA_KV_HEADS, S_STACK, 128))],
        out_specs=(pl.BlockSpec((nrow, CONV_CH), lambda i: (i, 0)),
                   pl.BlockSpec((nrow, 512), lambda i: (i, 0)),
                   pl.BlockSpec((nrow, 512), lambda i: (i, 0)),
                   pl.BlockSpec((SG, CONV_WIDTH - 1, CONV_CH), lambda i: (i, 0, 0)),
                   pl.BlockSpec((SG, RET_HEADS, RET_DK, RET_DV), lambda i: (i, 0, 0, 0)),
                   pl.BlockSpec((SG, WINDOW, 256), lambda i: (i, 0, 0))),
        scratch_shapes=[pltpu.VMEM((SG, 40, CONV_CH), F32), pltpu.VMEM((SG, 128, RET_HEADS * RET_DK), F32),
                        pltpu.VMEM((SG, 128, RET_HEADS * RET_DV), F32),
                        pltpu.VMEM((SG, 128, 128), F32), pltpu.VMEM((SG, 128, 128), F32)],
        compiler_params=_cparams(("parallel",)),
        name="sample_mixers",
    )(p, state_conv, state_ret, meta, win,
      conv_w32, conv_b.reshape(1, -1), conv_g.reshape(1, -1), conv_beta.reshape(1, -1),
      tabs["cos"], tabs["sin"], tabs["dmat"], tabs["dq"], tabs["dk"], tabs["sg"],
      gn_g.reshape(1, -1), gn_b.reshape(1, -1), jnp.eye(RET_DK, dtype=F32),
      ones_bd, qn, kn, tabs["bwin"], tabs["bsm"], sink_t)


def kernel(x_prompt, x_sample, cache_meta_kv, cache_swa_kv, state_ret, state_conv, meta_tokens, rel_bias,
           norm_mix, w_in, conv_w, conv_b, conv_ln_g, conv_ln_b, ret_gn_g, ret_gn_b, swa_q_norm,
           swa_k_norm, swa_sinks, w_out, norm_ffn, peer_wq, peer_keys, peer_u, peer_v):
    meta = jnp.broadcast_to(meta_tokens.astype(F32)[None], (BATCH, N_META, D_MODEL))
    pad = jnp.zeros((BATCH, LP - L_REAL, D_MODEL), F32)
    hp = jnp.concatenate([meta, x_prompt, pad], axis=1).reshape(NP_ROWS, D_MODEL)
    h = jnp.concatenate([hp, x_sample.reshape(NS_ROWS, D_MODEL)], axis=0)

    cos_t, sin_t = _rotary_tables(jnp.arange(LP))
    ret_tabs = _ret_tables()
    bias_tabs = _bias_tables(rel_bias)
    sample_tabs = _sample_tables(rel_bias)

    w_in_bf, w_out_bf, wq_bf = w_in.astype(BF16), w_out.astype(BF16), peer_wq.astype(BF16)
    u_bf = peer_u.astype(BF16)
    vt_bf = jnp.swapaxes(peer_v, 1, 2).astype(BF16)

    meta_p, win_p, ret_p, conv_p, win_s, ret_s, conv_s = [], [], [], [], [], [], []
    for l in range(DEPTH):
        p = _norm_proj(h, norm_mix[l], w_in_bf, l)

        w32 = jnp.concatenate([conv_w[l], jnp.zeros((1, CONV_CH), F32)], axis=0)
        conv_o, conv_tail = _conv_prompt(p, w32, conv_b[l], conv_ln_g[l], conv_ln_b[l])
        ret_o, ret_state = _ret_prompt(p, cos_t, sin_t, ret_tabs, ret_gn_g[l], ret_gn_b[l])
        swa_o, k_normed = _swa_prompt(p, swa_sinks[l], swa_q_norm[l], swa_k_norm[l], bias_tabs)
        s_conv, s_ret, s_swa, new_conv, new_ret, new_win = _sample_mixers(
            p, l, cache_meta_kv, cache_swa_kv, state_ret, state_conv, sample_tabs, w32, conv_b[l],
            conv_ln_g[l], conv_ln_b[l], ret_gn_g[l], ret_gn_b[l], swa_q_norm[l], swa_k_norm[l], swa_sinks[l])
        new_win = new_win.reshape(DEC_BATCH, WINDOW, 2, SWA_KV_HEADS, SWA_HD)

        kbd = jnp.zeros((2 * N_KEYS, 2 * PEER_HALF), F32)
        kbd = kbd.at[:N_KEYS, :PEER_HALF].set(peer_keys[l, 0]).at[N_KEYS:, PEER_HALF:].set(peer_keys[l, 1])
        h1, xn, st = _out_ffn(h, (conv_o, ret_o, swa_o), (s_conv, s_ret, s_swa), w_out_bf,
                              norm_ffn[l], wq_bf, kbd.astype(BF16), l)
        rho, c1, r2, e2 = _topk(st)
        h = _peer(xn, h1, u_bf, vt_bf, rho, c1, r2, e2, l)

        kp = k_normed.reshape(BATCH, LP, SWA_KV_HEADS, SWA_HD)

        def kv_rows(lo, hi, p=p, kp=kp):
            vrows = jnp.stack([p[b * LP + lo:b * LP + hi, C_SV:] for b in range(BATCH)])
            return jnp.stack([kp[:, lo:hi], vrows.reshape(BATCH, hi - lo, SWA_KV_HEADS, SWA_HD)], axis=2)

        meta_p.append(kv_rows(0, N_META))
        win_p.append(kv_rows(L_REAL - WINDOW, L_REAL))
        ret_p.append(ret_state)
        conv_p.append(conv_tail[:, 32 - (CONV_WIDTH - 1):])
        win_s.append(new_win)
        ret_s.append(new_ret)
        conv_s.append(new_conv)

    y_prompt = h[:NP_ROWS].reshape(BATCH, LP, D_MODEL)[:, N_META:L_REAL]
    y_sample = h[NP_ROWS:].reshape(DEC_BATCH, DEC_SEQ, D_MODEL)
    return (y_prompt, y_sample, jnp.stack(meta_p), jnp.stack(win_p), jnp.stack(ret_p), jnp.stack(conv_p),
            jnp.stack(win_s), jnp.stack(ret_s), jnp.stack(conv_s))
```
